```python
import math
import jax, jax.numpy as jnp
from jax import lax
import numpy as np

D_MODEL = 1024
BATCH = 16
SEQ = 2048
DEPTH = 1

HEAD_DIM = 64
N_FOX_HEADS = D_MODEL // 128
N_DIFF_HEADS = D_MODEL // 256
DIFF_V_DIM = 2 * HEAD_DIM
FOX_WIDTH = N_FOX_HEADS * HEAD_DIM
DIFF_QK_WIDTH = N_DIFF_HEADS * 2 * HEAD_DIM
DIFF_WIDTH = N_DIFF_HEADS * DIFF_V_DIM
MIX_WIDTH = FOX_WIDTH + DIFF_WIDTH
FOX_Q0 = 0
FOX_K0 = FOX_Q0 + FOX_WIDTH
FOX_V0 = FOX_K0 + FOX_WIDTH
FOX_F0 = FOX_V0 + FOX_WIDTH
DIFF_Q0 = FOX_F0 + N_FOX_HEADS
DIFF_K0 = DIFF_Q0 + DIFF_QK_WIDTH
DIFF_V0 = DIFF_K0 + DIFF_QK_WIDTH
IN_WIDTH = DIFF_V0 + DIFF_WIDTH

BLOCK_Q = 128
NUM_BUCKETS = 32
MAX_DISTANCE = 128
N_EXPERTS = 32
TOP_K = 4
D_FF = D_MODEL
SWIGLU_ALPHA = 1.702
SWIGLU_LIMIT = 7.0
MOE_BLOCK = 256
PLE_DIM = 256
NORM_EPS = 1e-6
NEG_INF = -1e30

kernel_name = 'hybrid_fox_diffattn_moe_ple'


def rms_norm(x, g):
    xf = x.astype(jnp.float32)
    y = xf * lax.rsqrt(jnp.mean(xf * xf, axis=-1, keepdims=True) + NORM_EPS)
    return (y * g.astype(jnp.float32)).astype(x.dtype)


def causal_mask(q0, q1):
    qpos = jnp.arange(q0, q1, dtype=jnp.int32)
    kpos = jnp.arange(q1, dtype=jnp.int32)
    return kpos[None, :] <= qpos[:, None], qpos[:, None] - kpos[None, :]


def t5_bucket(rel):
    n = jnp.maximum(rel, 0)
    max_exact = NUM_BUCKETS // 2
    nf = jnp.maximum(n, 1).astype(jnp.float32)
    large = max_exact + (jnp.log(nf / max_exact) / math.log(MAX_DISTANCE / max_exact)
                         * (NUM_BUCKETS - max_exact)).astype(jnp.int32)
    large = jnp.minimum(large, NUM_BUCKETS - 1)
    return jnp.where(n < max_exact, n, large)


def fox_attention(q, k, v, logf):
    seq = q.shape[2]
    scale = HEAD_DIM ** -0.5
    c = jnp.cumsum(logf, axis=-1)
    outs = []
    for blk in range(seq // BLOCK_Q):
        q0, q1 = blk * BLOCK_Q, (blk + 1) * BLOCK_Q
        mask, _ = causal_mask(q0, q1)
        s = jnp.einsum('bhqd,bhkd->bhqk', q[:, :, q0:q1], k[:, :, :q1]).astype(jnp.float32) * scale
        s = s + c[:, :, q0:q1, None] - c[:, :, None, :q1]
        s = jnp.where(mask, s, NEG_INF)
        pr = jax.nn.softmax(s, axis=-1).astype(v.dtype)
        outs.append(jnp.einsum('bhqk,bhkd->bhqd', pr, v[:, :, :q1]))
    return jnp.concatenate(outs, axis=2)


def diff_attention(q1, q2, k1, k2, v, rel_table, lam):
    seq = q1.shape[2]
    scale = HEAD_DIM ** -0.5
    outs = []
    for blk in range(seq // BLOCK_Q):
        q0, qe = blk * BLOCK_Q, (blk + 1) * BLOCK_Q
        mask, rel = causal_mask(q0, qe)
        bias = jnp.take(rel_table, t5_bucket(rel), axis=0).astype(jnp.float32)
        bias = jnp.transpose(bias, (2, 0, 1))[None]
        s1 = jnp.einsum('bhqd,bhkd->bhqk', q1[:, :, q0:qe], k1[:, :, :qe]).astype(jnp.float32) * scale + bias
        s2 = jnp.einsum('bhqd,bhkd->bhqk', q2[:, :, q0:qe], k2[:, :, :qe]).astype(jnp.float32) * scale + bias
        a1 = jax.nn.softmax(jnp.where(mask, s1, NEG_INF), axis=-1)
        a2 = jax.nn.softmax(jnp.where(mask, s2, NEG_INF), axis=-1)
        pr = (a1 - lam * a2).astype(v.dtype)
        outs.append(jnp.einsum('bhqk,bhkd->bhqd', pr, v[:, :, :qe]))
    return jnp.concatenate(outs, axis=2)


def clamped_swiglu(h):
    x_glu = jnp.minimum(h[..., ::2], SWIGLU_LIMIT)
    x_lin = jnp.clip(h[..., 1::2], -SWIGLU_LIMIT, SWIGLU_LIMIT)
    return x_glu * jax.nn.sigmoid(SWIGLU_ALPHA * x_glu) * (x_lin + 1.0)


def moe(xm, w_router, b_router, w1, b1, w2, b2):
    bsz, seq, d = xm.shape
    n_tok = bsz * seq
    xs = xm.reshape(n_tok, d)
    logits = (xs @ w_router + b_router).astype(jnp.float32)
    top_v, top_i = lax.top_k(logits, TOP_K)
    gates = jax.nn.softmax(top_v, axis=-1)
    m = n_tok * TOP_K
    e_flat = top_i.reshape(m)
    g_flat = gates.reshape(m)
    tok_flat = jnp.broadcast_to(jnp.arange(n_tok, dtype=jnp.int32)[:, None], (n_tok, TOP_K)).reshape(m)
    order = jnp.argsort(e_flat, stable=True)
    e_sorted = e_flat[order]
    counts = jnp.bincount(e_flat, length=N_EXPERTS)
    starts = jnp.cumsum(counts) - counts
    padded = ((counts + MOE_BLOCK - 1) // MOE_BLOCK) * MOE_BLOCK
    pad_ends = jnp.cumsum(padded)
    pad_starts = pad_ends - padded
    dest = pad_starts[e_sorted] + (jnp.arange(m, dtype=jnp.int32) - starts[e_sorted])
    n_blocks = (m + N_EXPERTS * MOE_BLOCK + MOE_BLOCK - 1) // MOE_BLOCK
    p_rows = n_blocks * MOE_BLOCK
    buf_tok = jnp.zeros((p_rows,), jnp.int32).at[dest].set(tok_flat[order])
    buf_gate = jnp.zeros((p_rows,), jnp.float32).at[dest].set(g_flat[order])
    block_start = jnp.arange(n_blocks, dtype=jnp.int32) * MOE_BLOCK
    block_expert = jnp.minimum(jnp.searchsorted(pad_ends, block_start, side='right'), N_EXPERTS - 1)

    def block_fn(args):
        tok, e = args
        xb = xs[tok]
        hb = clamped_swiglu(xb @ w1[e] + b1[e])
        return hb @ w2[e] + b2[e]

    outs = lax.map(block_fn, (buf_tok.reshape(n_blocks, MOE_BLOCK), block_expert))
    outs = outs.reshape(p_rows, d) * buf_gate[:, None].astype(outs.dtype)
    y = jnp.zeros((n_tok, d), outs.dtype).at[buf_tok].add(outs)
    return y.reshape(bsz, seq, d).astype(xm.dtype)


def setup_inputs(seed: int = 0) -> dict:
    key = jax.random.key(seed)
    ks = jax.random.split(key, 26)
    f32 = jnp.float32
    nrm = lambda k, shape, s: jax.random.normal(k, shape, f32) * s
    gain = lambda k, shape: 1.0 + 0.02 * jax.random.normal(k, shape, f32)
    return {
        'x': nrm(ks[0], (BATCH, SEQ, D_MODEL), 1.0),
        'p': nrm(ks[1], (DEPTH, BATCH, SEQ, PLE_DIM), 1.0),
        'rel_bias': nrm(ks[2], (NUM_BUCKETS, N_DIFF_HEADS), 0.3),
        'g_attn': gain(ks[3], (DEPTH, D_MODEL)),
        'w_in': nrm(ks[4], (DEPTH, D_MODEL, IN_WIDTH), D_MODEL ** -0.5),
        'b_f': 3.0 + 0.5 * jax.random.normal(ks[5], (DEPTH, N_FOX_HEADS), f32),
        'g_fox_q': gain(ks[6], (DEPTH, HEAD_DIM)),
        'g_fox_k': gain(ks[7], (DEPTH, HEAD_DIM)),
        'g_diff_q': gain(ks[8], (DEPTH, HEAD_DIM)),
        'g_diff_k': gain(ks[9], (DEPTH, HEAD_DIM)),
        'lambda_q1': nrm(ks[10], (DEPTH, HEAD_DIM), 0.1),
        'lambda_k1': nrm(ks[11], (DEPTH, HEAD_DIM), 0.1),
        'lambda_q2': nrm(ks[12], (DEPTH, HEAD_DIM), 0.1),
        'lambda_k2': nrm(ks[13], (DEPTH, HEAD_DIM), 0.1),
        'g_subln': gain(ks[14], (DEPTH, DIFF_V_DIM)),
        'w_out': nrm(ks[15], (DEPTH, MIX_WIDTH, D_MODEL), MIX_WIDTH ** -0.5),
        'g_mlp': gain(ks[16], (DEPTH, D_MODEL)),
        'w_router': nrm(ks[17], (DEPTH, D_MODEL, N_EXPERTS), D_MODEL ** -0.5),
        'b_router': nrm(ks[18], (DEPTH, N_EXPERTS), 0.01),
        'w1': nrm(ks[19], (DEPTH, N_EXPERTS, D_MODEL, 2 * D_FF), D_MODEL ** -0.5),
        'b1': nrm(ks[20], (DEPTH, N_EXPERTS, 2 * D_FF), 0.01),
        'w2': nrm(ks[21], (DEPTH, N_EXPERTS, D_FF, D_MODEL), D_FF ** -0.5),
        'b2': nrm(ks[22], (DEPTH, N_EXPERTS, D_MODEL), 0.01),
        'g_ple': gain(ks[23], (DEPTH, D_MODEL)),
        'w_ple_gate': nrm(ks[24], (DEPTH, D_MODEL, D_MODEL), D_MODEL ** -0.5),
        'w_ple_proj': nrm(ks[25], (DEPTH, PLE_DIM, D_MODEL), PLE_DIM ** -0.5),
    }


def reference(x, p, rel_bias, g_attn, w_in, b_f, g_fox_q, g_fox_k, g_diff_q, g_diff_k,
              lambda_q1, lambda_k1, lambda_q2, lambda_k2, g_subln, w_out, g_mlp,
              w_router, b_router, w1, b1, w2, b2, g_ple, w_ple_gate, w_ple_proj):
    bsz, seq, _ = x.shape
    to_heads = lambda t, nh, hd: t.reshape(bsz, seq, nh, hd).transpose(0, 2, 1, 3)
    h = x
    for i in range(DEPTH):
        a = rms_norm(h, g_attn[i])
        proj = a @ w_in[i]
        fq = rms_norm(to_heads(proj[..., FOX_Q0:FOX_K0], N_FOX_HEADS, HEAD_DIM), g_fox_q[i])
        fk = rms_norm(to_heads(proj[..., FOX_K0:FOX_V0], N_FOX_HEADS, HEAD_DIM), g_fox_k[i])
        fv = to_heads(proj[..., FOX_V0:FOX_F0], N_FOX_HEADS, HEAD_DIM)
        f_logit = (proj[..., FOX_F0:DIFF_Q0] + b_f[i]).astype(jnp.float32)
        logf = jnp.transpose(jax.nn.log_sigmoid(f_logit), (0, 2, 1))
        fox_out = fox_attention(fq, fk, fv, logf)
        fox_out = fox_out.transpose(0, 2, 1, 3).reshape(bsz, seq, FOX_WIDTH)
        dq = proj[..., DIFF_Q0:DIFF_K0].reshape(bsz, seq, N_DIFF_HEADS, 2, HEAD_DIM)
        dk = proj[..., DIFF_K0:DIFF_V0].reshape(bsz, seq, N_DIFF_HEADS, 2, HEAD_DIM)
        dq = rms_norm(dq, g_diff_q[i]).transpose(0, 2, 3, 1, 4)
        dk = rms_norm(dk, g_diff_k[i]).transpose(0, 2, 3, 1, 4)
        dv = to_heads(proj[..., DIFF_V0:IN_WIDTH], N_DIFF_HEADS, DIFF_V_DIM)
        lam_init = 0.8 - 0.6 * math.exp(-0.3 * i)
        lam = (jnp.exp(jnp.sum(lambda_q1[i].astype(jnp.float32) * lambda_k1[i].astype(jnp.float32)))
               - jnp.exp(jnp.sum(lambda_q2[i].astype(jnp.float32) * lambda_k2[i].astype(jnp.float32)))
               + lam_init)
        diff_out = diff_attention(dq[:, :, 0], dq[:, :, 1], dk[:, :, 0], dk[:, :, 1], dv, rel_bias, lam)
        diff_out = rms_norm(diff_out, g_subln[i]) * (1.0 - lam_init)
        diff_out = diff_out.transpose(0, 2, 1, 3).reshape(bsz, seq, DIFF_WIDTH)
        mixed = jnp.concatenate([fox_out, diff_out.astype(fox_out.dtype)], axis=-1)
        h = h + mixed @ w_out[i]
        h = h + moe(rms_norm(h, g_mlp[i]), w_router[i], b_router[i], w1[i], b1[i], w2[i], b2[i])
        gate = jax.nn.sigmoid(rms_norm(h, g_ple[i]) @ w_ple_gate[i])
        h = h + gate * (p[i] @ w_ple_proj[i])
    return h
```

```python
import functools
import math

import jax
import jax.numpy as jnp
from jax import lax
from jax.experimental import pallas as pl
from jax.experimental.pallas import tpu as pltpu

F32 = jnp.float32
BF16 = jnp.bfloat16

HEAD_DIM = 64
N_FOX_HEADS = 8
N_DIFF_HEADS = 4
GROUP = 512
LANES = 128
NUM_BUCKETS = 32
MAX_DISTANCE = 128
N_EXPERTS = 32
TOP_K = 4
SWIGLU_ALPHA = 1.702
SWIGLU_LIMIT = 7.0
MOE_BLOCK = 256
NORM_EPS = 1e-6
NEG_INF = -1e30

TM = 512
TQ = 256
TK = 256
VMEM_LIMIT = 48 * 1024 * 1024


def _params(n_axes):
    return pltpu.CompilerParams(dimension_semantics=("arbitrary",) * n_axes,
                                vmem_limit_bytes=VMEM_LIMIT)


def _dot(a, b):
    return jnp.dot(a, b, preferred_element_type=F32)


def _dot_nt(a, b):
    return lax.dot_general(a, b, (((1,), (1,)), ((), ())), preferred_element_type=F32)


def _split3(v):
    hi = v.astype(BF16)
    r = v - hi.astype(F32)
    mid = r.astype(BF16)
    lo = (r - mid.astype(F32)).astype(BF16)
    return hi, mid, lo


def _inproj_kernel(x_ref, g_ref, wfq_ref, wfk_ref, wfv_ref, wdq_ref, wdk_ref, wdv_ref, wf_ref, bf_ref,
                   gfq_ref, gfk_ref, gdq_ref, gdk_ref, seg_ref, tri_ref,
                   fq_ref, fk_ref, fv_ref, dq_ref, dk_ref, dv_ref, c_ref, carry_ref):
    @pl.when(pl.program_id(1) == 0)
    def _():
        carry_ref[...] = jnp.zeros_like(carry_ref)

    xf = x_ref[...]
    a = (xf * lax.rsqrt(jnp.mean(xf * xf, axis=-1, keepdims=True) + NORM_EPS) * g_ref[...]).astype(BF16)
    seg = seg_ref[...]

    def q_group(w_ref, gain_ref, out_ref):
        acc = _dot_nt(w_ref[...], a)
        ms = _dot(seg, (acc * acc).astype(BF16))
        out_ref[...] = (acc * lax.rsqrt(ms + NORM_EPS) * gain_ref[...]).astype(BF16)

    def k_group(w_ref, gain_ref, out_ref):
        acc = _dot(a, w_ref[...])
        ms = _dot((acc * acc).astype(BF16), seg)
        out_ref[...] = (acc * lax.rsqrt(ms + NORM_EPS) * gain_ref[...]).astype(BF16)

    def v_group(w_ref, out_ref):
        acc = _dot_nt(w_ref[...], a).astype(BF16)
        for hb in range(GROUP // LANES):
            for j in range(TM // TK):
                out_ref[hb, j] = acc[hb * LANES:(hb + 1) * LANES, j * TK:(j + 1) * TK]

    q_group(wfq_ref, gfq_ref, fq_ref)
    k_group(wfk_ref, gfk_ref, fk_ref)
    v_group(wfv_ref, fv_ref)
    q_group(wdq_ref, gdq_ref, dq_ref)
    k_group(wdk_ref, gdk_ref, dk_ref)
    v_group(wdv_ref, dv_ref)

    fl = _dot(a, wf_ref[...]) + bf_ref[...]
    logf = jnp.minimum(fl, 0.0) - jnp.log1p(jnp.exp(-jnp.abs(fl)))
    hi, mid, lo = _split3(logf)
    tri = tri_ref[...]
    cs = _dot(tri, hi) + _dot(tri, mid) + _dot(tri, lo) + carry_ref[...]
    c_ref[...] = cs
    carry_ref[...] = cs[TM - 1:TM, :]


def _two_map_queries(qT_blk):
    row = lax.broadcasted_iota(jnp.int32, qT_blk.shape, 0)
    zero = jnp.zeros_like(qT_blk)
    return jnp.concatenate([jnp.where(row < HEAD_DIM, qT_blk, zero),
                            jnp.where(row >= HEAD_DIM, qT_blk, zero)], axis=1)


def _softmax_step(sT, vT_blk, carry):
    m, l, acc = carry
    m_new = jnp.maximum(m, jnp.max(sT, axis=0, keepdims=True))
    alpha = jnp.exp(m - m_new)
    pT = jnp.exp(sT - m_new)
    l = alpha * l + jnp.sum(pT, axis=0, keepdims=True)
    acc = alpha * acc + _dot(vT_blk, pT.astype(BF16))
    return m_new, l, acc


def _causal_keep():
    key = lax.broadcasted_iota(jnp.int32, (TK, 2 * TQ), 0)
    col = lax.broadcasted_iota(jnp.int32, (TK, 2 * TQ), 1)
    qry = jnp.where(col >= TQ, col - TQ, col)
    return key <= qry


def _init_carry():
    return (jnp.full((1, 2 * TQ), NEG_INF, F32), jnp.zeros((1, 2 * TQ), F32),
            jnp.zeros((LANES, 2 * TQ), F32))


def _fox_kernel(qT_ref, k_ref, vT_ref, c_ref, o_ref, ca_ref, cb_ref):
    hb = pl.program_id(1)
    seq = k_ref.shape[0]
    lane = lax.broadcasted_iota(jnp.int32, (seq, LANES), 1)
    cblk = c_ref[...]
    ca_ref[...] = jnp.sum(jnp.where(lane == 2 * hb, cblk, 0.0), axis=1, keepdims=True)
    cb_ref[...] = jnp.sum(jnp.where(lane == 2 * hb + 1, cblk, 0.0), axis=1, keepdims=True)
    keep = _causal_keep()
    row = lax.broadcasted_iota(jnp.int32, (LANES, TQ), 0)

    for qi in range(seq // TQ):
        q2T = _two_map_queries(qT_ref[:, qi * TQ:(qi + 1) * TQ])
        ca0 = ca_ref[qi * TQ:qi * TQ + 1, :]
        cb0 = cb_ref[qi * TQ:qi * TQ + 1, :]

        def scores(kj):
            r0 = pl.multiple_of(kj * TK, TK)
            sT = _dot(k_ref[pl.ds(r0, TK), :], q2T)
            da = ca0 - ca_ref[pl.ds(r0, TK), :]
            db = cb0 - cb_ref[pl.ds(r0, TK), :]
            return jnp.concatenate([sT[:, :TQ] + da, sT[:, TQ:] + db], axis=1)

        def body(kj, carry):
            return _softmax_step(scores(kj), vT_ref[kj], carry)

        carry = lax.fori_loop(0, qi, body, _init_carry())
        sT = jnp.where(keep, scores(qi), NEG_INF)
        m, l, acc = _softmax_step(sT, vT_ref[qi], carry)
        oT = acc / l
        o = jnp.where(row < HEAD_DIM, oT[:, :TQ], oT[:, TQ:])
        o_ref[qi * TQ:(qi + 1) * TQ, :] = o.T.astype(BF16)


def _diff_kernel(lam_init, qT_ref, k_ref, vT_ref, bias_ref, lam_ref, gsub_ref, o_ref):
    seq = k_ref.shape[0]
    keep = _causal_keep()
    lp = lam_ref[...]
    lam = (jnp.exp(jnp.sum(lp[0:1] * lp[1:2], axis=1, keepdims=True))
           - jnp.exp(jnp.sum(lp[2:3] * lp[3:4], axis=1, keepdims=True)) + lam_init)

    for qi in range(seq // TQ):
        q2T = _two_map_queries(qT_ref[:, qi * TQ:(qi + 1) * TQ])

        def scores(kj):
            r0 = pl.multiple_of(kj * TK, TK)
            return _dot(k_ref[pl.ds(r0, TK), :], q2T)

        def body(kj, carry):
            return _softmax_step(scores(kj), vT_ref[kj], carry)

        carry = lax.fori_loop(0, max(qi - 1, 0), body, _init_carry())
        if qi >= 1:
            b = bias_ref[1]
            carry = _softmax_step(scores(qi - 1) + jnp.concatenate([b, b], axis=1), vT_ref[qi - 1], carry)
        b = bias_ref[0]
        sT = jnp.where(keep, scores(qi) + jnp.concatenate([b, b], axis=1), NEG_INF)
        m, l, acc = _softmax_step(sT, vT_ref[qi], carry)
        oT = acc / l
        o = oT[:, :TQ] - lam * oT[:, TQ:]
        y = o * lax.rsqrt(jnp.mean(o * o, axis=0, keepdims=True) + NORM_EPS) * gsub_ref[...]
        o_ref[qi * TQ:(qi + 1) * TQ, :] = y.T.astype(BF16)


def _mix_kernel(x_ref, fox_ref, dif_ref, wa_ref, wb_ref, g_ref, wr_hi_ref, wr_lo_ref, br_ref, tri_ref,
                h_ref, xm_ref, topi_ref, gate_ref, rank_ref, cnt_ref, carry_ref):
    @pl.when(pl.program_id(0) == 0)
    def _():
        carry_ref[...] = jnp.zeros_like(carry_ref)

    h = x_ref[...] + _dot(fox_ref[...], wa_ref[...]) + _dot(dif_ref[...], wb_ref[...])
    h_ref[...] = h
    xm = h * lax.rsqrt(jnp.mean(h * h, axis=-1, keepdims=True) + NORM_EPS) * g_ref[...]
    xm_ref[...] = xm.astype(BF16)

    x_hi = xm.astype(BF16)
    x_lo = (xm - x_hi.astype(F32)).astype(BF16)
    logits = (_dot(x_hi, wr_hi_ref[...]) + _dot(x_hi, wr_lo_ref[...]) + _dot(x_lo, wr_hi_ref[...])
              + br_ref[...])

    lane = lax.broadcasted_iota(jnp.int32, logits.shape, 1)
    vals, sels, idxs = [], [], []
    for _ in range(TOP_K):
        mx = jnp.max(logits, axis=1, keepdims=True)
        idx = jnp.min(jnp.where(logits == mx, lane, LANES), axis=1, keepdims=True)
        sel = lane == idx
        logits = jnp.where(sel, -jnp.inf, logits)
        vals.append(mx)
        sels.append(sel)
        idxs.append(idx)
    exps = [jnp.exp(v - vals[0]) for v in vals]
    denom = exps[0] + exps[1] + exps[2] + exps[3]

    multi_hot = sum(s.astype(F32) for s in sels)
    before = _dot(tri_ref[...], multi_hot.astype(BF16)) + carry_ref[...]
    topi = jnp.zeros(logits.shape, jnp.int32)
    gate = jnp.zeros(logits.shape, F32)
    rank = jnp.zeros(logits.shape, F32)
    for k in range(TOP_K):
        rk = jnp.sum(jnp.where(sels[k], before, 0.0), axis=1, keepdims=True)
        topi = jnp.where(lane == k, idxs[k], topi)
        gate = jnp.where(lane == k, exps[k] / denom, gate)
        rank = jnp.where(lane == k, rk, rank)
    topi_ref[...] = topi
    gate_ref[...] = gate
    rank_ref[...] = rank.astype(jnp.int32)
    carry_ref[...] = carry_ref[...] + jnp.sum(multi_hot, axis=0, keepdims=True)
    cnt_ref[...] = carry_ref[...]


def _expert_kernel(be_ref, nu_ref, xs_ref, w1_ref, b1_ref, w2_ref, b2_ref, o_ref):
    i = pl.program_id(0)

    @pl.when(i < nu_ref[0])
    def _():
        h = _dot(xs_ref[...], w1_ref[...]) + b1_ref[...]
        acts = []
        for c in range(h.shape[1] // (2 * LANES)):
            glu = jnp.minimum(h[:, 2 * c * LANES:(2 * c + 1) * LANES], SWIGLU_LIMIT)
            lin = jnp.clip(h[:, (2 * c + 1) * LANES:(2 * c + 2) * LANES], -SWIGLU_LIMIT, SWIGLU_LIMIT)
            acts.append((glu * jax.nn.sigmoid(SWIGLU_ALPHA * glu) * (lin + 1.0)).astype(BF16))
        act = jnp.concatenate(acts, axis=1)
        o_ref[...] = (_dot(act, w2_ref[...]) + b2_ref[...]).astype(o_ref.dtype)

    @pl.when(i >= nu_ref[0])
    def _():
        o_ref[...] = jnp.zeros_like(o_ref)


def _final_kernel(h_ref, y_ref, gate_ref, g_ref, wg_ref, p_ref, wp_ref, o_ref):
    h = h_ref[...]
    gates = gate_ref[...]
    for k in range(TOP_K):
        h = h + gates[:, k:k + 1] * y_ref[k].astype(F32)
    n = (h * lax.rsqrt(jnp.mean(h * h, axis=-1, keepdims=True) + NORM_EPS) * g_ref[...]).astype(BF16)
    gate = jax.nn.sigmoid(_dot(n, wg_ref[...]))
    o_ref[...] = h + gate * _dot(p_ref[...].astype(BF16), wp_ref[...])


def _t5_bucket(rel):
    n = jnp.maximum(rel, 0)
    max_exact = NUM_BUCKETS // 2
    nf = jnp.maximum(n, 1).astype(F32)
    large = max_exact + (jnp.log(nf / max_exact) / math.log(MAX_DISTANCE / max_exact)
                         * (NUM_BUCKETS - max_exact)).astype(jnp.int32)
    large = jnp.minimum(large, NUM_BUCKETS - 1)
    return jnp.where(n < max_exact, n, large)


def _full(shape):
    return pl.BlockSpec(shape, lambda *_: (0,) * len(shape))


def _layer(i, h_in, p_i, rel_bias, prm):
    bsz, seq, d = h_in.shape
    n_tok = bsz * seq
    ns = seq // TM
    nk = seq // TK
    x2 = h_in.reshape(n_tok, d)

    w_in = prm['w_in']
    fox_w = N_FOX_HEADS * HEAD_DIM
    offs = [0, fox_w, 2 * fox_w, 3 * fox_w, 3 * fox_w + N_FOX_HEADS]
    offs += [offs[4] + GROUP, offs[4] + 2 * GROUP, offs[4] + 3 * GROUP]
    col = lambda a, b: w_in[:, a:b].astype(BF16)
    wfqT, wfk, wfvT = col(offs[0], offs[1]).T, col(offs[1], offs[2]), col(offs[2], offs[3]).T
    wdqT, wdk, wdvT = col(offs[4], offs[5]).T, col(offs[5], offs[6]), col(offs[6], offs[7]).T
    wf = jnp.zeros((d, LANES), BF16).at[:, :N_FOX_HEADS].set(col(offs[3], offs[4]))
    bf = jnp.zeros((1, LANES), F32).at[0, :N_FOX_HEADS].set(prm['b_f'])
    scale = HEAD_DIM ** -0.5
    rep = GROUP // HEAD_DIM
    gfqT = jnp.broadcast_to((jnp.tile(prm['g_fox_q'], rep) * scale)[:, None], (GROUP, TM))
    gdqT = jnp.broadcast_to((jnp.tile(prm['g_diff_q'], rep) * scale)[:, None], (GROUP, TM))
    gfk = jnp.tile(prm['g_fox_k'], rep)[None, :]
    gdk = jnp.tile(prm['g_diff_k'], rep)[None, :]
    gi = jnp.arange(GROUP) // HEAD_DIM
    seg = jnp.where(gi[:, None] == gi[None, :], 1.0 / HEAD_DIM, 0.0).astype(BF16)
    ti = jnp.arange(TM)
    tri_incl = (ti[None, :] <= ti[:, None]).astype(BF16)
    tri_strict = (ti[None, :] < ti[:, None]).astype(BF16)

    w_spec_t = _full((GROUP, d))
    w_spec = _full((d, GROUP))
    qT_shape = jax.ShapeDtypeStruct((bsz, GROUP, seq), BF16)
    k_shape = jax.ShapeDtypeStruct((n_tok, GROUP), BF16)
    vT_shape = jax.ShapeDtypeStruct((bsz, GROUP // LANES, nk, LANES, TK), BF16)
    qT_spec = pl.BlockSpec((None, GROUP, TM), lambda b, s: (b, 0, s))
    k_spec = pl.BlockSpec((TM, GROUP), lambda b, s: (b * ns + s, 0))
    vT_spec = pl.BlockSpec((None, GROUP // LANES, TM // TK, LANES, TK), lambda b, s: (b, 0, s, 0, 0))
    fqT, fk, fvT, dqT, dk, dvT, csum = pl.pallas_call(
        _inproj_kernel,
        grid=(bsz, ns),
        in_specs=[pl.BlockSpec((TM, d), lambda b, s: (b * ns + s, 0)), _full((1, d)),
                  w_spec_t, w_spec, w_spec_t, w_spec_t, w_spec, w_spec_t, _full((d, LANES)), _full((1, LANES)),
                  _full((GROUP, TM)), _full((1, GROUP)), _full((GROUP, TM)), _full((1, GROUP)),
                  _full((GROUP, GROUP)), _full((TM, TM))],
        out_specs=[qT_spec, k_spec, vT_spec, qT_spec, k_spec, vT_spec,
                   pl.BlockSpec((TM, LANES), lambda b, s: (b * ns + s, 0))],
        out_shape=[qT_shape, k_shape, vT_shape, qT_shape, k_shape, vT_shape,
                   jax.ShapeDtypeStruct((n_tok, LANES), F32)],
        scratch_shapes=[pltpu.VMEM((1, LANES), F32)],
        compiler_params=_params(2),
        name="inproj",
    )(x2, prm['g_attn'][None, :], wfqT, wfk, wfvT, wdqT, wdk, wdvT, wf, bf, gfqT, gfk, gdqT, gdk, seg, tri_incl)

    n_hb = GROUP // LANES
    att_q = pl.BlockSpec((None, LANES, seq), lambda b, h: (b, h, 0))
    att_k = pl.BlockSpec((seq, LANES), lambda b, h: (b, h))
    att_v = pl.BlockSpec((None, None, nk, LANES, TK), lambda b, h: (b, h, 0, 0, 0))
    att_o = pl.BlockSpec((seq, LANES), lambda b, h: (b, h))
    fox_out = pl.pallas_call(
        _fox_kernel,
        grid=(bsz, n_hb),
        in_specs=[att_q, att_k, att_v, pl.BlockSpec((seq, LANES), lambda b, h: (b, 0))],
        out_specs=att_o,
        out_shape=jax.ShapeDtypeStruct((n_tok, GROUP), BF16),
        scratch_shapes=[pltpu.VMEM((seq, 1), F32), pltpu.VMEM((seq, 1), F32)],
        compiler_params=_params(2),
        name="fox_attention",
    )(fqT, fk, fvT, csum)

    kpos = jnp.arange(TK)[:, None]
    qpos = jnp.arange(TQ)[None, :]
    rel = jnp.stack([qpos - kpos, qpos - kpos + TK])
    table = rel_bias.astype(F32) - rel_bias[NUM_BUCKETS - 1].astype(F32)[None, :]
    biasT = jnp.transpose(jnp.take(table, _t5_bucket(rel), axis=0), (3, 0, 1, 2))
    lam_init = 0.8 - 0.6 * math.exp(-0.3 * i)
    lam_p = jnp.stack([prm['lambda_q1'], prm['lambda_k1'], prm['lambda_q2'], prm['lambda_k2']]).astype(F32)
    gsubT = jnp.broadcast_to((prm['g_subln'] * (1.0 - lam_init))[:, None], (LANES, TQ))
    diff_out = pl.pallas_call(
        functools.partial(_diff_kernel, lam_init),
        grid=(bsz, N_DIFF_HEADS),
        in_specs=[att_q, att_k, att_v, pl.BlockSpec((None, 2, TK, TQ), lambda b, h: (h, 0, 0, 0)),
                  _full((4, HEAD_DIM)), _full((LANES, TQ))],
        out_specs=att_o,
        out_shape=jax.ShapeDtypeStruct((n_tok, GROUP), BF16),
        compiler_params=_params(2),
        name="diff_attention",
    )(dqT, dk, dvT, biasT, lam_p, gsubT)

    w_out = prm['w_out'].astype(BF16)
    wr = jnp.zeros((d, LANES), F32).at[:, :N_EXPERTS].set(prm['w_router'])
    wr_hi = wr.astype(BF16)
    wr_lo = (wr - wr_hi.astype(F32)).astype(BF16)
    br = jnp.full((1, LANES), NEG_INF, F32).at[0, :N_EXPERTS].set(prm['b_router'])
    row_spec = lambda w: pl.BlockSpec((TM, w), lambda t: (t, 0))
    h1, xm, topi, gates, rank, counts = pl.pallas_call(
        _mix_kernel,
        grid=(n_tok // TM,),
        in_specs=[row_spec(d), row_spec(GROUP), row_spec(GROUP), _full((GROUP, d)), _full((GROUP, d)),
                  _full((1, d)), _full((d, LANES)), _full((d, LANES)), _full((1, LANES)), _full((TM, TM))],
        out_specs=[row_spec(d), row_spec(d), row_spec(LANES), row_spec(LANES), row_spec(LANES),
                   _full((1, LANES))],
        out_shape=[jax.ShapeDtypeStruct((n_tok, d), F32), jax.ShapeDtypeStruct((n_tok, d), BF16),
                   jax.ShapeDtypeStruct((n_tok, LANES), jnp.int32), jax.ShapeDtypeStruct((n_tok, LANES), F32),
                   jax.ShapeDtypeStruct((n_tok, LANES), jnp.int32), jax.ShapeDtypeStruct((1, LANES), F32)],
        scratch_shapes=[pltpu.VMEM((1, LANES), F32)],
        compiler_params=_params(1),
        name="mix_router",
    )(x2, fox_out, diff_out, w_out[:GROUP], w_out[GROUP:], prm['g_mlp'][None, :], wr_hi, wr_lo, br, tri_strict)

    m = n_tok * TOP_K
    n_blocks = (m + N_EXPERTS * MOE_BLOCK + MOE_BLOCK - 1) // MOE_BLOCK
    p_rows = n_blocks * MOE_BLOCK
    cnt = counts[0, :N_EXPERTS].astype(jnp.int32)
    padded = ((cnt + MOE_BLOCK - 1) // MOE_BLOCK) * MOE_BLOCK
    pad_ends = jnp.cumsum(padded)
    pad_starts = pad_ends - padded
    top_i = topi[:, :TOP_K]
    dest = pad_starts[top_i] + rank[:, :TOP_K]
    tok = jnp.broadcast_to(jnp.arange(n_tok, dtype=jnp.int32)[:, None], (n_tok, TOP_K))
    buf_tok = jnp.zeros((p_rows,), jnp.int32).at[dest.reshape(m)].set(tok.reshape(m))
    block_start = jnp.arange(n_blocks, dtype=jnp.int32) * MOE_BLOCK
    block_expert = jnp.minimum(jnp.searchsorted(pad_ends, block_start, side='right'),
                               N_EXPERTS - 1).astype(jnp.int32)
    n_used = (pad_ends[-1] // MOE_BLOCK).astype(jnp.int32)[None]

    d_ff = prm['w2'].shape[1]
    n_grp = 2 * d_ff // (2 * LANES)
    w1p = prm['w1'].reshape(N_EXPERTS, d, n_grp, LANES, 2).transpose(0, 1, 2, 4, 3)
    w1p = w1p.reshape(N_EXPERTS, d, 2 * d_ff).astype(BF16)
    b1p = prm['b1'].reshape(N_EXPERTS, n_grp, LANES, 2).transpose(0, 1, 3, 2).reshape(N_EXPERTS, 1, 2 * d_ff)
    w2b = prm['w2'].astype(BF16)
    b2 = prm['b2'].reshape(N_EXPERTS, 1, d)
    xs = jnp.take(xm, buf_tok, axis=0)
    ys = pl.pallas_call(
        _expert_kernel,
        grid_spec=pltpu.PrefetchScalarGridSpec(
            num_scalar_prefetch=2,
            grid=(n_blocks,),
            in_specs=[pl.BlockSpec((MOE_BLOCK, d), lambda t, be, nu: (t, 0)),
                      pl.BlockSpec((None, d, 2 * d_ff), lambda t, be, nu: (be[t], 0, 0)),
                      pl.BlockSpec((None, 1, 2 * d_ff), lambda t, be, nu: (be[t], 0, 0)),
                      pl.BlockSpec((None, d_ff, d), lambda t, be, nu: (be[t], 0, 0)),
                      pl.BlockSpec((None, 1, d), lambda t, be, nu: (be[t], 0, 0))],
            out_specs=pl.BlockSpec((MOE_BLOCK, d), lambda t, be, nu: (t, 0)),
        ),
        out_shape=jax.ShapeDtypeStruct((p_rows, d), BF16),
        compiler_params=_params(1),
        name="experts",
    )(block_expert, n_used, xs, w1p, b1p, w2b, b2)

    yk = jnp.take(ys, dest.T, axis=0)
    ple = p_i.shape[-1]
    out = pl.pallas_call(
        _final_kernel,
        grid=(n_tok // TM,),
        in_specs=[row_spec(d), pl.BlockSpec((TOP_K, TM, d), lambda t: (0, t, 0)), row_spec(LANES),
                  _full((1, d)), _full((d, d)), row_spec(ple), _full((ple, d))],
        out_specs=row_spec(d),
        out_shape=jax.ShapeDtypeStruct((n_tok, d), F32),
        compiler_params=_params(1),
        name="combine_ple",
    )(h1, yk, gates, prm['g_ple'][None, :], prm['w_ple_gate'].astype(BF16), p_i.reshape(n_tok, ple),
      prm['w_ple_proj'].astype(BF16))
    return out.reshape(bsz, seq, d)


def kernel(x, p, rel_bias, g_attn, w_in, b_f, g_fox_q, g_fox_k, g_diff_q, g_diff_k, lambda_q1, lambda_k1,
           lambda_q2, lambda_k2, g_subln, w_out, g_mlp, w_router, b_router, w1, b1, w2, b2, g_ple,
           w_ple_gate, w_ple_proj):
    stacked = dict(g_attn=g_attn, w_in=w_in, b_f=b_f, g_fox_q=g_fox_q, g_fox_k=g_fox_k, g_diff_q=g_diff_q,
                   g_diff_k=g_diff_k, lambda_q1=lambda_q1, lambda_k1=lambda_k1, lambda_q2=lambda_q2,
                   lambda_k2=lambda_k2, g_subln=g_subln, w_out=w_out, g_mlp=g_mlp, w_router=w_router,
                   b_router=b_router, w1=w1, b1=b1, w2=w2, b2=b2, g_ple=g_ple, w_ple_gate=w_ple_gate,
                   w_ple_proj=w_ple_proj)
    h = x
    for i in range(p.shape[0]):
        h = _layer(i, h, p[i], rel_bias, {name: v[i] for name, v in stacked.items()})
    return h
```

```python
import functools
import math

import jax
import jax.numpy as jnp
from jax import lax
from jax.experimental import pallas as pl
from jax.experimental.pallas import tpu as pltpu

F32 = jnp.float32
BF16 = jnp.bfloat16

HEAD_DIM = 64
N_FOX_HEADS = 8
N_DIFF_HEADS = 4
GROUP = 512
LANES = 128
NUM_BUCKETS = 32
MAX_DISTANCE = 128
N_EXPERTS = 32
TOP_K = 4
SWIGLU_ALPHA = 1.702
SWIGLU_LIMIT = 7.0
MOE_BLOCK = 256
NORM_EPS = 1e-6
NEG_INF = -1e30
LOG2E = math.log2(math.e)
N_SPLIT = 3

TM = 512
TQ = 256
TK = 256
LOOKAHEAD = 2
VMEM_LIMIT = 48 * 1024 * 1024


def _params(n_axes):
    return pltpu.CompilerParams(dimension_semantics=("arbitrary",) * n_axes,
                                vmem_limit_bytes=VMEM_LIMIT)


def _dot(a, b):
    return jnp.dot(a, b, preferred_element_type=F32)


def _dot_nt(a, b):
    return lax.dot_general(a, b, (((1,), (1,)), ((), ())), preferred_element_type=F32)


def _split3(v):
    hi = v.astype(BF16)
    r = v - hi.astype(F32)
    mid = r.astype(BF16)
    lo = (r - mid.astype(F32)).astype(BF16)
    return hi, mid, lo


def _inproj_kernel(x_ref, g_ref, wfq_ref, wfk_ref, wfv_ref, wdq_ref, wdk_ref, wdv_ref, wf_ref, bf_ref,
                   gfq_ref, gfk_ref, gdq_ref, gdk_ref, seg_ref, tri_ref, place_ref,
                   fq_ref, fk_ref, fv_ref, dq_ref, dk_ref, dv_ref, carry_ref):
    @pl.when(pl.program_id(1) == 0)
    def _():
        carry_ref[...] = jnp.zeros_like(carry_ref)

    xf = x_ref[...]
    a = (xf * lax.rsqrt(jnp.mean(xf * xf, axis=-1, keepdims=True) + NORM_EPS) * g_ref[...]).astype(BF16)
    seg = seg_ref[...]

    def q_group(w_ref, gain_ref, out_ref):
        acc = _dot_nt(w_ref[...], a)
        ms = _dot(seg, (acc * acc).astype(BF16))
        out_ref[...] = (acc * lax.rsqrt(ms + NORM_EPS) * gain_ref[...]).astype(BF16)

    def k_group(w_ref, gain_ref):
        acc = _dot(a, w_ref[...])
        ms = _dot((acc * acc).astype(BF16), seg)
        return (acc * lax.rsqrt(ms + NORM_EPS) * gain_ref[...]).astype(BF16)

    def v_group(w_ref, out_ref):
        acc = _dot_nt(w_ref[...], a).astype(BF16)
        for hb in range(GROUP // LANES):
            for j in range(TM // TK):
                out_ref[hb, j] = acc[hb * LANES:(hb + 1) * LANES, j * TK:(j + 1) * TK]

    q_group(wfq_ref, gfq_ref, fq_ref)
    v_group(wfv_ref, fv_ref)
    q_group(wdq_ref, gdq_ref, dq_ref)
    dk_ref[...] = k_group(wdk_ref, gdk_ref)
    v_group(wdv_ref, dv_ref)

    fl = _dot(a, wf_ref[...]) + bf_ref[...]
    logf = jnp.minimum(fl, 0.0) - jnp.log1p(jnp.exp(-jnp.abs(fl)))
    tri = tri_ref[...]
    cs = carry_ref[...]
    for part in _split3(logf):
        cs = cs + _dot(tri, part)
    carry_ref[...] = cs[TM - 1:TM, :]

    fk = k_group(wfk_ref, gfk_ref)
    extra = jnp.zeros((TM, GROUP), F32)
    for t, part in enumerate(_split3(cs * LOG2E)):
        extra = extra + _dot(part, place_ref[t])
    extra = extra.astype(BF16)
    for hb in range(GROUP // LANES):
        fk_ref[:, 2 * hb * LANES:(2 * hb + 1) * LANES] = fk[:, hb * LANES:(hb + 1) * LANES]
        fk_ref[:, (2 * hb + 1) * LANES:(2 * hb + 2) * LANES] = extra[:, hb * LANES:(hb + 1) * LANES]


def _two_map_queries(qT_blk):
    row = lax.broadcasted_iota(jnp.int32, qT_blk.shape, 0)
    zero = jnp.zeros_like(qT_blk)
    return jnp.concatenate([jnp.where(row < HEAD_DIM, qT_blk, zero),
                            jnp.where(row >= HEAD_DIM, qT_blk, zero)], axis=1)


def _softmax_step(sT, vT_blk, carry):
    m, l, acc = carry
    m_new = jnp.maximum(m, jnp.max(sT, axis=0, keepdims=True))
    alpha = jnp.exp2(m - m_new)
    pT = jnp.exp2(sT - m_new)
    l = alpha * l + jnp.sum(pT, axis=0, keepdims=True)
    acc = alpha * acc + _dot(vT_blk, pT.astype(BF16))
    return m_new, l, acc


def _causal_keep():
    key = lax.broadcasted_iota(jnp.int32, (TK, 2 * TQ), 0)
    col = lax.broadcasted_iota(jnp.int32, (TK, 2 * TQ), 1)
    qry = jnp.where(col >= TQ, col - TQ, col)
    return key <= qry


def _init_carry():
    return (jnp.full((1, 2 * TQ), NEG_INF, F32), jnp.zeros((1, 2 * TQ), F32),
            jnp.zeros((LANES, 2 * TQ), F32))


def _causal_sweep(seq, k_ref, vT_ref, queries, adjust, finish):
    pairs = [(qi, kj) for qi in range(seq // TQ) for kj in range(qi + 1)]
    q_ops = {}

    def score(qi, kj):
        if qi not in q_ops:
            q_ops[qi] = queries(qi)
        return _dot(k_ref[kj * TK:(kj + 1) * TK, :], q_ops[qi])

    pending = [score(*pr) for pr in pairs[:LOOKAHEAD]]
    carry = None
    for n, (qi, kj) in enumerate(pairs):
        if n + LOOKAHEAD < len(pairs):
            pending.append(score(*pairs[n + LOOKAHEAD]))
        if kj == 0:
            carry = _init_carry()
        carry = _softmax_step(adjust(pending.pop(0), qi, kj), vT_ref[kj], carry)
        if kj == qi:
            m, l, acc = carry
            finish(qi, acc * (1.0 / l))


def _fox_kernel(qT_ref, k_ref, vT_ref, o_ref):
    seq = k_ref.shape[0]
    keep = _causal_keep()
    row = lax.broadcasted_iota(jnp.int32, (LANES, TQ), 0)
    arow = lax.broadcasted_iota(jnp.int32, (LANES, 2 * TQ), 0)
    acol = lax.broadcasted_iota(jnp.int32, (LANES, 2 * TQ), 1)
    first = (acol < TQ) & (arow < N_SPLIT)
    second = (acol >= TQ) & (arow >= N_SPLIT) & (arow < 2 * N_SPLIT)
    minus_c = jnp.where(first | second, -1.0, 0.0).astype(BF16)

    def queries(qi):
        return jnp.concatenate([_two_map_queries(qT_ref[:, qi * TQ:(qi + 1) * TQ]), minus_c], axis=0)

    def adjust(sT, qi, kj):
        return jnp.where(keep, sT, NEG_INF) if kj == qi else sT

    def finish(qi, oT):
        o = jnp.where(row < HEAD_DIM, oT[:, :TQ], oT[:, TQ:])
        o_ref[qi * TQ:(qi + 1) * TQ, :] = o.T.astype(BF16)

    _causal_sweep(seq, k_ref, vT_ref, queries, adjust, finish)


def _diff_kernel(lam_init, qT_ref, k_ref, vT_ref, bias_ref, lam_ref, gsub_ref, o_ref):
    seq = k_ref.shape[0]
    lp = lam_ref[...]
    lam = (jnp.exp(jnp.sum(lp[0:1] * lp[1:2], axis=1, keepdims=True))
           - jnp.exp(jnp.sum(lp[2:3] * lp[3:4], axis=1, keepdims=True)) + lam_init)

    def queries(qi):
        return _two_map_queries(qT_ref[:, qi * TQ:(qi + 1) * TQ])

    def adjust(sT, qi, kj):
        if kj >= qi - 1:
            b = bias_ref[qi - kj]
            sT = sT + jnp.concatenate([b, b], axis=1)
        return sT

    def finish(qi, oT):
        o = oT[:, :TQ] - lam * oT[:, TQ:]
        y = o * lax.rsqrt(jnp.mean(o * o, axis=0, keepdims=True) + NORM_EPS) * gsub_ref[...]
        o_ref[qi * TQ:(qi + 1) * TQ, :] = y.T.astype(BF16)

    _causal_sweep(seq, k_ref, vT_ref, queries, adjust, finish)


def _mix_kernel(x_ref, fox_ref, dif_ref, wa_ref, wb_ref, g_ref, wr_hi_ref, wr_lo_ref, br_ref, tri_ref,
                h_ref, xm_ref, topi_ref, gate_ref, rank_ref, cnt_ref, carry_ref):
    @pl.when(pl.program_id(0) == 0)
    def _():
        carry_ref[...] = jnp.zeros_like(carry_ref)

    h = x_ref[...] + _dot(fox_ref[...], wa_ref[...]) + _dot(dif_ref[...], wb_ref[...])
    h_ref[...] = h
    xm = h * lax.rsqrt(jnp.mean(h * h, axis=-1, keepdims=True) + NORM_EPS) * g_ref[...]
    xm_ref[...] = xm.astype(BF16)

    x_hi = xm.astype(BF16)
    x_lo = (xm - x_hi.astype(F32)).astype(BF16)
    logits = (_dot(x_hi, wr_hi_ref[...]) + _dot(x_hi, wr_lo_ref[...]) + _dot(x_lo, wr_hi_ref[...])
              + br_ref[...])

    lane = lax.broadcasted_iota(jnp.int32, logits.shape, 1)
    vals, sels, idxs = [], [], []
    for _ in range(TOP_K):
        mx = jnp.max(logits, axis=1, keepdims=True)
        idx = jnp.min(jnp.where(logits == mx, lane, LANES), axis=1, keepdims=True)
        sel = lane == idx
        logits = jnp.where(sel, -jnp.inf, logits)
        vals.append(mx)
        sels.append(sel)
        idxs.append(idx)
    exps = [jnp.exp(v - vals[0]) for v in vals]
    denom = exps[0] + exps[1] + exps[2] + exps[3]

    multi_hot = sum(s.astype(F32) for s in sels)
    before = _dot(tri_ref[...], multi_hot.astype(BF16)) + carry_ref[...]
    topi = jnp.zeros(logits.shape, jnp.int32)
    gate = jnp.zeros(logits.shape, F32)
    rank = jnp.zeros(logits.shape, F32)
    for k in range(TOP_K):
        rk = jnp.sum(jnp.where(sels[k], before, 0.0), axis=1, keepdims=True)
        topi = jnp.where(lane == k, idxs[k], topi)
        gate = jnp.where(lane == k, exps[k] / denom, gate)
        rank = jnp.where(lane == k, rk, rank)
    topi_ref[...] = topi
    gate_ref[...] = gate
    rank_ref[...] = rank.astype(jnp.int32)
    carry_ref[...] = carry_ref[...] + jnp.sum(multi_hot, axis=0, keepdims=True)
    cnt_ref[...] = carry_ref[...]


def _expert_kernel(be_ref, nu_ref, xs_ref, w1_ref, b1_ref, w2_ref, b2_ref, o_ref):
    i = pl.program_id(0)

    @pl.when(i < nu_ref[0])
    def _():
        h = _dot(xs_ref[...], w1_ref[...]) + b1_ref[...]
        acts = []
        for c in range(h.shape[1] // (2 * LANES)):
            glu = jnp.minimum(h[:, 2 * c * LANES:(2 * c + 1) * LANES], SWIGLU_LIMIT)
            lin = jnp.clip(h[:, (2 * c + 1) * LANES:(2 * c + 2) * LANES], -SWIGLU_LIMIT, SWIGLU_LIMIT)
            acts.append((glu * jax.nn.sigmoid(SWIGLU_ALPHA * glu) * (lin + 1.0)).astype(BF16))
        act = jnp.concatenate(acts, axis=1)
        o_ref[...] = (_dot(act, w2_ref[...]) + b2_ref[...]).astype(o_ref.dtype)

    @pl.when(i >= nu_ref[0])
    def _():
        o_ref[...] = jnp.zeros_like(o_ref)


def _final_kernel(h_ref, y_ref, gate_ref, g_ref, wg_ref, p_ref, wp_ref, o_ref):
    h = h_ref[...]
    gates = gate_ref[...]
    for k in range(TOP_K):
        h = h + gates[:, k:k + 1] * y_ref[k].astype(F32)
    n = (h * lax.rsqrt(jnp.mean(h * h, axis=-1, keepdims=True) + NORM_EPS) * g_ref[...]).astype(BF16)
    gate = jax.nn.sigmoid(_dot(n, wg_ref[...]))
    o_ref[...] = h + gate * _dot(p_ref[...].astype(BF16), wp_ref[...])


def _t5_bucket(rel):
    n = jnp.maximum(rel, 0)
    max_exact = NUM_BUCKETS // 2
    nf = jnp.maximum(n, 1).astype(F32)
    large = max_exact + (jnp.log(nf / max_exact) / math.log(MAX_DISTANCE / max_exact)
                         * (NUM_BUCKETS - max_exact)).astype(jnp.int32)
    large = jnp.minimum(large, NUM_BUCKETS - 1)
    return jnp.where(n < max_exact, n, large)


def _full(shape):
    return pl.BlockSpec(shape, lambda *_: (0,) * len(shape))


def _layer(i, h_in, p_i, rel_bias, prm):
    bsz, seq, d = h_in.shape
    n_tok = bsz * seq
    ns = seq // TM
    nk = seq // TK
    x2 = h_in.reshape(n_tok, d)

    w_in = prm['w_in']
    fox_w = N_FOX_HEADS * HEAD_DIM
    offs = [0, fox_w, 2 * fox_w, 3 * fox_w, 3 * fox_w + N_FOX_HEADS]
    offs += [offs[4] + GROUP, offs[4] + 2 * GROUP, offs[4] + 3 * GROUP]
    col = lambda a, b: w_in[:, a:b].astype(BF16)
    wfqT, wfk, wfvT = col(offs[0], offs[1]).T, col(offs[1], offs[2]), col(offs[2], offs[3]).T
    wdqT, wdk, wdvT = col(offs[4], offs[5]).T, col(offs[5], offs[6]), col(offs[6], offs[7]).T
    wf = jnp.zeros((d, LANES), BF16).at[:, :N_FOX_HEADS].set(col(offs[3], offs[4]))
    bf = jnp.zeros((1, LANES), F32).at[0, :N_FOX_HEADS].set(prm['b_f'])
    scale = HEAD_DIM ** -0.5 * LOG2E
    rep = GROUP // HEAD_DIM
    heads = jnp.arange(N_FOX_HEADS)
    place = jnp.stack([jnp.zeros((LANES, GROUP), BF16)
                       .at[heads, (heads // 2) * LANES + (heads % 2) * N_SPLIT + t].set(1.0)
                       for t in range(N_SPLIT)])
    gfqT = jnp.broadcast_to((jnp.tile(prm['g_fox_q'], rep) * scale)[:, None], (GROUP, TM))
    gdqT = jnp.broadcast_to((jnp.tile(prm['g_diff_q'], rep) * scale)[:, None], (GROUP, TM))
    gfk = jnp.tile(prm['g_fox_k'], rep)[None, :]
    gdk = jnp.tile(prm['g_diff_k'], rep)[None, :]
    gi = jnp.arange(GROUP) // HEAD_DIM
    seg = jnp.where(gi[:, None] == gi[None, :], 1.0 / HEAD_DIM, 0.0).astype(BF16)
    ti = jnp.arange(TM)
    tri_incl = (ti[None, :] <= ti[:, None]).astype(BF16)
    tri_strict = (ti[None, :] < ti[:, None]).astype(BF16)

    w_spec_t = _full((GROUP, d))
    w_spec = _full((d, GROUP))
    qT_shape = jax.ShapeDtypeStruct((bsz, GROUP, seq), BF16)
    k_shape = jax.ShapeDtypeStruct((n_tok, GROUP), BF16)
    vT_shape = jax.ShapeDtypeStruct((bsz, GROUP // LANES, nk, LANES, TK), BF16)
    qT_spec = pl.BlockSpec((None, GROUP, TM), lambda b, s: (b, 0, s))
    k_spec = pl.BlockSpec((TM, GROUP), lambda b, s: (b * ns + s, 0))
    vT_spec = pl.BlockSpec((None, GROUP // LANES, TM // TK, LANES, TK), lambda b, s: (b, 0, s, 0, 0))
    fk_shape = jax.ShapeDtypeStruct((n_tok, 2 * GROUP), BF16)
    fk_spec = pl.BlockSpec((TM, 2 * GROUP), lambda b, s: (b * ns + s, 0))
    fqT, fk, fvT, dqT, dk, dvT = pl.pallas_call(
        _inproj_kernel,
        grid=(bsz, ns),
        in_specs=[pl.BlockSpec((TM, d), lambda b, s: (b * ns + s, 0)), _full((1, d)),
                  w_spec_t, w_spec, w_spec_t, w_spec_t, w_spec, w_spec_t, _full((d, LANES)), _full((1, LANES)),
                  _full((GROUP, TM)), _full((1, GROUP)), _full((GROUP, TM)), _full((1, GROUP)),
                  _full((GROUP, GROUP)), _full((TM, TM)), _full((N_SPLIT, LANES, GROUP))],
        out_specs=[qT_spec, fk_spec, vT_spec, qT_spec, k_spec, vT_spec],
        out_shape=[qT_shape, fk_shape, vT_shape, qT_shape, k_shape, vT_shape],
        scratch_shapes=[pltpu.VMEM((1, LANES), F32)],
        compiler_params=_params(2),
        name="inproj",
    )(x2, prm['g_attn'][None, :], wfqT, wfk, wfvT, wdqT, wdk, wdvT, wf, bf, gfqT, gfk, gdqT, gdk, seg, tri_incl,
      place)

    n_hb = GROUP // LANES
    att_q = pl.BlockSpec((None, LANES, seq), lambda b, h: (b, h, 0))
    att_k = pl.BlockSpec((seq, LANES), lambda b, h: (b, h))
    att_v = pl.BlockSpec((None, None, nk, LANES, TK), lambda b, h: (b, h, 0, 0, 0))
    att_o = pl.BlockSpec((seq, LANES), lambda b, h: (b, h))
    fox_out = pl.pallas_call(
        _fox_kernel,
        grid=(bsz, n_hb),
        in_specs=[att_q, pl.BlockSpec((seq, 2 * LANES), lambda b, h: (b, h)), att_v],
        out_specs=att_o,
        out_shape=jax.ShapeDtypeStruct((n_tok, GROUP), BF16),
        compiler_params=_params(2),
        name="fox_attention",
    )(fqT, fk, fvT)

    kpos = jnp.arange(TK)[:, None]
    qpos = jnp.arange(TQ)[None, :]
    rel = jnp.stack([qpos - kpos, qpos - kpos + TK])
    assert TK + 1 >= MAX_DISTANCE and TQ == TK
    table = (rel_bias.astype(F32) - rel_bias[NUM_BUCKETS - 1].astype(F32)[None, :]) * LOG2E
    biasT = jnp.transpose(jnp.take(table, _t5_bucket(rel), axis=0), (3, 0, 1, 2))
    biasT = jnp.where((rel >= 0)[None], biasT, NEG_INF)
    lam_init = 0.8 - 0.6 * math.exp(-0.3 * i)
    lam_p = jnp.stack([prm['lambda_q1'], prm['lambda_k1'], prm['lambda_q2'], prm['lambda_k2']]).astype(F32)
    gsubT = jnp.broadcast_to((prm['g_subln'] * (1.0 - lam_init))[:, None], (LANES, TQ))
    diff_out = pl.pallas_call(
        functools.partial(_diff_kernel, lam_init),
        grid=(bsz, N_DIFF_HEADS),
        in_specs=[att_q, att_k, att_v, pl.BlockSpec((None, 2, TK, TQ), lambda b, h: (h, 0, 0, 0)),
                  _full((4, HEAD_DIM)), _full((LANES, TQ))],
        out_specs=att_o,
        out_shape=jax.ShapeDtypeStruct((n_tok, GROUP), BF16),
        compiler_params=_params(2),
        name="diff_attention",
    )(dqT, dk, dvT, biasT, lam_p, gsubT)

    w_out = prm['w_out'].astype(BF16)
    wr = jnp.zeros((d, LANES), F32).at[:, :N_EXPERTS].set(prm['w_router'])
    wr_hi = wr.astype(BF16)
    wr_lo = (wr - wr_hi.astype(F32)).astype(BF16)
    br = jnp.full((1, LANES), NEG_INF, F32).at[0, :N_EXPERTS].set(prm['b_router'])
    row_spec = lambda w: pl.BlockSpec((TM, w), lambda t: (t, 0))
    h1, xm, topi, gates, rank, counts = pl.pallas_call(
        _mix_kernel,
        grid=(n_tok // TM,),
        in_specs=[row_spec(d), row_spec(GROUP), row_spec(GROUP), _full((GROUP, d)), _full((GROUP, d)),
                  _full((1, d)), _full((d, LANES)), _full((d, LANES)), _full((1, LANES)), _full((TM, TM))],
        out_specs=[row_spec(d), row_spec(d), row_spec(LANES), row_spec(LANES), row_spec(LANES),
                   _full((1, LANES))],
        out_shape=[jax.ShapeDtypeStruct((n_tok, d), F32), jax.ShapeDtypeStruct((n_tok, d), BF16),
                   jax.ShapeDtypeStruct((n_tok, LANES), jnp.int32), jax.ShapeDtypeStruct((n_tok, LANES), F32),
                   jax.ShapeDtypeStruct((n_tok, LANES), jnp.int32), jax.ShapeDtypeStruct((1, LANES), F32)],
        scratch_shapes=[pltpu.VMEM((1, LANES), F32)],
        compiler_params=_params(1),
        name="mix_router",
    )(x2, fox_out, diff_out, w_out[:GROUP], w_out[GROUP:], prm['g_mlp'][None, :], wr_hi, wr_lo, br, tri_strict)

    m = n_tok * TOP_K
    n_blocks = (m + N_EXPERTS * MOE_BLOCK + MOE_BLOCK - 1) // MOE_BLOCK
    p_rows = n_blocks * MOE_BLOCK
    cnt = counts[0, :N_EXPERTS].astype(jnp.int32)
    padded = ((cnt + MOE_BLOCK - 1) // MOE_BLOCK) * MOE_BLOCK
    pad_ends = jnp.cumsum(padded)
    pad_starts = pad_ends - padded
    top_i = topi[:, :TOP_K]
    dest = pad_starts[top_i] + rank[:, :TOP_K]
    tok = jnp.broadcast_to(jnp.arange(n_tok, dtype=jnp.int32)[:, None], (n_tok, TOP_K))
    buf_tok = jnp.zeros((p_rows,), jnp.int32).at[dest.reshape(m)].set(tok.reshape(m))
    block_start = jnp.arange(n_blocks, dtype=jnp.int32) * MOE_BLOCK
    block_expert = jnp.minimum(jnp.searchsorted(pad_ends, block_start, side='right'),
                               N_EXPERTS - 1).astype(jnp.int32)
    n_used = (pad_ends[-1] // MOE_BLOCK).astype(jnp.int32)[None]

    d_ff = prm['w2'].shape[1]
    n_grp = 2 * d_ff // (2 * LANES)
    w1p = prm['w1'].reshape(N_EXPERTS, d, n_grp, LANES, 2).transpose(0, 1, 2, 4, 3)
    w1p = w1p.reshape(N_EXPERTS, d, 2 * d_ff).astype(BF16)
    b1p = prm['b1'].reshape(N_EXPERTS, n_grp, LANES, 2).transpose(0, 1, 3, 2).reshape(N_EXPERTS, 1, 2 * d_ff)
    w2b = prm['w2'].astype(BF16)
    b2 = prm['b2'].reshape(N_EXPERTS, 1, d)
    xs = jnp.take(xm, buf_tok, axis=0)
    ys = pl.pallas_call(
        _expert_kernel,
        grid_spec=pltpu.PrefetchScalarGridSpec(
            num_scalar_prefetch=2,
            grid=(n_blocks,),
            in_specs=[pl.BlockSpec((MOE_BLOCK, d), lambda t, be, nu: (t, 0)),
                      pl.BlockSpec((None, d, 2 * d_ff), lambda t, be, nu: (be[t], 0, 0)),
                      pl.BlockSpec((None, 1, 2 * d_ff), lambda t, be, nu: (be[t], 0, 0)),
                      pl.BlockSpec((None, d_ff, d), lambda t, be, nu: (be[t], 0, 0)),
                      pl.BlockSpec((None, 1, d), lambda t, be, nu: (be[t], 0, 0))],
            out_specs=pl.BlockSpec((MOE_BLOCK, d), lambda t, be, nu: (t, 0)),
        ),
        out_shape=jax.ShapeDtypeStruct((p_rows, d), BF16),
        compiler_params=_params(1),
        name="experts",
    )(block_expert, n_used, xs, w1p, b1p, w2b, b2)

    yk = jnp.take(ys, dest.T, axis=0)
    ple = p_i.shape[-1]
    out = pl.pallas_call(
        _final_kernel,
        grid=(n_tok // TM,),
        in_specs=[row_spec(d), pl.BlockSpec((TOP_K, TM, d), lambda t: (0, t, 0)), row_spec(LANES),
                  _full((1, d)), _full((d, d)), row_spec(ple), _full((ple, d))],
        out_specs=row_spec(d),
        out_shape=jax.ShapeDtypeStruct((n_tok, d), F32),
        compiler_params=_params(1),
        name="combine_ple",
    )(h1, yk, gates, prm['g_ple'][None, :], prm['w_ple_gate'].astype(BF16), p_i.reshape(n_tok, ple),
      prm['w_ple_proj'].astype(BF16))
    return out.reshape(bsz, seq, d)


def kernel(x, p, rel_bias, g_attn, w_in, b_f, g_fox_q, g_fox_k, g_diff_q, g_diff_k, lambda_q1, lambda_k1,
           lambda_q2, lambda_k2, g_subln, w_out, g_mlp, w_router, b_router, w1, b1, w2, b2, g_ple,
           w_ple_gate, w_ple_proj):
    stacked = dict(g_attn=g_attn, w_in=w_in, b_f=b_f, g_fox_q=g_fox_q, g_fox_k=g_fox_k, g_diff_q=g_diff_q,
                   g_diff_k=g_diff_k, lambda_q1=lambda_q1, lambda_k1=lambda_k1, lambda_q2=lambda_q2,
                   lambda_k2=lambda_k2, g_subln=g_subln, w_out=w_out, g_mlp=g_mlp, w_router=w_router,
                   b_router=b_router, w1=w1, b1=b1, w2=w2, b2=b2, g_ple=g_ple, w_ple_gate=w_ple_gate,
                   w_ple_proj=w_ple_proj)
    h = x
    for i in range(p.shape[0]):
        h = _layer(i, h, p[i], rel_bias, {name: v[i] for name, v in stacked.items()})
    return h
```

```python
import functools
import math

import jax
import jax.numpy as jnp
from jax import lax
from jax.experimental import pallas as pl
from jax.experimental.pallas import tpu as pltpu

F32 = jnp.float32
BF16 = jnp.bfloat16

HEAD_DIM = 64
N_FOX_HEADS = 8
N_DIFF_HEADS = 4
GROUP = 512
LANES = 128
NUM_BUCKETS = 32
MAX_DISTANCE = 128
N_EXPERTS = 32
TOP_K = 4
SWIGLU_ALPHA = 1.702
SWIGLU_LIMIT = 7.0
MOE_BLOCK = 256
NORM_EPS = 1e-6
NEG_INF = -1e30
LOG2E = math.log2(math.e)
N_SPLIT = 3

TM = 512
TQ = 256
TK = 256
LOOKAHEAD = 2
VMEM_LIMIT = 48 * 1024 * 1024


def _params(n_axes):
    return pltpu.CompilerParams(dimension_semantics=("arbitrary",) * n_axes,
                                vmem_limit_bytes=VMEM_LIMIT)


def _dot(a, b):
    return jnp.dot(a, b, preferred_element_type=F32)


def _dot_nt(a, b):
    return lax.dot_general(a, b, (((1,), (1,)), ((), ())), preferred_element_type=F32)


def _split3(v):
    hi = v.astype(BF16)
    r = v - hi.astype(F32)
    mid = r.astype(BF16)
    lo = (r - mid.astype(F32)).astype(BF16)
    return hi, mid, lo


def _inproj_kernel(x_ref, g_ref, wfq_ref, wfk_ref, wfv_ref, wdq_ref, wdk_ref, wdv_ref, wf_ref, bf_ref,
                   gfq_ref, gfk_ref, gdq_ref, gdk_ref, seg_ref, tri_ref, place_ref,
                   fq_ref, fk_ref, fv_ref, dq_ref, dk_ref, dv_ref, carry_ref):
    @pl.when(pl.program_id(1) == 0)
    def _():
        carry_ref[...] = jnp.zeros_like(carry_ref)

    xf = x_ref[...]
    a = (xf * lax.rsqrt(jnp.mean(xf * xf, axis=-1, keepdims=True) + NORM_EPS) * g_ref[...]).astype(BF16)
    seg = seg_ref[...]

    def q_group(w_ref, gain_ref, out_ref):
        acc = _dot_nt(w_ref[...], a)
        ms = _dot(seg, (acc * acc).astype(BF16))
        out_ref[...] = (acc * lax.rsqrt(ms + NORM_EPS) * gain_ref[...]).astype(BF16)

    def k_group(w_ref, gain_ref):
        acc = _dot(a, w_ref[...])
        ms = _dot((acc * acc).astype(BF16), seg)
        return (acc * lax.rsqrt(ms + NORM_EPS) * gain_ref[...]).astype(BF16)

    def v_group(w_ref, out_ref):
        acc = _dot_nt(w_ref[...], a).astype(BF16)
        for hb in range(GROUP // LANES):
            for j in range(TM // TK):
                out_ref[hb, j] = acc[hb * LANES:(hb + 1) * LANES, j * TK:(j + 1) * TK]

    q_group(wfq_ref, gfq_ref, fq_ref)
    v_group(wfv_ref, fv_ref)
    q_group(wdq_ref, gdq_ref, dq_ref)
    dk_ref[...] = k_group(wdk_ref, gdk_ref)
    v_group(wdv_ref, dv_ref)

    fl = _dot(a, wf_ref[...]) + bf_ref[...]
    logf = jnp.minimum(fl, 0.0) - jnp.log1p(jnp.exp(-jnp.abs(fl)))
    tri = tri_ref[...]
    cs = carry_ref[...]
    for part in _split3(logf):
        cs = cs + _dot(tri, part)
    carry_ref[...] = cs[TM - 1:TM, :]

    fk = k_group(wfk_ref, gfk_ref)
    extra = jnp.zeros((TM, GROUP), F32)
    for t, part in enumerate(_split3(cs * LOG2E)):
        extra = extra + _dot(part, place_ref[t])
    extra = extra.astype(BF16)
    for hb in range(GROUP // LANES):
        fk_ref[:, 2 * hb * LANES:(2 * hb + 1) * LANES] = fk[:, hb * LANES:(hb + 1) * LANES]
        fk_ref[:, (2 * hb + 1) * LANES:(2 * hb + 2) * LANES] = extra[:, hb * LANES:(hb + 1) * LANES]


def _two_map_queries(qT_blk):
    row = lax.broadcasted_iota(jnp.int32, qT_blk.shape, 0)
    zero = jnp.zeros_like(qT_blk)
    return jnp.concatenate([jnp.where(row < HEAD_DIM, qT_blk, zero),
                            jnp.where(row >= HEAD_DIM, qT_blk, zero)], axis=1)


def _softmax_step(sT, vT_blk, carry):
    m, l, acc = carry
    m_new = jnp.maximum(m, jnp.max(sT, axis=0, keepdims=True))
    alpha = jnp.exp2(m - m_new)
    pT = jnp.exp2(sT - m_new)
    l = alpha * l + jnp.sum(pT, axis=0, keepdims=True)
    acc = alpha * acc + _dot(vT_blk, pT.astype(BF16))
    return m_new, l, acc


def _causal_keep():
    key = lax.broadcasted_iota(jnp.int32, (TK, 2 * TQ), 0)
    col = lax.broadcasted_iota(jnp.int32, (TK, 2 * TQ), 1)
    qry = jnp.where(col >= TQ, col - TQ, col)
    return key <= qry


def _init_carry():
    return (jnp.full((1, 2 * TQ), NEG_INF, F32), jnp.zeros((1, 2 * TQ), F32),
            jnp.zeros((LANES, 2 * TQ), F32))


def _causal_sweep(seq, k_ref, vT_ref, queries, adjust, finish):
    pairs = [(qi, kj) for qi in range(seq // TQ) for kj in range(qi + 1)]
    q_ops = {}

    def score(qi, kj):
        if qi not in q_ops:
            q_ops[qi] = queries(qi)
        return _dot(k_ref[kj * TK:(kj + 1) * TK, :], q_ops[qi])

    pending = [score(*pr) for pr in pairs[:LOOKAHEAD]]
    carry = None
    for n, (qi, kj) in enumerate(pairs):
        if n + LOOKAHEAD < len(pairs):
            pending.append(score(*pairs[n + LOOKAHEAD]))
        if kj == 0:
            carry = _init_carry()
        carry = _softmax_step(adjust(pending.pop(0), qi, kj), vT_ref[kj], carry)
        if kj == qi:
            m, l, acc = carry
            finish(qi, acc * (1.0 / l))


def _fox_kernel(qT_ref, k_ref, vT_ref, o_ref):
    seq = k_ref.shape[0]
    keep = _causal_keep()
    row = lax.broadcasted_iota(jnp.int32, (LANES, TQ), 0)
    arow = lax.broadcasted_iota(jnp.int32, (LANES, 2 * TQ), 0)
    acol = lax.broadcasted_iota(jnp.int32, (LANES, 2 * TQ), 1)
    first = (acol < TQ) & (arow < N_SPLIT)
    second = (acol >= TQ) & (arow >= N_SPLIT) & (arow < 2 * N_SPLIT)
    minus_c = jnp.where(first | second, -1.0, 0.0).astype(BF16)

    def queries(qi):
        return jnp.concatenate([_two_map_queries(qT_ref[:, qi * TQ:(qi + 1) * TQ]), minus_c], axis=0)

    def adjust(sT, qi, kj):
        return jnp.where(keep, sT, NEG_INF) if kj == qi else sT

    def finish(qi, oT):
        o = jnp.where(row < HEAD_DIM, oT[:, :TQ], oT[:, TQ:])
        o_ref[qi * TQ:(qi + 1) * TQ, :] = o.T.astype(BF16)

    _causal_sweep(seq, k_ref, vT_ref, queries, adjust, finish)


def _diff_kernel(lam_init, qT_ref, k_ref, vT_ref, bias_ref, lam_ref, gsub_ref, o_ref):
    seq = k_ref.shape[0]
    lp = lam_ref[...]
    lam = (jnp.exp(jnp.sum(lp[0:1] * lp[1:2], axis=1, keepdims=True))
           - jnp.exp(jnp.sum(lp[2:3] * lp[3:4], axis=1, keepdims=True)) + lam_init)

    def queries(qi):
        return _two_map_queries(qT_ref[:, qi * TQ:(qi + 1) * TQ])

    def adjust(sT, qi, kj):
        if kj >= qi - 1:
            b = bias_ref[qi - kj]
            sT = sT + jnp.concatenate([b, b], axis=1)
        return sT

    def finish(qi, oT):
        o = oT[:, :TQ] - lam * oT[:, TQ:]
        y = o * lax.rsqrt(jnp.mean(o * o, axis=0, keepdims=True) + NORM_EPS) * gsub_ref[...]
        o_ref[qi * TQ:(qi + 1) * TQ, :] = y.T.astype(BF16)

    _causal_sweep(seq, k_ref, vT_ref, queries, adjust, finish)


def _mix_kernel(x_ref, fox_ref, dif_ref, wa_ref, wb_ref, g_ref, wr_hi_ref, wr_lo_ref, br_ref, tri_ref,
                h_ref, xm_ref, topi_ref, gate_ref, rank_ref, cnt_ref, carry_ref):
    @pl.when(pl.program_id(0) == 0)
    def _():
        carry_ref[...] = jnp.zeros_like(carry_ref)

    h = x_ref[...] + _dot(fox_ref[...], wa_ref[...]) + _dot(dif_ref[...], wb_ref[...])
    h_ref[...] = h
    xm = h * lax.rsqrt(jnp.mean(h * h, axis=-1, keepdims=True) + NORM_EPS) * g_ref[...]
    xm_ref[:, 0, :] = xm

    x_hi = xm.astype(BF16)
    x_lo = (xm - x_hi.astype(F32)).astype(BF16)
    logits = (_dot(x_hi, wr_hi_ref[...]) + _dot(x_hi, wr_lo_ref[...]) + _dot(x_lo, wr_hi_ref[...])
              + br_ref[...])

    lane = lax.broadcasted_iota(jnp.int32, logits.shape, 1)
    vals, sels, idxs = [], [], []
    for _ in range(TOP_K):
        mx = jnp.max(logits, axis=1, keepdims=True)
        idx = jnp.min(jnp.where(logits == mx, lane, LANES), axis=1, keepdims=True)
        sel = lane == idx
        logits = jnp.where(sel, -jnp.inf, logits)
        vals.append(mx)
        sels.append(sel)
        idxs.append(idx)
    exps = [jnp.exp(v - vals[0]) for v in vals]
    denom = exps[0] + exps[1] + exps[2] + exps[3]

    multi_hot = sum(s.astype(F32) for s in sels)
    before = _dot(tri_ref[...], multi_hot.astype(BF16)) + carry_ref[...]
    topi = jnp.zeros(logits.shape, jnp.int32)
    gate = jnp.zeros(logits.shape, F32)
    rank = jnp.zeros(logits.shape, F32)
    for k in range(TOP_K):
        rk = jnp.sum(jnp.where(sels[k], before, 0.0), axis=1, keepdims=True)
        topi = jnp.where(lane == k, idxs[k], topi)
        gate = jnp.where(lane == k, exps[k] / denom, gate)
        rank = jnp.where(lane == k, rk, rank)
    topi_ref[...] = topi
    gate_ref[...] = gate
    rank_ref[...] = rank.astype(jnp.int32)
    carry_ref[...] = carry_ref[...] + jnp.sum(multi_hot, axis=0, keepdims=True)
    cnt_ref[...] = carry_ref[...]


def _expert_kernel(be_ref, nu_ref, tok_hbm, asg_hbm, x_hbm, w1_ref, b1_ref, w2_ref, b2_ref, y_hbm,
                   tok_ref, asg_ref, x0, x1, y0, y1, idx_sem, g_sem, s_sem):
    i = pl.program_id(0)
    n_used = nu_ref[0]
    n_blocks = asg_hbm.shape[0]
    spare_row = y_hbm.shape[0] - MOE_BLOCK
    xbufs, ybufs = (x0, x1), (y0, y1)

    def tok_copy(blk, slot):
        return pltpu.make_async_copy(tok_hbm.at[blk], tok_ref.at[slot], idx_sem.at[0, slot])

    def asg_copy(blk, slot):
        return pltpu.make_async_copy(asg_hbm.at[blk], asg_ref.at[slot], idx_sem.at[1, slot])

    def row_loop(fn):
        for r in range(MOE_BLOCK):
            fn(r)

    def gather_issue(slot):
        row_loop(lambda r: pltpu.make_async_copy(x_hbm.at[tok_ref[slot, r]], xbufs[slot].at[pl.ds(r, 1)],
                                                 g_sem.at[slot]).start())

    def gather_wait(slot):
        pltpu.make_async_copy(xbufs[slot], xbufs[slot], g_sem.at[slot]).wait()

    def scatter_issue(slot):
        row_loop(lambda r: pltpu.make_async_copy(ybufs[slot].at[pl.ds(r, 1)], y_hbm.at[asg_ref[slot, r]],
                                                 s_sem.at[slot]).start())

    def scatter_wait(slot):
        pltpu.make_async_copy(ybufs[slot], ybufs[slot], s_sem.at[slot]).wait()

    @pl.when(i == 0)
    def _():
        y1[...] = jnp.zeros(y1.shape, y1.dtype)
        for r in range(MOE_BLOCK):
            asg_ref[1, r] = spare_row + r
        tok_copy(0, 0).start()
        tok_copy(0, 0).wait()
        tok_copy(jnp.minimum(1, n_blocks - 1), 1).start()
        gather_issue(0)

    def step(cur):
        nxt = 1 - cur
        asg_copy(i, cur).start()
        tok_copy(0, nxt).wait()
        gather_issue(nxt)
        tok_copy(jnp.minimum(i + 2, n_blocks - 1), cur).start()

        @pl.when(i >= 1)
        def _():
            asg_copy(0, nxt).wait()
            scatter_wait(cur)

        scatter_issue(nxt)
        gather_wait(cur)
        xbuf, ybuf = xbufs[cur], ybufs[cur]
        xs = xbuf[...].astype(BF16)
        h = _dot(xs, w1_ref[...]) + b1_ref[...]
        acts = []
        for c in range(h.shape[1] // (2 * LANES)):
            glu = jnp.minimum(h[:, 2 * c * LANES:(2 * c + 1) * LANES], SWIGLU_LIMIT)
            lin = jnp.clip(h[:, (2 * c + 1) * LANES:(2 * c + 2) * LANES], -SWIGLU_LIMIT, SWIGLU_LIMIT)
            acts.append((glu * jax.nn.sigmoid(SWIGLU_ALPHA * glu) * (lin + 1.0)).astype(BF16))
        out = _dot(jnp.concatenate(acts, axis=1), w2_ref[...]) + b2_ref[...]
        ybuf[...] = out

        @pl.when(i == n_used - 1)
        def _():
            asg_copy(0, cur).wait()
            scatter_issue(cur)
            scatter_wait(nxt)
            scatter_wait(cur)
            gather_wait(nxt)
            tok_copy(0, cur).wait()

    for parity in range(2):
        pl.when((i < n_used) & (i % 2 == parity))(functools.partial(step, parity))


def _final_kernel(h_ref, *rest):
    y_refs, (gate_ref, g_ref, wg_ref, p_ref, wp_ref, o_ref) = rest[:TOP_K], rest[TOP_K:]
    gates = gate_ref[...]
    moe = sum(gates[:, k:k + 1] * y_ref[:, 0, :] for k, y_ref in enumerate(y_refs))
    h = h_ref[...] + moe
    n = (h * lax.rsqrt(jnp.mean(h * h, axis=-1, keepdims=True) + NORM_EPS) * g_ref[...]).astype(BF16)
    gate = jax.nn.sigmoid(_dot(n, wg_ref[...]))
    o_ref[...] = h + gate * _dot(p_ref[...].astype(BF16), wp_ref[...])


def _t5_bucket(rel):
    n = jnp.maximum(rel, 0)
    max_exact = NUM_BUCKETS // 2
    nf = jnp.maximum(n, 1).astype(F32)
    large = max_exact + (jnp.log(nf / max_exact) / math.log(MAX_DISTANCE / max_exact)
                         * (NUM_BUCKETS - max_exact)).astype(jnp.int32)
    large = jnp.minimum(large, NUM_BUCKETS - 1)
    return jnp.where(n < max_exact, n, large)


def _full(shape):
    return pl.BlockSpec(shape, lambda *_: (0,) * len(shape))


def _layer(i, h_in, p_i, rel_bias, prm):
    bsz, seq, d = h_in.shape
    n_tok = bsz * seq
    ns = seq // TM
    nk = seq // TK
    x2 = h_in.reshape(n_tok, d)

    w_in = prm['w_in']
    fox_w = N_FOX_HEADS * HEAD_DIM
    offs = [0, fox_w, 2 * fox_w, 3 * fox_w, 3 * fox_w + N_FOX_HEADS]
    offs += [offs[4] + GROUP, offs[4] + 2 * GROUP, offs[4] + 3 * GROUP]
    col = lambda a, b: w_in[:, a:b].astype(BF16)
    wfqT, wfk, wfvT = col(offs[0], offs[1]).T, col(offs[1], offs[2]), col(offs[2], offs[3]).T
    wdqT, wdk, wdvT = col(offs[4], offs[5]).T, col(offs[5], offs[6]), col(offs[6], offs[7]).T
    wf = jnp.zeros((d, LANES), BF16).at[:, :N_FOX_HEADS].set(col(offs[3], offs[4]))
    bf = jnp.zeros((1, LANES), F32).at[0, :N_FOX_HEADS].set(prm['b_f'])
    scale = HEAD_DIM ** -0.5 * LOG2E
    rep = GROUP // HEAD_DIM
    heads = jnp.arange(N_FOX_HEADS)
    place = jnp.stack([jnp.zeros((LANES, GROUP), BF16)
                       .at[heads, (heads // 2) * LANES + (heads % 2) * N_SPLIT + t].set(1.0)
                       for t in range(N_SPLIT)])
    gfqT = jnp.broadcast_to((jnp.tile(prm['g_fox_q'], rep) * scale)[:, None], (GROUP, TM))
    gdqT = jnp.broadcast_to((jnp.tile(prm['g_diff_q'], rep) * scale)[:, None], (GROUP, TM))
    gfk = jnp.tile(prm['g_fox_k'], rep)[None, :]
    gdk = jnp.tile(prm['g_diff_k'], rep)[None, :]
    gi = jnp.arange(GROUP) // HEAD_DIM
    seg = jnp.where(gi[:, None] == gi[None, :], 1.0 / HEAD_DIM, 0.0).astype(BF16)
    ti = jnp.arange(TM)
    tri_incl = (ti[None, :] <= ti[:, None]).astype(BF16)
    tri_strict = (ti[None, :] < ti[:, None]).astype(BF16)

    w_spec_t = _full((GROUP, d))
    w_spec = _full((d, GROUP))
    qT_shape = jax.ShapeDtypeStruct((bsz, GROUP, seq), BF16)
    k_shape = jax.ShapeDtypeStruct((n_tok, GROUP), BF16)
    vT_shape = jax.ShapeDtypeStruct((bsz, GROUP // LANES, nk, LANES, TK), BF16)
    qT_spec = pl.BlockSpec((None, GROUP, TM), lambda b, s: (b, 0, s))
    k_spec = pl.BlockSpec((TM, GROUP), lambda b, s: (b * ns + s, 0))
    vT_spec = pl.BlockSpec((None, GROUP // LANES, TM // TK, LANES, TK), lambda b, s: (b, 0, s, 0, 0))
    fk_shape = jax.ShapeDtypeStruct((n_tok, 2 * GROUP), BF16)
    fk_spec = pl.BlockSpec((TM, 2 * GROUP), lambda b, s: (b * ns + s, 0))
    fqT, fk, fvT, dqT, dk, dvT = pl.pallas_call(
        _inproj_kernel,
        grid=(bsz, ns),
        in_specs=[pl.BlockSpec((TM, d), lambda b, s: (b * ns + s, 0)), _full((1, d)),
                  w_spec_t, w_spec, w_spec_t, w_spec_t, w_spec, w_spec_t, _full((d, LANES)), _full((1, LANES)),
                  _full((GROUP, TM)), _full((1, GROUP)), _full((GROUP, TM)), _full((1, GROUP)),
                  _full((GROUP, GROUP)), _full((TM, TM)), _full((N_SPLIT, LANES, GROUP))],
        out_specs=[qT_spec, fk_spec, vT_spec, qT_spec, k_spec, vT_spec],
        out_shape=[qT_shape, fk_shape, vT_shape, qT_shape, k_shape, vT_shape],
        scratch_shapes=[pltpu.VMEM((1, LANES), F32)],
        compiler_params=_params(2),
        name="inproj",
    )(x2, prm['g_attn'][None, :], wfqT, wfk, wfvT, wdqT, wdk, wdvT, wf, bf, gfqT, gfk, gdqT, gdk, seg, tri_incl,
      place)

    n_hb = GROUP // LANES
    att_q = pl.BlockSpec((None, LANES, seq), lambda b, h: (b, h, 0))
    att_k = pl.BlockSpec((seq, LANES), lambda b, h: (b, h))
    att_v = pl.BlockSpec((None, None, nk, LANES, TK), lambda b, h: (b, h, 0, 0, 0))
    att_o = pl.BlockSpec((seq, LANES), lambda b, h: (b, h))
    fox_out = pl.pallas_call(
        _fox_kernel,
        grid=(bsz, n_hb),
        in_specs=[att_q, pl.BlockSpec((seq, 2 * LANES), lambda b, h: (b, h)), att_v],
        out_specs=att_o,
        out_shape=jax.ShapeDtypeStruct((n_tok, GROUP), BF16),
        compiler_params=_params(2),
        name="fox_attention",
    )(fqT, fk, fvT)

    kpos = jnp.arange(TK)[:, None]
    qpos = jnp.arange(TQ)[None, :]
    rel = jnp.stack([qpos - kpos, qpos - kpos + TK])
    assert TK + 1 >= MAX_DISTANCE and TQ == TK
    table = (rel_bias.astype(F32) - rel_bias[NUM_BUCKETS - 1].astype(F32)[None, :]) * LOG2E
    biasT = jnp.transpose(jnp.take(table, _t5_bucket(rel), axis=0), (3, 0, 1, 2))
    biasT = jnp.where((rel >= 0)[None], biasT, NEG_INF)
    lam_init = 0.8 - 0.6 * math.exp(-0.3 * i)
    lam_p = jnp.stack([prm['lambda_q1'], prm['lambda_k1'], prm['lambda_q2'], prm['lambda_k2']]).astype(F32)
    gsubT = jnp.broadcast_to((prm['g_subln'] * (1.0 - lam_init))[:, None], (LANES, TQ))
    diff_out = pl.pallas_call(
        functools.partial(_diff_kernel, lam_init),
        grid=(bsz, N_DIFF_HEADS),
        in_specs=[att_q, att_k, att_v, pl.BlockSpec((None, 2, TK, TQ), lambda b, h: (h, 0, 0, 0)),
                  _full((4, HEAD_DIM)), _full((LANES, TQ))],
        out_specs=att_o,
        out_shape=jax.ShapeDtypeStruct((n_tok, GROUP), BF16),
        compiler_params=_params(2),
        name="diff_attention",
    )(dqT, dk, dvT, biasT, lam_p, gsubT)

    w_out = prm['w_out'].astype(BF16)
    wr = jnp.zeros((d, LANES), F32).at[:, :N_EXPERTS].set(prm['w_router'])
    wr_hi = wr.astype(BF16)
    wr_lo = (wr - wr_hi.astype(F32)).astype(BF16)
    br = jnp.full((1, LANES), NEG_INF, F32).at[0, :N_EXPERTS].set(prm['b_router'])
    row_spec = lambda w: pl.BlockSpec((TM, w), lambda t: (t, 0))
    h1, xm, topi, gates, rank, counts = pl.pallas_call(
        _mix_kernel,
        grid=(n_tok // TM,),
        in_specs=[row_spec(d), row_spec(GROUP), row_spec(GROUP), _full((GROUP, d)), _full((GROUP, d)),
                  _full((1, d)), _full((d, LANES)), _full((d, LANES)), _full((1, LANES)), _full((TM, TM))],
        out_specs=[row_spec(d), pl.BlockSpec((TM, 1, d), lambda t: (t, 0, 0)), row_spec(LANES),
                   row_spec(LANES), row_spec(LANES), _full((1, LANES))],
        out_shape=[jax.ShapeDtypeStruct((n_tok, d), F32), jax.ShapeDtypeStruct((n_tok, 1, d), F32),
                   jax.ShapeDtypeStruct((n_tok, LANES), jnp.int32), jax.ShapeDtypeStruct((n_tok, LANES), F32),
                   jax.ShapeDtypeStruct((n_tok, LANES), jnp.int32), jax.ShapeDtypeStruct((1, LANES), F32)],
        scratch_shapes=[pltpu.VMEM((1, LANES), F32)],
        compiler_params=_params(1),
        name="mix_router",
    )(x2, fox_out, diff_out, w_out[:GROUP], w_out[GROUP:], prm['g_mlp'][None, :], wr_hi, wr_lo, br, tri_strict)

    m = n_tok * TOP_K
    n_blocks = (m + N_EXPERTS * MOE_BLOCK + MOE_BLOCK - 1) // MOE_BLOCK
    p_rows = n_blocks * MOE_BLOCK
    cnt = counts[0, :N_EXPERTS].astype(jnp.int32)
    padded = ((cnt + MOE_BLOCK - 1) // MOE_BLOCK) * MOE_BLOCK
    pad_ends = jnp.cumsum(padded)
    pad_starts = pad_ends - padded
    top_i = topi[:, :TOP_K]
    dest = pad_starts[top_i] + rank[:, :TOP_K]
    out_row = (jnp.arange(TOP_K, dtype=jnp.int32)[None, :] * n_tok
               + jnp.arange(n_tok, dtype=jnp.int32)[:, None]).reshape(m)
    asg = (m + jnp.arange(p_rows, dtype=jnp.int32) % MOE_BLOCK).at[dest.reshape(m)].set(
        out_row, unique_indices=True).reshape(n_blocks, MOE_BLOCK)
    block_start = jnp.arange(n_blocks, dtype=jnp.int32) * MOE_BLOCK
    block_expert = jnp.minimum(jnp.sum(pad_ends[None, :] <= block_start[:, None], axis=1),
                               N_EXPERTS - 1).astype(jnp.int32)
    n_used = (pad_ends[-1] // MOE_BLOCK).astype(jnp.int32)[None]

    d_ff = prm['w2'].shape[1]
    n_grp = 2 * d_ff // (2 * LANES)
    w1p = prm['w1'].reshape(N_EXPERTS, d, n_grp, LANES, 2).transpose(0, 1, 2, 4, 3)
    w1p = w1p.reshape(N_EXPERTS, d, 2 * d_ff).astype(BF16)
    b1p = prm['b1'].reshape(N_EXPERTS, n_grp, LANES, 2).transpose(0, 1, 3, 2).reshape(N_EXPERTS, 1, 2 * d_ff)
    w2b = prm['w2'].astype(BF16)
    b2 = prm['b2'].reshape(N_EXPERTS, 1, d)
    any_spec = pl.BlockSpec(memory_space=pl.ANY)
    ys = pl.pallas_call(
        _expert_kernel,
        grid_spec=pltpu.PrefetchScalarGridSpec(
            num_scalar_prefetch=2,
            grid=(n_blocks,),
            in_specs=[any_spec, any_spec, any_spec,
                      pl.BlockSpec((None, d, 2 * d_ff), lambda t, be, nu: (be[t], 0, 0)),
                      pl.BlockSpec((None, 1, 2 * d_ff), lambda t, be, nu: (be[t], 0, 0)),
                      pl.BlockSpec((None, d_ff, d), lambda t, be, nu: (be[t], 0, 0)),
                      pl.BlockSpec((None, 1, d), lambda t, be, nu: (be[t], 0, 0))],
            out_specs=any_spec,
            scratch_shapes=[pltpu.SMEM((2, MOE_BLOCK), jnp.int32), pltpu.SMEM((2, MOE_BLOCK), jnp.int32)]
            + [pltpu.VMEM((MOE_BLOCK, d), F32)] * 4
            + [pltpu.SemaphoreType.DMA((2, 2)), pltpu.SemaphoreType.DMA((2,)), pltpu.SemaphoreType.DMA((2,))],
        ),
        out_shape=jax.ShapeDtypeStruct((m + MOE_BLOCK, 1, d), F32),
        compiler_params=_params(1),
        name="experts",
    )(block_expert, n_used, asg % n_tok, asg, xm, w1p, b1p, w2b, b2)

    ple = p_i.shape[-1]
    out = pl.pallas_call(
        _final_kernel,
        grid=(n_tok // TM,),
        in_specs=[row_spec(d)]
        + [pl.BlockSpec((TM, 1, d), functools.partial(lambda k, t: (k * (n_tok // TM) + t, 0, 0), k))
           for k in range(TOP_K)]
        + [row_spec(LANES),
                  _full((1, d)), _full((d, d)), row_spec(ple), _full((ple, d))],
        out_specs=row_spec(d),
        out_shape=jax.ShapeDtypeStruct((n_tok, d), F32),
        compiler_params=_params(1),
        name="combine_ple",
    )(h1, ys, ys, ys, ys, gates, prm['g_ple'][None, :],
      prm['w_ple_gate'].astype(BF16), p_i.reshape(n_tok, ple), prm['w_ple_proj'].astype(BF16))
    return out.reshape(bsz, seq, d)


def kernel(x, p, rel_bias, g_attn, w_in, b_f, g_fox_q, g_fox_k, g_diff_q, g_diff_k, lambda_q1, lambda_k1,
           lambda_q2, lambda_k2, g_subln, w_out, g_mlp, w_router, b_router, w1, b1, w2, b2, g_ple,
           w_ple_gate, w_ple_proj):
    stacked = dict(g_attn=g_attn, w_in=w_in, b_f=b_f, g_fox_q=g_fox_q, g_fox_k=g_fox_k, g_diff_q=g_diff_q,
                   g_diff_k=g_diff_k, lambda_q1=lambda_q1, lambda_k1=lambda_k1, lambda_q2=lambda_q2,
                   lambda_k2=lambda_k2, g_subln=g_subln, w_out=w_out, g_mlp=g_mlp, w_router=w_router,
                   b_router=b_router, w1=w1, b1=b1, w2=w2, b2=b2, g_ple=g_ple, w_ple_gate=w_ple_gate,
                   w_ple_proj=w_ple_proj)
    h = x
    for i in range(p.shape[0]):
        h = _layer(i, h, p[i], rel_bias, {name: v[i] for name, v in stacked.items()})
    return h
```

```python
import functools
import math

import jax
import jax.numpy as jnp
from jax import lax
from jax.experimental import pallas as pl
from jax.experimental.pallas import tpu as pltpu

F32 = jnp.float32
BF16 = jnp.bfloat16

HEAD_DIM = 64
N_FOX_HEADS = 8
N_DIFF_HEADS = 4
GROUP = 512
LANES = 128
NUM_BUCKETS = 32
MAX_DISTANCE = 128
N_EXPERTS = 32
TOP_K = 4
SWIGLU_ALPHA = 1.702
SWIGLU_LIMIT = 7.0
MOE_BLOCK = 256
NORM_EPS = 1e-6
NEG_INF = -1e30
LOG2E = math.log2(math.e)
N_SPLIT = 3

TM = 512
TQ = 256
TK = 256
LOOKAHEAD = 2
VMEM_LIMIT = 48 * 1024 * 1024
EXPERT_VMEM_LIMIT = 58 * 1024 * 1024


def _params(n_axes):
    return pltpu.CompilerParams(dimension_semantics=("arbitrary",) * n_axes,
                                vmem_limit_bytes=VMEM_LIMIT)


def _dot(a, b):
    return jnp.dot(a, b, preferred_element_type=F32)


def _dot_nt(a, b):
    return lax.dot_general(a, b, (((1,), (1,)), ((), ())), preferred_element_type=F32)


def _split3(v):
    hi = v.astype(BF16)
    r = v - hi.astype(F32)
    mid = r.astype(BF16)
    lo = (r - mid.astype(F32)).astype(BF16)
    return hi, mid, lo


def _inproj_kernel(x_ref, g_ref, wfq_ref, wfk_ref, wfv_ref, wdq_ref, wdk_ref, wdv_ref, wf_ref, bf_ref,
                   gfq_ref, gfk_ref, gdq_ref, gdk_ref, seg_ref, tri_ref, place_ref,
                   fq_ref, fk_ref, fv_ref, dq_ref, dk_ref, dv_ref, carry_ref):
    @pl.when(pl.program_id(1) == 0)
    def _():
        carry_ref[...] = jnp.zeros_like(carry_ref)

    xf = x_ref[...]
    a = (xf * lax.rsqrt(jnp.mean(xf * xf, axis=-1, keepdims=True) + NORM_EPS) * g_ref[...]).astype(BF16)
    seg = seg_ref[...]

    def q_group(w_ref, gain_ref, out_ref):
        acc = _dot_nt(w_ref[...], a)
        ms = _dot(seg, (acc * acc).astype(BF16))
        out_ref[...] = (acc * lax.rsqrt(ms + NORM_EPS) * gain_ref[...]).astype(BF16)

    def k_group(w_ref, gain_ref):
        acc = _dot(a, w_ref[...])
        ms = _dot((acc * acc).astype(BF16), seg)
        return (acc * lax.rsqrt(ms + NORM_EPS) * gain_ref[...]).astype(BF16)

    def v_group(w_ref, out_ref):
        acc = _dot_nt(w_ref[...], a).astype(BF16)
        for hb in range(GROUP // LANES):
            for j in range(TM // TK):
                out_ref[hb, j] = acc[hb * LANES:(hb + 1) * LANES, j * TK:(j + 1) * TK]

    q_group(wfq_ref, gfq_ref, fq_ref)
    v_group(wfv_ref, fv_ref)
    q_group(wdq_ref, gdq_ref, dq_ref)
    dk_ref[...] = k_group(wdk_ref, gdk_ref)
    v_group(wdv_ref, dv_ref)

    fl = _dot(a, wf_ref[...]) + bf_ref[...]
    logf = jnp.minimum(fl, 0.0) - jnp.log1p(jnp.exp(-jnp.abs(fl)))
    tri = tri_ref[...]
    cs = carry_ref[...]
    for part in _split3(logf):
        cs = cs + _dot(tri, part)
    carry_ref[...] = cs[TM - 1:TM, :]

    fk = k_group(wfk_ref, gfk_ref)
    extra = jnp.zeros((TM, GROUP), F32)
    for t, part in enumerate(_split3(cs * LOG2E)):
        extra = extra + _dot(part, place_ref[t])
    extra = extra.astype(BF16)
    for hb in range(GROUP // LANES):
        fk_ref[:, 2 * hb * LANES:(2 * hb + 1) * LANES] = fk[:, hb * LANES:(hb + 1) * LANES]
        fk_ref[:, (2 * hb + 1) * LANES:(2 * hb + 2) * LANES] = extra[:, hb * LANES:(hb + 1) * LANES]


def _two_map_queries(qT_blk):
    row = lax.broadcasted_iota(jnp.int32, qT_blk.shape, 0)
    zero = jnp.zeros_like(qT_blk)
    return jnp.concatenate([jnp.where(row < HEAD_DIM, qT_blk, zero),
                            jnp.where(row >= HEAD_DIM, qT_blk, zero)], axis=1)


def _softmax_step(sT, vT_blk, carry):
    m, l, acc = carry
    m_new = jnp.maximum(m, jnp.max(sT, axis=0, keepdims=True))
    alpha = jnp.exp2(m - m_new)
    pT = jnp.exp2(sT - m_new)
    l = alpha * l + jnp.sum(pT, axis=0, keepdims=True)
    acc = alpha * acc + _dot(vT_blk, pT.astype(BF16))
    return m_new, l, acc


def _causal_keep():
    key = lax.broadcasted_iota(jnp.int32, (TK, 2 * TQ), 0)
    col = lax.broadcasted_iota(jnp.int32, (TK, 2 * TQ), 1)
    qry = jnp.where(col >= TQ, col - TQ, col)
    return key <= qry


def _init_carry():
    return (jnp.full((1, 2 * TQ), NEG_INF, F32), jnp.zeros((1, 2 * TQ), F32),
            jnp.zeros((LANES, 2 * TQ), F32))


def _causal_sweep(seq, k_ref, vT_ref, queries, adjust, finish):
    pairs = [(qi, kj) for qi in range(seq // TQ) for kj in range(qi + 1)]
    q_ops = {}

    def score(qi, kj):
        if qi not in q_ops:
            q_ops[qi] = queries(qi)
        return _dot(k_ref[kj * TK:(kj + 1) * TK, :], q_ops[qi])

    pending = [score(*pr) for pr in pairs[:LOOKAHEAD]]
    carry = None
    for n, (qi, kj) in enumerate(pairs):
        if n + LOOKAHEAD < len(pairs):
            pending.append(score(*pairs[n + LOOKAHEAD]))
        if kj == 0:
            carry = _init_carry()
        carry = _softmax_step(adjust(pending.pop(0), qi, kj), vT_ref[kj], carry)
        if kj == qi:
            m, l, acc = carry
            finish(qi, acc * (1.0 / l))


def _fox_kernel(qT_ref, k_ref, vT_ref, o_ref):
    seq = k_ref.shape[0]
    keep = _causal_keep()
    row = lax.broadcasted_iota(jnp.int32, (LANES, TQ), 0)
    arow = lax.broadcasted_iota(jnp.int32, (LANES, 2 * TQ), 0)
    acol = lax.broadcasted_iota(jnp.int32, (LANES, 2 * TQ), 1)
    first = (acol < TQ) & (arow < N_SPLIT)
    second = (acol >= TQ) & (arow >= N_SPLIT) & (arow < 2 * N_SPLIT)
    minus_c = jnp.where(first | second, -1.0, 0.0).astype(BF16)

    def queries(qi):
        return jnp.concatenate([_two_map_queries(qT_ref[:, qi * TQ:(qi + 1) * TQ]), minus_c], axis=0)

    def adjust(sT, qi, kj):
        return jnp.where(keep, sT, NEG_INF) if kj == qi else sT

    def finish(qi, oT):
        o = jnp.where(row < HEAD_DIM, oT[:, :TQ], oT[:, TQ:])
        o_ref[qi * TQ:(qi + 1) * TQ, :] = o.T.astype(BF16)

    _causal_sweep(seq, k_ref, vT_ref, queries, adjust, finish)


def _diff_kernel(lam_init, qT_ref, k_ref, vT_ref, bias_ref, lam_ref, gsub_ref, o_ref):
    seq = k_ref.shape[0]
    lp = lam_ref[...]
    lam = (jnp.exp(jnp.sum(lp[0:1] * lp[1:2], axis=1, keepdims=True))
           - jnp.exp(jnp.sum(lp[2:3] * lp[3:4], axis=1, keepdims=True)) + lam_init)

    def queries(qi):
        return _two_map_queries(qT_ref[:, qi * TQ:(qi + 1) * TQ])

    def adjust(sT, qi, kj):
        if kj >= qi - 1:
            b = bias_ref[qi - kj]
            sT = sT + jnp.concatenate([b, b], axis=1)
        return sT

    def finish(qi, oT):
        o = oT[:, :TQ] - lam * oT[:, TQ:]
        y = o * lax.rsqrt(jnp.mean(o * o, axis=0, keepdims=True) + NORM_EPS) * gsub_ref[...]
        o_ref[qi * TQ:(qi + 1) * TQ, :] = y.T.astype(BF16)

    _causal_sweep(seq, k_ref, vT_ref, queries, adjust, finish)


def _mix_kernel(x_ref, fox_ref, dif_ref, wa_ref, wb_ref, g_ref, wr_hi_ref, wr_lo_ref, br_ref, tri_ref,
                h_ref, xm_ref, topi_ref, gate_ref, rank_ref, cnt_ref, carry_ref):
    @pl.when(pl.program_id(0) == 0)
    def _():
        carry_ref[...] = jnp.zeros_like(carry_ref)

    h = x_ref[...] + _dot(fox_ref[...], wa_ref[...]) + _dot(dif_ref[...], wb_ref[...])
    h_ref[...] = h
    xm = h * lax.rsqrt(jnp.mean(h * h, axis=-1, keepdims=True) + NORM_EPS) * g_ref[...]
    xm_ref[:, 0, :] = xm

    x_hi = xm.astype(BF16)
    x_lo = (xm - x_hi.astype(F32)).astype(BF16)
    logits = (_dot(x_hi, wr_hi_ref[...]) + _dot(x_hi, wr_lo_ref[...]) + _dot(x_lo, wr_hi_ref[...])
              + br_ref[...])

    lane = lax.broadcasted_iota(jnp.int32, logits.shape, 1)
    vals, sels, idxs = [], [], []
    for _ in range(TOP_K):
        mx = jnp.max(logits, axis=1, keepdims=True)
        idx = jnp.min(jnp.where(logits == mx, lane, LANES), axis=1, keepdims=True)
        sel = lane == idx
        logits = jnp.where(sel, -jnp.inf, logits)
        vals.append(mx)
        sels.append(sel)
        idxs.append(idx)
    exps = [jnp.exp(v - vals[0]) for v in vals]
    denom = exps[0] + exps[1] + exps[2] + exps[3]

    multi_hot = sum(s.astype(F32) for s in sels)
    before = _dot(tri_ref[...], multi_hot.astype(BF16)) + carry_ref[...]
    topi = jnp.zeros(logits.shape, jnp.int32)
    gate = jnp.zeros(logits.shape, F32)
    rank = jnp.zeros(logits.shape, F32)
    for k in range(TOP_K):
        rk = jnp.sum(jnp.where(sels[k], before, 0.0), axis=1, keepdims=True)
        topi = jnp.where(lane == k, idxs[k], topi)
        gate = jnp.where(lane == k, exps[k] / denom, gate)
        rank = jnp.where(lane == k, rk, rank)
    topi_ref[...] = topi
    gate_ref[...] = gate
    rank_ref[...] = rank.astype(jnp.int32)
    carry_ref[...] = carry_ref[...] + jnp.sum(multi_hot, axis=0, keepdims=True)
    cnt_ref[...] = carry_ref[...]


def _expert_kernel(be_ref, nu_ref, tok_hbm, asg_hbm, x_hbm, w1f_ref, b1_ref, w2f_ref, b2_ref, perm_ref, y_hbm,
                   tok_ref, asg_ref, x0, x1, y0, y1, w1_ref, w2_ref, idx_sem, g_sem, s_sem):
    i = pl.program_id(0)
    n_used = nu_ref[0]
    n_blocks = asg_hbm.shape[0]
    spare_row = y_hbm.shape[0] - MOE_BLOCK
    xbufs, ybufs = (x0, x1), (y0, y1)

    def tok_copy(blk, slot):
        return pltpu.make_async_copy(tok_hbm.at[blk], tok_ref.at[slot], idx_sem.at[0, slot])

    def asg_copy(blk, slot):
        return pltpu.make_async_copy(asg_hbm.at[blk], asg_ref.at[slot], idx_sem.at[1, slot])

    def row_loop(fn):
        for r in range(MOE_BLOCK):
            fn(r)

    def gather_issue(slot):
        row_loop(lambda r: pltpu.make_async_copy(x_hbm.at[tok_ref[slot, r]], xbufs[slot].at[pl.ds(r, 1)],
                                                 g_sem.at[slot]).start())

    def gather_wait(slot):
        pltpu.make_async_copy(xbufs[slot], xbufs[slot], g_sem.at[slot]).wait()

    def scatter_issue(slot):
        row_loop(lambda r: pltpu.make_async_copy(ybufs[slot].at[pl.ds(r, 1)], y_hbm.at[asg_ref[slot, r]],
                                                 s_sem.at[slot]).start())

    def scatter_wait(slot):
        pltpu.make_async_copy(ybufs[slot], ybufs[slot], s_sem.at[slot]).wait()

    @pl.when(i == 0)
    def _():
        y1[...] = jnp.zeros(y1.shape, y1.dtype)
        for r in range(MOE_BLOCK):
            asg_ref[1, r] = spare_row + r
        tok_copy(0, 0).start()
        tok_copy(0, 0).wait()
        tok_copy(jnp.minimum(1, n_blocks - 1), 1).start()
        gather_issue(0)

    @pl.when((i < n_used) & ((i == 0) | (be_ref[i] != be_ref[jnp.maximum(i - 1, 0)])))
    def _():
        for g in range(w1f_ref.shape[1] // (2 * LANES)):
            cols = slice(2 * g * LANES, 2 * (g + 1) * LANES)
            w1_ref[:, cols] = _dot(w1f_ref[:, cols].astype(BF16), perm_ref[...]).astype(BF16)
        w2_ref[...] = w2f_ref[...].astype(BF16)

    def step(cur):
        nxt = 1 - cur
        asg_copy(i, cur).start()
        tok_copy(0, nxt).wait()
        gather_issue(nxt)
        tok_copy(jnp.minimum(i + 2, n_blocks - 1), cur).start()

        @pl.when(i >= 1)
        def _():
            asg_copy(0, nxt).wait()
            scatter_wait(cur)

        scatter_issue(nxt)
        gather_wait(cur)
        xbuf, ybuf = xbufs[cur], ybufs[cur]
        xs = xbuf[...].astype(BF16)
        h = _dot(xs, w1_ref[...]) + b1_ref[...]
        acts = []
        for c in range(h.shape[1] // (2 * LANES)):
            glu = jnp.minimum(h[:, 2 * c * LANES:(2 * c + 1) * LANES], SWIGLU_LIMIT)
            lin = jnp.clip(h[:, (2 * c + 1) * LANES:(2 * c + 2) * LANES], -SWIGLU_LIMIT, SWIGLU_LIMIT)
            acts.append((glu * jax.nn.sigmoid(SWIGLU_ALPHA * glu) * (lin + 1.0)).astype(BF16))
        out = _dot(jnp.concatenate(acts, axis=1), w2_ref[...]) + b2_ref[...]
        ybuf[...] = out

        @pl.when(i == n_used - 1)
        def _():
            asg_copy(0, cur).wait()
            scatter_issue(cur)
            scatter_wait(nxt)
            scatter_wait(cur)
            gather_wait(nxt)
            tok_copy(0, cur).wait()

    for parity in range(2):
        pl.when((i < n_used) & (i % 2 == parity))(functools.partial(step, parity))


def _final_kernel(h_ref, *rest):
    y_refs, (gate_ref, g_ref, wg_ref, p_ref, wp_ref, o_ref) = rest[:TOP_K], rest[TOP_K:]
    gates = gate_ref[...]
    moe = sum(gates[:, k:k + 1] * y_ref[:, 0, :] for k, y_ref in enumerate(y_refs))
    h = h_ref[...] + moe
    n = (h * lax.rsqrt(jnp.mean(h * h, axis=-1, keepdims=True) + NORM_EPS) * g_ref[...]).astype(BF16)
    gate = jax.nn.sigmoid(_dot(n, wg_ref[...]))
    o_ref[...] = h + gate * _dot(p_ref[...].astype(BF16), wp_ref[...])


def _t5_bucket(rel):
    n = jnp.maximum(rel, 0)
    max_exact = NUM_BUCKETS // 2
    nf = jnp.maximum(n, 1).astype(F32)
    large = max_exact + (jnp.log(nf / max_exact) / math.log(MAX_DISTANCE / max_exact)
                         * (NUM_BUCKETS - max_exact)).astype(jnp.int32)
    large = jnp.minimum(large, NUM_BUCKETS - 1)
    return jnp.where(n < max_exact, n, large)


def _full(shape):
    return pl.BlockSpec(shape, lambda *_: (0,) * len(shape))


def _layer(i, h_in, p_i, rel_bias, prm):
    bsz, seq, d = h_in.shape
    n_tok = bsz * seq
    ns = seq // TM
    nk = seq // TK
    x2 = h_in.reshape(n_tok, d)

    w_in = prm['w_in']
    fox_w = N_FOX_HEADS * HEAD_DIM
    offs = [0, fox_w, 2 * fox_w, 3 * fox_w, 3 * fox_w + N_FOX_HEADS]
    offs += [offs[4] + GROUP, offs[4] + 2 * GROUP, offs[4] + 3 * GROUP]
    col = lambda a, b: w_in[:, a:b].astype(BF16)
    wfqT, wfk, wfvT = col(offs[0], offs[1]).T, col(offs[1], offs[2]), col(offs[2], offs[3]).T
    wdqT, wdk, wdvT = col(offs[4], offs[5]).T, col(offs[5], offs[6]), col(offs[6], offs[7]).T
    wf = jnp.zeros((d, LANES), BF16).at[:, :N_FOX_HEADS].set(col(offs[3], offs[4]))
    bf = jnp.zeros((1, LANES), F32).at[0, :N_FOX_HEADS].set(prm['b_f'])
    scale = HEAD_DIM ** -0.5 * LOG2E
    rep = GROUP // HEAD_DIM
    heads = jnp.arange(N_FOX_HEADS)
    place = jnp.stack([jnp.zeros((LANES, GROUP), BF16)
                       .at[heads, (heads // 2) * LANES + (heads % 2) * N_SPLIT + t].set(1.0)
                       for t in range(N_SPLIT)])
    gfqT = jnp.broadcast_to((jnp.tile(prm['g_fox_q'], rep) * scale)[:, None], (GROUP, TM))
    gdqT = jnp.broadcast_to((jnp.tile(prm['g_diff_q'], rep) * scale)[:, None], (GROUP, TM))
    gfk = jnp.tile(prm['g_fox_k'], rep)[None, :]
    gdk = jnp.tile(prm['g_diff_k'], rep)[None, :]
    gi = jnp.arange(GROUP) // HEAD_DIM
    seg = jnp.where(gi[:, None] == gi[None, :], 1.0 / HEAD_DIM, 0.0).astype(BF16)
    ti = jnp.arange(TM)
    tri_incl = (ti[None, :] <= ti[:, None]).astype(BF16)
    tri_strict = (ti[None, :] < ti[:, None]).astype(BF16)

    w_spec_t = _full((GROUP, d))
    w_spec = _full((d, GROUP))
    qT_shape = jax.ShapeDtypeStruct((bsz, GROUP, seq), BF16)
    k_shape = jax.ShapeDtypeStruct((n_tok, GROUP), BF16)
    vT_shape = jax.ShapeDtypeStruct((bsz, GROUP // LANES, nk, LANES, TK), BF16)
    qT_spec = pl.BlockSpec((None, GROUP, TM), lambda b, s: (b, 0, s))
    k_spec = pl.BlockSpec((TM, GROUP), lambda b, s: (b * ns + s, 0))
    vT_spec = pl.BlockSpec((None, GROUP // LANES, TM // TK, LANES, TK), lambda b, s: (b, 0, s, 0, 0))
    fk_shape = jax.ShapeDtypeStruct((n_tok, 2 * GROUP), BF16)
    fk_spec = pl.BlockSpec((TM, 2 * GROUP), lambda b, s: (b * ns + s, 0))
    fqT, fk, fvT, dqT, dk, dvT = pl.pallas_call(
        _inproj_kernel,
        grid=(bsz, ns),
        in_specs=[pl.BlockSpec((TM, d), lambda b, s: (b * ns + s, 0)), _full((1, d)),
                  w_spec_t, w_spec, w_spec_t, w_spec_t, w_spec, w_spec_t, _full((d, LANES)), _full((1, LANES)),
                  _full((GROUP, TM)), _full((1, GROUP)), _full((GROUP, TM)), _full((1, GROUP)),
                  _full((GROUP, GROUP)), _full((TM, TM)), _full((N_SPLIT, LANES, GROUP))],
        out_specs=[qT_spec, fk_spec, vT_spec, qT_spec, k_spec, vT_spec],
        out_shape=[qT_shape, fk_shape, vT_shape, qT_shape, k_shape, vT_shape],
        scratch_shapes=[pltpu.VMEM((1, LANES), F32)],
        compiler_params=_params(2),
        name="inproj",
    )(x2, prm['g_attn'][None, :], wfqT, wfk, wfvT, wdqT, wdk, wdvT, wf, bf, gfqT, gfk, gdqT, gdk, seg, tri_incl,
      place)

    n_hb = GROUP // LANES
    att_q = pl.BlockSpec((None, LANES, seq), lambda b, h: (b, h, 0))
    att_k = pl.BlockSpec((seq, LANES), lambda b, h: (b, h))
    att_v = pl.BlockSpec((None, None, nk, LANES, TK), lambda b, h: (b, h, 0, 0, 0))
    att_o = pl.BlockSpec((seq, LANES), lambda b, h: (b, h))
    fox_out = pl.pallas_call(
        _fox_kernel,
        grid=(bsz, n_hb),
        in_specs=[att_q, pl.BlockSpec((seq, 2 * LANES), lambda b, h: (b, h)), att_v],
        out_specs=att_o,
        out_shape=jax.ShapeDtypeStruct((n_tok, GROUP), BF16),
        compiler_params=_params(2),
        name="fox_attention",
    )(fqT, fk, fvT)

    kpos = jnp.arange(TK)[:, None]
    qpos = jnp.arange(TQ)[None, :]
    rel = jnp.stack([qpos - kpos, qpos - kpos + TK])
    assert TK + 1 >= MAX_DISTANCE and TQ == TK
    table = (rel_bias.astype(F32) - rel_bias[NUM_BUCKETS - 1].astype(F32)[None, :]) * LOG2E
    biasT = jnp.transpose(jnp.take(table, _t5_bucket(rel), axis=0), (3, 0, 1, 2))
    biasT = jnp.where((rel >= 0)[None], biasT, NEG_INF)
    lam_init = 0.8 - 0.6 * math.exp(-0.3 * i)
    lam_p = jnp.stack([prm['lambda_q1'], prm['lambda_k1'], prm['lambda_q2'], prm['lambda_k2']]).astype(F32)
    gsubT = jnp.broadcast_to((prm['g_subln'] * (1.0 - lam_init))[:, None], (LANES, TQ))
    diff_out = pl.pallas_call(
        functools.partial(_diff_kernel, lam_init),
        grid=(bsz, N_DIFF_HEADS),
        in_specs=[att_q, att_k, att_v, pl.BlockSpec((None, 2, TK, TQ), lambda b, h: (h, 0, 0, 0)),
                  _full((4, HEAD_DIM)), _full((LANES, TQ))],
        out_specs=att_o,
        out_shape=jax.ShapeDtypeStruct((n_tok, GROUP), BF16),
        compiler_params=_params(2),
        name="diff_attention",
    )(dqT, dk, dvT, biasT, lam_p, gsubT)

    w_out = prm['w_out'].astype(BF16)
    wr = jnp.zeros((d, LANES), F32).at[:, :N_EXPERTS].set(prm['w_router'])
    wr_hi = wr.astype(BF16)
    wr_lo = (wr - wr_hi.astype(F32)).astype(BF16)
    br = jnp.full((1, LANES), NEG_INF, F32).at[0, :N_EXPERTS].set(prm['b_router'])
    row_spec = lambda w: pl.BlockSpec((TM, w), lambda t: (t, 0))
    h1, xm, topi, gates, rank, counts = pl.pallas_call(
        _mix_kernel,
        grid=(n_tok // TM,),
        in_specs=[row_spec(d), row_spec(GROUP), row_spec(GROUP), _full((GROUP, d)), _full((GROUP, d)),
                  _full((1, d)), _full((d, LANES)), _full((d, LANES)), _full((1, LANES)), _full((TM, TM))],
        out_specs=[row_spec(d), pl.BlockSpec((TM, 1, d), lambda t: (t, 0, 0)), row_spec(LANES),
                   row_spec(LANES), row_spec(LANES), _full((1, LANES))],
        out_shape=[jax.ShapeDtypeStruct((n_tok, d), F32), jax.ShapeDtypeStruct((n_tok, 1, d), F32),
                   jax.ShapeDtypeStruct((n_tok, LANES), jnp.int32), jax.ShapeDtypeStruct((n_tok, LANES), F32),
                   jax.ShapeDtypeStruct((n_tok, LANES), jnp.int32), jax.ShapeDtypeStruct((1, LANES), F32)],
        scratch_shapes=[pltpu.VMEM((1, LANES), F32)],
        compiler_params=_params(1),
        name="mix_router",
    )(x2, fox_out, diff_out, w_out[:GROUP], w_out[GROUP:], prm['g_mlp'][None, :], wr_hi, wr_lo, br, tri_strict)

    m = n_tok * TOP_K
    n_blocks = (m + N_EXPERTS * MOE_BLOCK + MOE_BLOCK - 1) // MOE_BLOCK
    p_rows = n_blocks * MOE_BLOCK
    cnt = counts[0, :N_EXPERTS].astype(jnp.int32)
    padded = ((cnt + MOE_BLOCK - 1) // MOE_BLOCK) * MOE_BLOCK
    pad_ends = jnp.cumsum(padded)
    pad_starts = pad_ends - padded
    top_i = topi[:, :TOP_K]
    start_of = jnp.sum(jnp.where(top_i[:, :, None] == jnp.arange(N_EXPERTS, dtype=jnp.int32), pad_starts, 0), axis=-1)
    dest = start_of + rank[:, :TOP_K]
    out_row = (jnp.arange(TOP_K, dtype=jnp.int32)[None, :] * n_tok
               + jnp.arange(n_tok, dtype=jnp.int32)[:, None]).reshape(m)
    asg = (m + jnp.arange(p_rows, dtype=jnp.int32) % MOE_BLOCK).at[dest.reshape(m)].set(
        out_row, unique_indices=True).reshape(n_blocks, MOE_BLOCK)
    block_start = jnp.arange(n_blocks, dtype=jnp.int32) * MOE_BLOCK
    block_expert = jnp.minimum(jnp.sum(pad_ends[None, :] <= block_start[:, None], axis=1),
                               N_EXPERTS - 1).astype(jnp.int32)
    n_used = (pad_ends[-1] // MOE_BLOCK).astype(jnp.int32)[None]

    d_ff = prm['w2'].shape[1]
    n_grp = 2 * d_ff // (2 * LANES)
    b1p = prm['b1'].reshape(N_EXPERTS, n_grp, LANES, 2).transpose(0, 1, 3, 2).reshape(N_EXPERTS, 1, 2 * d_ff)
    b2 = prm['b2'].reshape(N_EXPERTS, 1, d)
    pj = jnp.arange(LANES)
    perm = (jnp.zeros((2 * LANES, 2 * LANES), BF16).at[2 * pj, pj].set(1.0)
            .at[2 * pj + 1, LANES + pj].set(1.0))
    any_spec = pl.BlockSpec(memory_space=pl.ANY)
    ys = pl.pallas_call(
        _expert_kernel,
        grid_spec=pltpu.PrefetchScalarGridSpec(
            num_scalar_prefetch=2,
            grid=(n_blocks,),
            in_specs=[any_spec, any_spec, any_spec,
                      pl.BlockSpec((None, d, 2 * d_ff), lambda t, be, nu: (be[t], 0, 0)),
                      pl.BlockSpec((None, 1, 2 * d_ff), lambda t, be, nu: (be[t], 0, 0)),
                      pl.BlockSpec((None, d_ff, d), lambda t, be, nu: (be[t], 0, 0)),
                      pl.BlockSpec((None, 1, d), lambda t, be, nu: (be[t], 0, 0)),
                      pl.BlockSpec((2 * LANES, 2 * LANES), lambda t, be, nu: (0, 0))],
            out_specs=any_spec,
            scratch_shapes=[pltpu.SMEM((2, MOE_BLOCK), jnp.int32), pltpu.SMEM((2, MOE_BLOCK), jnp.int32)]
            + [pltpu.VMEM((MOE_BLOCK, d), F32)] * 4
            + [pltpu.VMEM((d, 2 * d_ff), BF16), pltpu.VMEM((d_ff, d), BF16)]
            + [pltpu.SemaphoreType.DMA((2, 2)), pltpu.SemaphoreType.DMA((2,)), pltpu.SemaphoreType.DMA((2,))],
        ),
        out_shape=jax.ShapeDtypeStruct((m + MOE_BLOCK, 1, d), F32),
        compiler_params=pltpu.CompilerParams(dimension_semantics=("arbitrary",),
                                             vmem_limit_bytes=EXPERT_VMEM_LIMIT),
        name="experts",
    )(block_expert, n_used, asg % n_tok, asg, xm, prm['w1'], b1p, prm['w2'], b2, perm)

    ple = p_i.shape[-1]
    out = pl.pallas_call(
        _final_kernel,
        grid=(n_tok // TM,),
        in_specs=[row_spec(d)]
        + [pl.BlockSpec((TM, 1, d), functools.partial(lambda k, t: (k * (n_tok // TM) + t, 0, 0), k))
           for k in range(TOP_K)]
        + [row_spec(LANES),
                  _full((1, d)), _full((d, d)), row_spec(ple), _full((ple, d))],
        out_specs=row_spec(d),
        out_shape=jax.ShapeDtypeStruct((n_tok, d), F32),
        compiler_params=_params(1),
        name="combine_ple",
    )(h1, ys, ys, ys, ys, gates, prm['g_ple'][None, :],
      prm['w_ple_gate'].astype(BF16), p_i.reshape(n_tok, ple), prm['w_ple_proj'].astype(BF16))
    return out.reshape(bsz, seq, d)


def kernel(x, p, rel_bias, g_attn, w_in, b_f, g_fox_q, g_fox_k, g_diff_q, g_diff_k, lambda_q1, lambda_k1,
           lambda_q2, lambda_k2, g_subln, w_out, g_mlp, w_router, b_router, w1, b1, w2, b2, g_ple,
           w_ple_gate, w_ple_proj):
    stacked = dict(g_attn=g_attn, w_in=w_in, b_f=b_f, g_fox_q=g_fox_q, g_fox_k=g_fox_k, g_diff_q=g_diff_q,
                   g_diff_k=g_diff_k, lambda_q1=lambda_q1, lambda_k1=lambda_k1, lambda_q2=lambda_q2,
                   lambda_k2=lambda_k2, g_subln=g_subln, w_out=w_out, g_mlp=g_mlp, w_router=w_router,
                   b_router=b_router, w1=w1, b1=b1, w2=w2, b2=b2, g_ple=g_ple, w_ple_gate=w_ple_gate,
                   w_ple_proj=w_ple_proj)
    h = x
    for i in range(p.shape[0]):
        h = _layer(i, h, p[i], rel_bias, {name: v[i] for name, v in stacked.items()})
    return h
```

```python
import functools
import math

import jax
import jax.numpy as jnp
from jax import lax
from jax.experimental import pallas as pl
from jax.experimental.pallas import tpu as pltpu

F32 = jnp.float32
BF16 = jnp.bfloat16

HEAD_DIM = 64
N_FOX_HEADS = 8
N_DIFF_HEADS = 4
GROUP = 512
LANES = 128
NUM_BUCKETS = 32
MAX_DISTANCE = 128
N_EXPERTS = 32
TOP_K = 4
SWIGLU_ALPHA = 1.702
SWIGLU_LIMIT = 7.0
MOE_BLOCK = 256
NORM_EPS = 1e-6
NEG_INF = -1e30
LOG2E = math.log2(math.e)
N_SPLIT = 3

TM = 512
TQ = 256
TK = 256
LOOKAHEAD = 2
VMEM_LIMIT = 48 * 1024 * 1024
EXPERT_VMEM_LIMIT = 58 * 1024 * 1024


def _params(n_axes):
    return pltpu.CompilerParams(dimension_semantics=("arbitrary",) * n_axes,
                                vmem_limit_bytes=VMEM_LIMIT)


def _dot(a, b):
    return jnp.dot(a, b, preferred_element_type=F32)


def _dot_nt(a, b):
    return lax.dot_general(a, b, (((1,), (1,)), ((), ())), preferred_element_type=F32)


def _split3(v):
    hi = v.astype(BF16)
    r = v - hi.astype(F32)
    mid = r.astype(BF16)
    lo = (r - mid.astype(F32)).astype(BF16)
    return hi, mid, lo


def _inproj_kernel(x_ref, g_ref, wfq_ref, wfk_ref, wfv_ref, wdq_ref, wdk_ref, wdv_ref, wf_ref, bf_ref,
                   gfq_ref, gfk_ref, gdq_ref, gdk_ref, seg_ref, tri_ref, place_ref,
                   fq_ref, fk_ref, fv_ref, dq_ref, dk_ref, dv_ref, carry_ref):
    @pl.when(pl.program_id(1) == 0)
    def _():
        carry_ref[...] = jnp.zeros_like(carry_ref)

    xf = x_ref[...]
    a = (xf * lax.rsqrt(jnp.mean(xf * xf, axis=-1, keepdims=True) + NORM_EPS) * g_ref[...]).astype(BF16)
    seg = seg_ref[...]

    def q_group(w_ref, gain_ref, out_ref):
        acc = _dot_nt(w_ref[...], a)
        ms = _dot(seg, (acc * acc).astype(BF16))
        out_ref[...] = (acc * lax.rsqrt(ms + NORM_EPS) * gain_ref[...]).astype(BF16)

    def k_group(w_ref, gain_ref):
        acc = _dot(a, w_ref[...])
        ms = _dot((acc * acc).astype(BF16), seg)
        return (acc * lax.rsqrt(ms + NORM_EPS) * gain_ref[...]).astype(BF16)

    def v_group(w_ref, out_ref):
        acc = _dot_nt(w_ref[...], a).astype(BF16)
        for hb in range(GROUP // LANES):
            for j in range(TM // TK):
                out_ref[hb, j] = acc[hb * LANES:(hb + 1) * LANES, j * TK:(j + 1) * TK]

    q_group(wfq_ref, gfq_ref, fq_ref)
    v_group(wfv_ref, fv_ref)
    q_group(wdq_ref, gdq_ref, dq_ref)
    dk_ref[...] = k_group(wdk_ref, gdk_ref)
    v_group(wdv_ref, dv_ref)

    fl = _dot(a, wf_ref[...]) + bf_ref[...]
    logf = jnp.minimum(fl, 0.0) - jnp.log1p(jnp.exp(-jnp.abs(fl)))
    tri = tri_ref[...]
    cs = carry_ref[...]
    for part in _split3(logf):
        cs = cs + _dot(tri, part)
    carry_ref[...] = cs[TM - 1:TM, :]

    fk = k_group(wfk_ref, gfk_ref)
    extra = jnp.zeros((TM, GROUP), F32)
    for t, part in enumerate(_split3(cs * LOG2E)):
        extra = extra + _dot(part, place_ref[t])
    extra = extra.astype(BF16)
    for hb in range(GROUP // LANES):
        fk_ref[:, 2 * hb * LANES:(2 * hb + 1) * LANES] = fk[:, hb * LANES:(hb + 1) * LANES]
        fk_ref[:, (2 * hb + 1) * LANES:(2 * hb + 2) * LANES] = extra[:, hb * LANES:(hb + 1) * LANES]


def _two_map_queries(qT_blk):
    row = lax.broadcasted_iota(jnp.int32, qT_blk.shape, 0)
    zero = jnp.zeros_like(qT_blk)
    return jnp.concatenate([jnp.where(row < HEAD_DIM, qT_blk, zero),
                            jnp.where(row >= HEAD_DIM, qT_blk, zero)], axis=1)


def _softmax_step(sT, vT_blk, carry):
    m, l, acc = carry
    m_new = jnp.maximum(m, jnp.max(sT, axis=0, keepdims=True))
    alpha = jnp.exp2(m - m_new)
    pT = jnp.exp2(sT - m_new)
    l = alpha * l + jnp.sum(pT, axis=0, keepdims=True)
    acc = alpha * acc + _dot(vT_blk, pT.astype(BF16))
    return m_new, l, acc


def _causal_keep():
    key = lax.broadcasted_iota(jnp.int32, (TK, 2 * TQ), 0)
    col = lax.broadcasted_iota(jnp.int32, (TK, 2 * TQ), 1)
    qry = jnp.where(col >= TQ, col - TQ, col)
    return key <= qry


def _init_carry():
    return (jnp.full((1, 2 * TQ), NEG_INF, F32), jnp.zeros((1, 2 * TQ), F32),
            jnp.zeros((LANES, 2 * TQ), F32))


def _causal_sweep(seq, k_ref, vT_ref, queries, adjust, finish):
    pairs = [(qi, kj) for qi in range(seq // TQ) for kj in range(qi + 1)]
    q_ops = {}

    def score(qi, kj):
        if qi not in q_ops:
            q_ops[qi] = queries(qi)
        return _dot(k_ref[kj * TK:(kj + 1) * TK, :], q_ops[qi])

    pending = [score(*pr) for pr in pairs[:LOOKAHEAD]]
    carry = None
    for n, (qi, kj) in enumerate(pairs):
        if n + LOOKAHEAD < len(pairs):
            pending.append(score(*pairs[n + LOOKAHEAD]))
        if kj == 0:
            carry = _init_carry()
        carry = _softmax_step(adjust(pending.pop(0), qi, kj), vT_ref[kj], carry)
        if kj == qi:
            m, l, acc = carry
            finish(qi, acc * (1.0 / l))


def _fox_kernel(qT_ref, k_ref, vT_ref, o_ref):
    seq = k_ref.shape[0]
    keep = _causal_keep()
    row = lax.broadcasted_iota(jnp.int32, (LANES, TQ), 0)
    arow = lax.broadcasted_iota(jnp.int32, (LANES, 2 * TQ), 0)
    acol = lax.broadcasted_iota(jnp.int32, (LANES, 2 * TQ), 1)
    first = (acol < TQ) & (arow < N_SPLIT)
    second = (acol >= TQ) & (arow >= N_SPLIT) & (arow < 2 * N_SPLIT)
    minus_c = jnp.where(first | second, -1.0, 0.0).astype(BF16)

    def queries(qi):
        return jnp.concatenate([_two_map_queries(qT_ref[:, qi * TQ:(qi + 1) * TQ]), minus_c], axis=0)

    def adjust(sT, qi, kj):
        return jnp.where(keep, sT, NEG_INF) if kj == qi else sT

    def finish(qi, oT):
        o = jnp.where(row < HEAD_DIM, oT[:, :TQ], oT[:, TQ:])
        o_ref[qi * TQ:(qi + 1) * TQ, :] = o.T.astype(BF16)

    _causal_sweep(seq, k_ref, vT_ref, queries, adjust, finish)


def _diff_kernel(lam_init, qT_ref, k_ref, vT_ref, bias_ref, lam_ref, gsub_ref, o_ref):
    seq = k_ref.shape[0]
    lp = lam_ref[...]
    lam = (jnp.exp(jnp.sum(lp[0:1] * lp[1:2], axis=1, keepdims=True))
           - jnp.exp(jnp.sum(lp[2:3] * lp[3:4], axis=1, keepdims=True)) + lam_init)

    def queries(qi):
        return _two_map_queries(qT_ref[:, qi * TQ:(qi + 1) * TQ])

    def adjust(sT, qi, kj):
        if kj >= qi - 1:
            b = bias_ref[qi - kj]
            sT = sT + jnp.concatenate([b, b], axis=1)
        return sT

    def finish(qi, oT):
        o = oT[:, :TQ] - lam * oT[:, TQ:]
        y = o * lax.rsqrt(jnp.mean(o * o, axis=0, keepdims=True) + NORM_EPS) * gsub_ref[...]
        o_ref[qi * TQ:(qi + 1) * TQ, :] = y.T.astype(BF16)

    _causal_sweep(seq, k_ref, vT_ref, queries, adjust, finish)


def _mix_kernel(x_ref, fox_ref, dif_ref, wa_ref, wb_ref, g_ref, wr_hi_ref, wr_lo_ref, br_ref, tri_ref,
                h_ref, xm_ref, topi_ref, gate_ref, rank_ref, cnt_ref, carry_ref):
    @pl.when(pl.program_id(0) == 0)
    def _():
        carry_ref[...] = jnp.zeros_like(carry_ref)

    h = x_ref[...] + _dot(fox_ref[...], wa_ref[...]) + _dot(dif_ref[...], wb_ref[...])
    h_ref[...] = h
    xm = h * lax.rsqrt(jnp.mean(h * h, axis=-1, keepdims=True) + NORM_EPS) * g_ref[...]
    xm_ref[:, 0, :] = xm

    x_hi = xm.astype(BF16)
    x_lo = (xm - x_hi.astype(F32)).astype(BF16)
    logits = (_dot(x_hi, wr_hi_ref[...]) + _dot(x_hi, wr_lo_ref[...]) + _dot(x_lo, wr_hi_ref[...])
              + br_ref[...])

    lane = lax.broadcasted_iota(jnp.int32, logits.shape, 1)
    vals, sels, idxs = [], [], []
    for _ in range(TOP_K):
        mx = jnp.max(logits, axis=1, keepdims=True)
        idx = jnp.min(jnp.where(logits == mx, lane, LANES), axis=1, keepdims=True)
        sel = lane == idx
        logits = jnp.where(sel, -jnp.inf, logits)
        vals.append(mx)
        sels.append(sel)
        idxs.append(idx)
    exps = [jnp.exp(v - vals[0]) for v in vals]
    denom = exps[0] + exps[1] + exps[2] + exps[3]

    multi_hot = sum(s.astype(F32) for s in sels)
    before = _dot(tri_ref[...], multi_hot.astype(BF16)) + carry_ref[...]
    topi = jnp.zeros(logits.shape, jnp.int32)
    gate = jnp.zeros(logits.shape, F32)
    rank = jnp.zeros(logits.shape, F32)
    for k in range(TOP_K):
        rk = jnp.sum(jnp.where(sels[k], before, 0.0), axis=1, keepdims=True)
        topi = jnp.where(lane == k, idxs[k], topi)
        gate = jnp.where(lane == k, exps[k] / denom, gate)
        rank = jnp.where(lane == k, rk, rank)
    topi_ref[...] = topi
    gate_ref[...] = gate
    rank_ref[...] = rank.astype(jnp.int32)
    carry_ref[...] = carry_ref[...] + jnp.sum(multi_hot, axis=0, keepdims=True)
    cnt_ref[...] = carry_ref[...]


def _expert_kernel(be_ref, nu_ref, tok_hbm, asg_hbm, x_hbm, w1f_ref, b1_ref, w2f_ref, b2_ref, perm_ref, y_hbm,
                   tok_ref, asg_ref, x0, x1, y0, y1, w1_ref, w2_ref, idx_sem, g_sem, s_sem):
    i = pl.program_id(0)
    n_used = nu_ref[0]
    n_blocks = asg_hbm.shape[0]
    spare_row = y_hbm.shape[0] - MOE_BLOCK
    xbufs, ybufs = (x0, x1), (y0, y1)

    def tok_copy(blk, slot):
        return pltpu.make_async_copy(tok_hbm.at[blk], tok_ref.at[slot], idx_sem.at[0, slot])

    def asg_copy(blk, slot):
        return pltpu.make_async_copy(asg_hbm.at[blk], asg_ref.at[slot], idx_sem.at[1, slot])

    def row_loop(fn):
        for r in range(MOE_BLOCK):
            fn(r)

    def gather_issue(slot):
        row_loop(lambda r: pltpu.make_async_copy(x_hbm.at[tok_ref[slot, r]], xbufs[slot].at[pl.ds(r, 1)],
                                                 g_sem.at[slot]).start())

    def gather_wait(slot):
        pltpu.make_async_copy(xbufs[slot], xbufs[slot], g_sem.at[slot]).wait()

    def scatter_issue(slot):
        row_loop(lambda r: pltpu.make_async_copy(ybufs[slot].at[pl.ds(r, 1)], y_hbm.at[asg_ref[slot, r]],
                                                 s_sem.at[slot]).start())

    def scatter_wait(slot):
        pltpu.make_async_copy(ybufs[slot], ybufs[slot], s_sem.at[slot]).wait()

    @pl.when(i == 0)
    def _():
        y1[...] = jnp.zeros(y1.shape, y1.dtype)
        for r in range(MOE_BLOCK):
            asg_ref[1, r] = spare_row + r
        tok_copy(0, 0).start()
        tok_copy(0, 0).wait()
        tok_copy(jnp.minimum(1, n_blocks - 1), 1).start()
        gather_issue(0)

    @pl.when((i < n_used) & ((i == 0) | (be_ref[i] != be_ref[jnp.maximum(i - 1, 0)])))
    def _():
        for g in range(w1f_ref.shape[1] // (2 * LANES)):
            cols = slice(2 * g * LANES, 2 * (g + 1) * LANES)
            w1_ref[:, cols] = _dot(w1f_ref[:, cols].astype(BF16), perm_ref[...]).astype(BF16)
        w2_ref[...] = w2f_ref[...].astype(BF16)

    def step(cur):
        nxt = 1 - cur
        asg_copy(i, cur).start()
        tok_copy(0, nxt).wait()
        gather_issue(nxt)
        tok_copy(jnp.minimum(i + 2, n_blocks - 1), cur).start()

        @pl.when(i >= 1)
        def _():
            asg_copy(0, nxt).wait()
            scatter_wait(cur)

        scatter_issue(nxt)
        gather_wait(cur)
        xbuf, ybuf = xbufs[cur], ybufs[cur]
        xs = xbuf[...].astype(BF16)
        h = _dot(xs, w1_ref[...]) + b1_ref[...]
        acts = []
        for c in range(h.shape[1] // (2 * LANES)):
            glu = jnp.minimum(h[:, 2 * c * LANES:(2 * c + 1) * LANES], SWIGLU_LIMIT)
            lin = jnp.clip(h[:, (2 * c + 1) * LANES:(2 * c + 2) * LANES], -SWIGLU_LIMIT, SWIGLU_LIMIT)
            acts.append((glu * jax.nn.sigmoid(SWIGLU_ALPHA * glu) * (lin + 1.0)).astype(BF16))
        out = _dot(jnp.concatenate(acts, axis=1), w2_ref[...]) + b2_ref[...]
        ybuf[...] = out

        @pl.when(i == n_used - 1)
        def _():
            asg_copy(0, cur).wait()
            scatter_issue(cur)
            scatter_wait(nxt)
            scatter_wait(cur)
            gather_wait(nxt)
            tok_copy(0, cur).wait()

    for parity in range(2):
        pl.when((i < n_used) & (i % 2 == parity))(functools.partial(step, parity))


def _final_kernel(h_ref, *rest):
    y_refs, (gate_ref, g_ref, wg_ref, p_ref, wp_ref, o_ref) = rest[:TOP_K], rest[TOP_K:]
    gates = gate_ref[...]
    moe = sum(gates[:, k:k + 1] * y_ref[:, 0, :] for k, y_ref in enumerate(y_refs))
    h = h_ref[...] + moe
    n = (h * lax.rsqrt(jnp.mean(h * h, axis=-1, keepdims=True) + NORM_EPS) * g_ref[...]).astype(BF16)
    gate = jax.nn.sigmoid(_dot(n, wg_ref[...]))
    o_ref[...] = h + gate * _dot(p_ref[...].astype(BF16), wp_ref[...])


def _t5_bucket(rel):
    n = jnp.maximum(rel, 0)
    max_exact = NUM_BUCKETS // 2
    nf = jnp.maximum(n, 1).astype(F32)
    large = max_exact + (jnp.log(nf / max_exact) / math.log(MAX_DISTANCE / max_exact)
                         * (NUM_BUCKETS - max_exact)).astype(jnp.int32)
    large = jnp.minimum(large, NUM_BUCKETS - 1)
    return jnp.where(n < max_exact, n, large)


def _full(shape):
    return pl.BlockSpec(shape, lambda *_: (0,) * len(shape))


def _layer(i, h_in, p_i, rel_bias, prm):
    bsz, seq, d = h_in.shape
    n_tok = bsz * seq
    ns = seq // TM
    nk = seq // TK
    x2 = h_in.reshape(n_tok, d)

    w_in = prm['w_in']
    fox_w = N_FOX_HEADS * HEAD_DIM
    offs = [0, fox_w, 2 * fox_w, 3 * fox_w, 3 * fox_w + N_FOX_HEADS]
    offs += [offs[4] + GROUP, offs[4] + 2 * GROUP, offs[4] + 3 * GROUP]
    col = lambda a, b: w_in[:, a:b].astype(BF16)
    wfqT, wfk, wfvT = col(offs[0], offs[1]).T, col(offs[1], offs[2]), col(offs[2], offs[3]).T
    wdqT, wdk, wdvT = col(offs[4], offs[5]).T, col(offs[5], offs[6]), col(offs[6], offs[7]).T
    wf = jnp.zeros((d, LANES), BF16).at[:, :N_FOX_HEADS].set(col(offs[3], offs[4]))
    bf = jnp.zeros((1, LANES), F32).at[0, :N_FOX_HEADS].set(prm['b_f'])
    scale = HEAD_DIM ** -0.5 * LOG2E
    rep = GROUP // HEAD_DIM
    heads = jnp.arange(N_FOX_HEADS)
    place = jnp.stack([jnp.zeros((LANES, GROUP), BF16)
                       .at[heads, (heads // 2) * LANES + (heads % 2) * N_SPLIT + t].set(1.0)
                       for t in range(N_SPLIT)])
    gfqT = jnp.broadcast_to((jnp.tile(prm['g_fox_q'], rep) * scale)[:, None], (GROUP, TM))
    gdqT = jnp.broadcast_to((jnp.tile(prm['g_diff_q'], rep) * scale)[:, None], (GROUP, TM))
    gfk = jnp.tile(prm['g_fox_k'], rep)[None, :]
    gdk = jnp.tile(prm['g_diff_k'], rep)[None, :]
    gi = jnp.arange(GROUP) // HEAD_DIM
    seg = jnp.where(gi[:, None] == gi[None, :], 1.0 / HEAD_DIM, 0.0).astype(BF16)
    ti = jnp.arange(TM)
    tri_incl = (ti[None, :] <= ti[:, None]).astype(BF16)
    tri_strict = (ti[None, :] < ti[:, None]).astype(BF16)

    w_spec_t = _full((GROUP, d))
    w_spec = _full((d, GROUP))
    qT_shape = jax.ShapeDtypeStruct((bsz, GROUP, seq), BF16)
    k_shape = jax.ShapeDtypeStruct((n_tok, GROUP), BF16)
    vT_shape = jax.ShapeDtypeStruct((bsz, GROUP // LANES, nk, LANES, TK), BF16)
    qT_spec = pl.BlockSpec((None, GROUP, TM), lambda b, s: (b, 0, s))
    k_spec = pl.BlockSpec((TM, GROUP), lambda b, s: (b * ns + s, 0))
    vT_spec = pl.BlockSpec((None, GROUP // LANES, TM // TK, LANES, TK), lambda b, s: (b, 0, s, 0, 0))
    fk_shape = jax.ShapeDtypeStruct((n_tok, 2 * GROUP), BF16)
    fk_spec = pl.BlockSpec((TM, 2 * GROUP), lambda b, s: (b * ns + s, 0))
    fqT, fk, fvT, dqT, dk, dvT = pl.pallas_call(
        _inproj_kernel,
        grid=(bsz, ns),
        in_specs=[pl.BlockSpec((TM, d), lambda b, s: (b * ns + s, 0)), _full((1, d)),
                  w_spec_t, w_spec, w_spec_t, w_spec_t, w_spec, w_spec_t, _full((d, LANES)), _full((1, LANES)),
                  _full((GROUP, TM)), _full((1, GROUP)), _full((GROUP, TM)), _full((1, GROUP)),
                  _full((GROUP, GROUP)), _full((TM, TM)), _full((N_SPLIT, LANES, GROUP))],
        out_specs=[qT_spec, fk_spec, vT_spec, qT_spec, k_spec, vT_spec],
        out_shape=[qT_shape, fk_shape, vT_shape, qT_shape, k_shape, vT_shape],
        scratch_shapes=[pltpu.VMEM((1, LANES), F32)],
        compiler_params=_params(2),
        name="inproj",
    )(x2, prm['g_attn'][None, :], wfqT, wfk, wfvT, wdqT, wdk, wdvT, wf, bf, gfqT, gfk, gdqT, gdk, seg, tri_incl,
      place)

    n_hb = GROUP // LANES
    att_q = pl.BlockSpec((None, LANES, seq), lambda b, h: (b, h, 0))
    att_k = pl.BlockSpec((seq, LANES), lambda b, h: (b, h))
    att_v = pl.BlockSpec((None, None, nk, LANES, TK), lambda b, h: (b, h, 0, 0, 0))
    att_o = pl.BlockSpec((seq, LANES), lambda b, h: (b, h))
    fox_out = pl.pallas_call(
        _fox_kernel,
        grid=(bsz, n_hb),
        in_specs=[att_q, pl.BlockSpec((seq, 2 * LANES), lambda b, h: (b, h)), att_v],
        out_specs=att_o,
        out_shape=jax.ShapeDtypeStruct((n_tok, GROUP), BF16),
        compiler_params=_params(2),
        name="fox_attention",
    )(fqT, fk, fvT)

    kpos = jnp.arange(TK)[:, None]
    qpos = jnp.arange(TQ)[None, :]
    rel = jnp.stack([qpos - kpos, qpos - kpos + TK])
    assert TK + 1 >= MAX_DISTANCE and TQ == TK
    table = (rel_bias.astype(F32) - rel_bias[NUM_BUCKETS - 1].astype(F32)[None, :]) * LOG2E
    onehot = _t5_bucket(rel)[None, ..., None] == jnp.arange(NUM_BUCKETS, dtype=jnp.int32)
    biasT = jnp.sum(jnp.where(onehot, table.T[:, None, None, None, :], 0.0), axis=-1)
    biasT = jnp.where((rel >= 0)[None], biasT, NEG_INF)
    lam_init = 0.8 - 0.6 * math.exp(-0.3 * i)
    lam_p = jnp.stack([prm['lambda_q1'], prm['lambda_k1'], prm['lambda_q2'], prm['lambda_k2']]).astype(F32)
    gsubT = jnp.broadcast_to((prm['g_subln'] * (1.0 - lam_init))[:, None], (LANES, TQ))
    diff_out = pl.pallas_call(
        functools.partial(_diff_kernel, lam_init),
        grid=(bsz, N_DIFF_HEADS),
        in_specs=[att_q, att_k, att_v, pl.BlockSpec((None, 2, TK, TQ), lambda b, h: (h, 0, 0, 0)),
                  _full((4, HEAD_DIM)), _full((LANES, TQ))],
        out_specs=att_o,
        out_shape=jax.ShapeDtypeStruct((n_tok, GROUP), BF16),
        compiler_params=_params(2),
        name="diff_attention",
    )(dqT, dk, dvT, biasT, lam_p, gsubT)

    w_out = prm['w_out'].astype(BF16)
    wr = jnp.zeros((d, LANES), F32).at[:, :N_EXPERTS].set(prm['w_router'])
    wr_hi = wr.astype(BF16)
    wr_lo = (wr - wr_hi.astype(F32)).astype(BF16)
    br = jnp.full((1, LANES), NEG_INF, F32).at[0, :N_EXPERTS].set(prm['b_router'])
    row_spec = lambda w: pl.BlockSpec((TM, w), lambda t: (t, 0))
    h1, xm, topi, gates, rank, counts = pl.pallas_call(
        _mix_kernel,
        grid=(n_tok // TM,),
        in_specs=[row_spec(d), row_spec(GROUP), row_spec(GROUP), _full((GROUP, d)), _full((GROUP, d)),
                  _full((1, d)), _full((d, LANES)), _full((d, LANES)), _full((1, LANES)), _full((TM, TM))],
        out_specs=[row_spec(d), pl.BlockSpec((TM, 1, d), lambda t: (t, 0, 0)), row_spec(LANES),
                   row_spec(LANES), row_spec(LANES), _full((1, LANES))],
        out_shape=[jax.ShapeDtypeStruct((n_tok, d), F32), jax.ShapeDtypeStruct((n_tok, 1, d), F32),
                   jax.ShapeDtypeStruct((n_tok, LANES), jnp.int32), jax.ShapeDtypeStruct((n_tok, LANES), F32),
                   jax.ShapeDtypeStruct((n_tok, LANES), jnp.int32), jax.ShapeDtypeStruct((1, LANES), F32)],
        scratch_shapes=[pltpu.VMEM((1, LANES), F32)],
        compiler_params=_params(1),
        name="mix_router",
    )(x2, fox_out, diff_out, w_out[:GROUP], w_out[GROUP:], prm['g_mlp'][None, :], wr_hi, wr_lo, br, tri_strict)

    m = n_tok * TOP_K
    n_blocks = (m + N_EXPERTS * MOE_BLOCK + MOE_BLOCK - 1) // MOE_BLOCK
    p_rows = n_blocks * MOE_BLOCK
    cnt = counts[0, :N_EXPERTS].astype(jnp.int32)
    padded = ((cnt + MOE_BLOCK - 1) // MOE_BLOCK) * MOE_BLOCK
    pad_ends = jnp.cumsum(padded)
    pad_starts = pad_ends - padded
    block_start = jnp.arange(n_blocks, dtype=jnp.int32) * MOE_BLOCK
    block_expert = jnp.minimum(jnp.sum(pad_ends[None, :] <= block_start[:, None], axis=1),
                               N_EXPERTS - 1).astype(jnp.int32)
    n_used = (pad_ends[-1] // MOE_BLOCK).astype(jnp.int32)[None]
    flat = jnp.sort((topi[:, :TOP_K] * m + jnp.arange(m, dtype=jnp.int32).reshape(n_tok, TOP_K)).reshape(m)) % m
    grouped_rows = jnp.concatenate([(flat % TOP_K) * n_tok + flat // TOP_K, jnp.zeros((MOE_BLOCK,), jnp.int32)])
    onehot_e = block_expert[:, None] == jnp.arange(N_EXPERTS, dtype=jnp.int32)[None, :]
    pick = lambda v: jnp.sum(jnp.where(onehot_e, v[None, :], 0), axis=1)
    in_group = block_start - pick(pad_starts)
    n_valid = jnp.clip(pick(cnt) - in_group, 0, MOE_BLOCK)
    first = jnp.clip(pick(jnp.cumsum(cnt) - cnt) + in_group, 0, m)
    rows = jax.vmap(lambda o: lax.dynamic_slice(grouped_rows, (o,), (MOE_BLOCK,)))(first)
    lane_r = jnp.arange(MOE_BLOCK, dtype=jnp.int32)[None, :]
    asg = jnp.where(lane_r < n_valid[:, None], rows, m + lane_r)

    d_ff = prm['w2'].shape[1]
    n_grp = 2 * d_ff // (2 * LANES)
    b1p = prm['b1'].reshape(N_EXPERTS, n_grp, LANES, 2).transpose(0, 1, 3, 2).reshape(N_EXPERTS, 1, 2 * d_ff)
    b2 = prm['b2'].reshape(N_EXPERTS, 1, d)
    pj = jnp.arange(LANES)
    perm = (jnp.zeros((2 * LANES, 2 * LANES), BF16).at[2 * pj, pj].set(1.0)
            .at[2 * pj + 1, LANES + pj].set(1.0))
    any_spec = pl.BlockSpec(memory_space=pl.ANY)
    ys = pl.pallas_call(
        _expert_kernel,
        grid_spec=pltpu.PrefetchScalarGridSpec(
            num_scalar_prefetch=2,
            grid=(n_blocks,),
            in_specs=[any_spec, any_spec, any_spec,
                      pl.BlockSpec((None, d, 2 * d_ff), lambda t, be, nu: (be[t], 0, 0)),
                      pl.BlockSpec((None, 1, 2 * d_ff), lambda t, be, nu: (be[t], 0, 0)),
                      pl.BlockSpec((None, d_ff, d), lambda t, be, nu: (be[t], 0, 0)),
                      pl.BlockSpec((None, 1, d), lambda t, be, nu: (be[t], 0, 0)),
                      pl.BlockSpec((2 * LANES, 2 * LANES), lambda t, be, nu: (0, 0))],
            out_specs=any_spec,
            scratch_shapes=[pltpu.SMEM((2, MOE_BLOCK), jnp.int32), pltpu.SMEM((2, MOE_BLOCK), jnp.int32)]
            + [pltpu.VMEM((MOE_BLOCK, d), F32)] * 4
            + [pltpu.VMEM((d, 2 * d_ff), BF16), pltpu.VMEM((d_ff, d), BF16)]
            + [pltpu.SemaphoreType.DMA((2, 2)), pltpu.SemaphoreType.DMA((2,)), pltpu.SemaphoreType.DMA((2,))],
        ),
        out_shape=jax.ShapeDtypeStruct((m + MOE_BLOCK, 1, d), F32),
        compiler_params=pltpu.CompilerParams(dimension_semantics=("arbitrary",),
                                             vmem_limit_bytes=EXPERT_VMEM_LIMIT),
        name="experts",
    )(block_expert, n_used, asg % n_tok, asg, xm, prm['w1'], b1p, prm['w2'], b2, perm)

    ple = p_i.shape[-1]
    out = pl.pallas_call(
        _final_kernel,
        grid=(n_tok // TM,),
        in_specs=[row_spec(d)]
        + [pl.BlockSpec((TM, 1, d), functools.partial(lambda k, t: (k * (n_tok // TM) + t, 0, 0), k))
           for k in range(TOP_K)]
        + [row_spec(LANES),
                  _full((1, d)), _full((d, d)), row_spec(ple), _full((ple, d))],
        out_specs=row_spec(d),
        out_shape=jax.ShapeDtypeStruct((n_tok, d), F32),
        compiler_params=_params(1),
        name="combine_ple",
    )(h1, ys, ys, ys, ys, gates, prm['g_ple'][None, :],
      prm['w_ple_gate'].astype(BF16), p_i.reshape(n_tok, ple), prm['w_ple_proj'].astype(BF16))
    return out.reshape(bsz, seq, d)


def kernel(x, p, rel_bias, g_attn, w_in, b_f, g_fox_q, g_fox_k, g_diff_q, g_diff_k, lambda_q1, lambda_k1,
           lambda_q2, lambda_k2, g_subln, w_out, g_mlp, w_router, b_router, w1, b1, w2, b2, g_ple,
           w_ple_gate, w_ple_proj):
    stacked = dict(g_attn=g_attn, w_in=w_in, b_f=b_f, g_fox_q=g_fox_q, g_fox_k=g_fox_k, g_diff_q=g_diff_q,
                   g_diff_k=g_diff_k, lambda_q1=lambda_q1, lambda_k1=lambda_k1, lambda_q2=lambda_q2,
                   lambda_k2=lambda_k2, g_subln=g_subln, w_out=w_out, g_mlp=g_mlp, w_router=w_router,
                   b_router=b_router, w1=w1, b1=b1, w2=w2, b2=b2, g_ple=g_ple, w_ple_gate=w_ple_gate,
                   w_ple_proj=w_ple_proj)
    h = x
    for i in range(p.shape[0]):
        h = _layer(i, h, p[i], rel_bias, {name: v[i] for name, v in stacked.items()})
    return h
```

```python
import functools
import math

import jax
import jax.numpy as jnp
from jax import lax
from jax.experimental import pallas as pl
from jax.experimental.pallas import tpu as pltpu

F32 = jnp.float32
BF16 = jnp.bfloat16

HEAD_DIM = 64
N_FOX_HEADS = 8
N_DIFF_HEADS = 4
GROUP = 512
LANES = 128
NUM_BUCKETS = 32
MAX_DISTANCE = 128
N_EXPERTS = 32
TOP_K = 4
SWIGLU_ALPHA = 1.702
SWIGLU_LIMIT = 7.0
MOE_BLOCK = 256
IDX_ALIGN = 128
NORM_EPS = 1e-6
NEG_INF = -1e30
LOG2E = math.log2(math.e)
N_SPLIT = 3

TM = 512
TQ = 256
TK = 256
LOOKAHEAD = 2
VMEM_LIMIT = 48 * 1024 * 1024
EXPERT_VMEM_LIMIT = 58 * 1024 * 1024


def _params(n_axes):
    return pltpu.CompilerParams(dimension_semantics=("arbitrary",) * n_axes,
                                vmem_limit_bytes=VMEM_LIMIT)


def _dot(a, b):
    return jnp.dot(a, b, preferred_element_type=F32)


def _dot_nt(a, b):
    return lax.dot_general(a, b, (((1,), (1,)), ((), ())), preferred_element_type=F32)


def _split3(v):
    hi = v.astype(BF16)
    r = v - hi.astype(F32)
    mid = r.astype(BF16)
    lo = (r - mid.astype(F32)).astype(BF16)
    return hi, mid, lo


def _inproj_kernel(x_ref, g_ref, wfq_ref, wfk_ref, wfv_ref, wdq_ref, wdk_ref, wdv_ref, wf_ref, bf_ref,
                   gfq_ref, gfk_ref, gdq_ref, gdk_ref, seg_ref, tri_ref, place_ref,
                   fq_ref, fk_ref, fv_ref, dq_ref, dk_ref, dv_ref, carry_ref):
    @pl.when(pl.program_id(1) == 0)
    def _():
        carry_ref[...] = jnp.zeros_like(carry_ref)

    xf = x_ref[...]
    a = (xf * lax.rsqrt(jnp.mean(xf * xf, axis=-1, keepdims=True) + NORM_EPS) * g_ref[...]).astype(BF16)
    seg = seg_ref[...]

    def q_group(w_ref, gain_ref, out_ref):
        acc = _dot_nt(w_ref[...], a)
        ms = _dot(seg, (acc * acc).astype(BF16))
        out_ref[...] = (acc * lax.rsqrt(ms + NORM_EPS) * gain_ref[...]).astype(BF16)

    def k_group(w_ref, gain_ref):
        acc = _dot(a, w_ref[...])
        ms = _dot((acc * acc).astype(BF16), seg)
        return (acc * lax.rsqrt(ms + NORM_EPS) * gain_ref[...]).astype(BF16)

    def v_group(w_ref, out_ref):
        acc = _dot_nt(w_ref[...], a).astype(BF16)
        for hb in range(GROUP // LANES):
            for j in range(TM // TK):
                out_ref[hb, j] = acc[hb * LANES:(hb + 1) * LANES, j * TK:(j + 1) * TK]

    q_group(wfq_ref, gfq_ref, fq_ref)
    v_group(wfv_ref, fv_ref)
    q_group(wdq_ref, gdq_ref, dq_ref)
    dk_ref[...] = k_group(wdk_ref, gdk_ref)
    v_group(wdv_ref, dv_ref)

    fl = _dot(a, wf_ref[...]) + bf_ref[...]
    logf = jnp.minimum(fl, 0.0) - jnp.log1p(jnp.exp(-jnp.abs(fl)))
    tri = tri_ref[...]
    cs = carry_ref[...]
    for part in _split3(logf):
        cs = cs + _dot(tri, part)
    carry_ref[...] = cs[TM - 1:TM, :]

    fk = k_group(wfk_ref, gfk_ref)
    extra = jnp.zeros((TM, GROUP), F32)
    for t, part in enumerate(_split3(cs * LOG2E)):
        extra = extra + _dot(part, place_ref[t])
    extra = extra.astype(BF16)
    for hb in range(GROUP // LANES):
        fk_ref[:, 2 * hb * LANES:(2 * hb + 1) * LANES] = fk[:, hb * LANES:(hb + 1) * LANES]
        fk_ref[:, (2 * hb + 1) * LANES:(2 * hb + 2) * LANES] = extra[:, hb * LANES:(hb + 1) * LANES]


def _two_map_queries(qT_blk):
    row = lax.broadcasted_iota(jnp.int32, qT_blk.shape, 0)
    zero = jnp.zeros_like(qT_blk)
    return jnp.concatenate([jnp.where(row < HEAD_DIM, qT_blk, zero),
                            jnp.where(row >= HEAD_DIM, qT_blk, zero)], axis=1)


def _softmax_step(sT, vT_blk, carry):
    m, l, acc = carry
    m_new = jnp.maximum(m, jnp.max(sT, axis=0, keepdims=True))
    alpha = jnp.exp2(m - m_new)
    pT = jnp.exp2(sT - m_new)
    l = alpha * l + jnp.sum(pT, axis=0, keepdims=True)
    acc = alpha * acc + _dot(vT_blk, pT.astype(BF16))
    return m_new, l, acc


def _causal_keep():
    key = lax.broadcasted_iota(jnp.int32, (TK, 2 * TQ), 0)
    col = lax.broadcasted_iota(jnp.int32, (TK, 2 * TQ), 1)
    qry = jnp.where(col >= TQ, col - TQ, col)
    return key <= qry


def _init_carry():
    return (jnp.full((1, 2 * TQ), NEG_INF, F32), jnp.zeros((1, 2 * TQ), F32),
            jnp.zeros((LANES, 2 * TQ), F32))


def _causal_sweep(seq, k_ref, vT_ref, queries, adjust, finish):
    pairs = [(qi, kj) for qi in range(seq // TQ) for kj in range(qi + 1)]
    q_ops = {}

    def score(qi, kj):
        if qi not in q_ops:
            q_ops[qi] = queries(qi)
        return _dot(k_ref[kj * TK:(kj + 1) * TK, :], q_ops[qi])

    pending = [score(*pr) for pr in pairs[:LOOKAHEAD]]
    carry = None
    for n, (qi, kj) in enumerate(pairs):
        if n + LOOKAHEAD < len(pairs):
            pending.append(score(*pairs[n + LOOKAHEAD]))
        if kj == 0:
            carry = _init_carry()
        carry = _softmax_step(adjust(pending.pop(0), qi, kj), vT_ref[kj], carry)
        if kj == qi:
            m, l, acc = carry
            finish(qi, acc * (1.0 / l))


def _fox_kernel(qT_ref, k_ref, vT_ref, o_ref):
    seq = k_ref.shape[0]
    keep = _causal_keep()
    row = lax.broadcasted_iota(jnp.int32, (LANES, TQ), 0)
    arow = lax.broadcasted_iota(jnp.int32, (LANES, 2 * TQ), 0)
    acol = lax.broadcasted_iota(jnp.int32, (LANES, 2 * TQ), 1)
    first = (acol < TQ) & (arow < N_SPLIT)
    second = (acol >= TQ) & (arow >= N_SPLIT) & (arow < 2 * N_SPLIT)
    minus_c = jnp.where(first | second, -1.0, 0.0).astype(BF16)

    def queries(qi):
        return jnp.concatenate([_two_map_queries(qT_ref[:, qi * TQ:(qi + 1) * TQ]), minus_c], axis=0)

    def adjust(sT, qi, kj):
        return jnp.where(keep, sT, NEG_INF) if kj == qi else sT

    def finish(qi, oT):
        o = jnp.where(row < HEAD_DIM, oT[:, :TQ], oT[:, TQ:])
        o_ref[qi * TQ:(qi + 1) * TQ, :] = o.T.astype(BF16)

    _causal_sweep(seq, k_ref, vT_ref, queries, adjust, finish)


def _diff_kernel(lam_init, qT_ref, k_ref, vT_ref, bias_ref, lam_ref, gsub_ref, o_ref):
    seq = k_ref.shape[0]
    lp = lam_ref[...]
    lam = (jnp.exp(jnp.sum(lp[0:1] * lp[1:2], axis=1, keepdims=True))
           - jnp.exp(jnp.sum(lp[2:3] * lp[3:4], axis=1, keepdims=True)) + lam_init)

    def queries(qi):
        return _two_map_queries(qT_ref[:, qi * TQ:(qi + 1) * TQ])

    def adjust(sT, qi, kj):
        if kj >= qi - 1:
            b = bias_ref[qi - kj]
            sT = sT + jnp.concatenate([b, b], axis=1)
        return sT

    def finish(qi, oT):
        o = oT[:, :TQ] - lam * oT[:, TQ:]
        y = o * lax.rsqrt(jnp.mean(o * o, axis=0, keepdims=True) + NORM_EPS) * gsub_ref[...]
        o_ref[qi * TQ:(qi + 1) * TQ, :] = y.T.astype(BF16)

    _causal_sweep(seq, k_ref, vT_ref, queries, adjust, finish)


def _mix_kernel(x_ref, fox_ref, dif_ref, wa_ref, wb_ref, g_ref, wr_hi_ref, wr_lo_ref, br_ref,
                h_ref, xm_ref, topi_ref, gate_ref, cnt_ref, carry_ref):
    @pl.when(pl.program_id(0) == 0)
    def _():
        carry_ref[...] = jnp.zeros_like(carry_ref)

    h = x_ref[...] + _dot(fox_ref[...], wa_ref[...]) + _dot(dif_ref[...], wb_ref[...])
    h_ref[...] = h
    xm = h * lax.rsqrt(jnp.mean(h * h, axis=-1, keepdims=True) + NORM_EPS) * g_ref[...]
    xm_ref[:, 0, :] = xm

    x_hi = xm.astype(BF16)
    x_lo = (xm - x_hi.astype(F32)).astype(BF16)
    logits = (_dot(x_hi, wr_hi_ref[...]) + _dot(x_hi, wr_lo_ref[...]) + _dot(x_lo, wr_hi_ref[...])
              + br_ref[...])

    lane = lax.broadcasted_iota(jnp.int32, logits.shape, 1)
    vals, sels, idxs = [], [], []
    for _ in range(TOP_K):
        mx = jnp.max(logits, axis=1, keepdims=True)
        idx = jnp.min(jnp.where(logits == mx, lane, LANES), axis=1, keepdims=True)
        sel = lane == idx
        logits = jnp.where(sel, -jnp.inf, logits)
        vals.append(mx)
        sels.append(sel)
        idxs.append(idx)
    exps = [jnp.exp(v - vals[0]) for v in vals]
    denom = exps[0] + exps[1] + exps[2] + exps[3]

    multi_hot = sum(s.astype(F32) for s in sels)
    topi = jnp.zeros(logits.shape, jnp.int32)
    gate = jnp.zeros(logits.shape, F32)
    for k in range(TOP_K):
        topi = jnp.where(lane == k, idxs[k], topi)
        gate = jnp.where(lane == k, exps[k] / denom, gate)
    topi_ref[...] = topi
    gate_ref[...] = gate
    carry_ref[...] = carry_ref[...] + jnp.sum(multi_hot, axis=0, keepdims=True)
    cnt_ref[...] = carry_ref[...]


def _expert_kernel(be_ref, nu_ref, off_ref, nv_ref, tok_hbm, asg_hbm, x_hbm, w1f_ref, b1_ref, w2f_ref, b2_ref,
                   perm_ref, y_hbm, tok_ref, asg_ref, x0, x1, y0, y1, w1_ref, w2_ref, idx_sem, g_sem, s_sem):
    i = pl.program_id(0)
    n_used = nu_ref[0]
    n_blocks = be_ref.shape[0]
    spare_row = y_hbm.shape[0] - MOE_BLOCK
    xbufs, ybufs = (x0, x1), (y0, y1)

    def tok_copy(blk, slot):
        start = pl.multiple_of(off_ref[blk], IDX_ALIGN)
        return pltpu.make_async_copy(tok_hbm.at[pl.ds(start, MOE_BLOCK)], tok_ref.at[slot], idx_sem.at[0, slot])

    def asg_copy(blk, slot):
        start = pl.multiple_of(off_ref[blk], IDX_ALIGN)
        return pltpu.make_async_copy(asg_hbm.at[pl.ds(start, MOE_BLOCK)], asg_ref.at[slot], idx_sem.at[1, slot])

    def asg_ready(blk, slot):
        asg_copy(0, slot).wait()

        def fix(r, carry):
            asg_ref[slot, r] = spare_row + r
            return carry
        lax.fori_loop(nv_ref[blk], MOE_BLOCK, fix, 0)

    def row_loop(fn):
        for r in range(MOE_BLOCK):
            fn(r)

    def gather_issue(slot):
        row_loop(lambda r: pltpu.make_async_copy(x_hbm.at[tok_ref[slot, r]], xbufs[slot].at[pl.ds(r, 1)],
                                                 g_sem.at[slot]).start())

    def gather_wait(slot):
        pltpu.make_async_copy(xbufs[slot], xbufs[slot], g_sem.at[slot]).wait()

    def scatter_issue(slot):
        row_loop(lambda r: pltpu.make_async_copy(ybufs[slot].at[pl.ds(r, 1)], y_hbm.at[asg_ref[slot, r]],
                                                 s_sem.at[slot]).start())

    def scatter_wait(slot):
        pltpu.make_async_copy(ybufs[slot], ybufs[slot], s_sem.at[slot]).wait()

    @pl.when(i == 0)
    def _():
        y1[...] = jnp.zeros(y1.shape, y1.dtype)
        for r in range(MOE_BLOCK):
            asg_ref[1, r] = spare_row + r
        tok_copy(0, 0).start()
        tok_copy(0, 0).wait()
        tok_copy(jnp.minimum(1, n_blocks - 1), 1).start()
        gather_issue(0)

    @pl.when((i < n_used) & ((i == 0) | (be_ref[i] != be_ref[jnp.maximum(i - 1, 0)])))
    def _():
        for g in range(w1f_ref.shape[1] // (2 * LANES)):
            cols = slice(2 * g * LANES, 2 * (g + 1) * LANES)
            w1_ref[:, cols] = _dot(w1f_ref[:, cols].astype(BF16), perm_ref[...]).astype(BF16)
        w2_ref[...] = w2f_ref[...].astype(BF16)

    def step(cur):
        nxt = 1 - cur
        asg_copy(i, cur).start()
        tok_copy(0, nxt).wait()
        gather_issue(nxt)
        tok_copy(jnp.minimum(i + 2, n_blocks - 1), cur).start()

        @pl.when(i >= 1)
        def _():
            asg_ready(i - 1, nxt)
            scatter_wait(cur)

        scatter_issue(nxt)
        gather_wait(cur)
        xbuf, ybuf = xbufs[cur], ybufs[cur]
        xs = xbuf[...].astype(BF16)
        h = _dot(xs, w1_ref[...]) + b1_ref[...]
        acts = []
        for c in range(h.shape[1] // (2 * LANES)):
            glu = jnp.minimum(h[:, 2 * c * LANES:(2 * c + 1) * LANES], SWIGLU_LIMIT)
            lin = jnp.clip(h[:, (2 * c + 1) * LANES:(2 * c + 2) * LANES], -SWIGLU_LIMIT, SWIGLU_LIMIT)
            acts.append((glu * jax.nn.sigmoid(SWIGLU_ALPHA * glu) * (lin + 1.0)).astype(BF16))
        out = _dot(jnp.concatenate(acts, axis=1), w2_ref[...]) + b2_ref[...]
        ybuf[...] = out

        @pl.when(i == n_used - 1)
        def _():
            asg_ready(i, cur)
            scatter_issue(cur)
            scatter_wait(nxt)
            scatter_wait(cur)
            gather_wait(nxt)
            tok_copy(0, cur).wait()

    for parity in range(2):
        pl.when((i < n_used) & (i % 2 == parity))(functools.partial(step, parity))


def _final_kernel(h_ref, *rest):
    y_refs, (gate_ref, g_ref, wg_ref, p_ref, wp_ref, o_ref) = rest[:TOP_K], rest[TOP_K:]
    gates = gate_ref[...]
    moe = sum(gates[:, k:k + 1] * y_ref[:, 0, :] for k, y_ref in enumerate(y_refs))
    h = h_ref[...] + moe
    n = (h * lax.rsqrt(jnp.mean(h * h, axis=-1, keepdims=True) + NORM_EPS) * g_ref[...]).astype(BF16)
    gate = jax.nn.sigmoid(_dot(n, wg_ref[...]))
    o_ref[...] = h + gate * _dot(p_ref[...].astype(BF16), wp_ref[...])


def _t5_bucket(rel):
    n = jnp.maximum(rel, 0)
    max_exact = NUM_BUCKETS // 2
    nf = jnp.maximum(n, 1).astype(F32)
    large = max_exact + (jnp.log(nf / max_exact) / math.log(MAX_DISTANCE / max_exact)
                         * (NUM_BUCKETS - max_exact)).astype(jnp.int32)
    large = jnp.minimum(large, NUM_BUCKETS - 1)
    return jnp.where(n < max_exact, n, large)


def _full(shape):
    return pl.BlockSpec(shape, lambda *_: (0,) * len(shape))


def _layer(i, h_in, p_i, rel_bias, prm):
    bsz, seq, d = h_in.shape
    n_tok = bsz * seq
    ns = seq // TM
    nk = seq // TK
    x2 = h_in.reshape(n_tok, d)

    w_in = prm['w_in']
    fox_w = N_FOX_HEADS * HEAD_DIM
    offs = [0, fox_w, 2 * fox_w, 3 * fox_w, 3 * fox_w + N_FOX_HEADS]
    offs += [offs[4] + GROUP, offs[4] + 2 * GROUP, offs[4] + 3 * GROUP]
    col = lambda a, b: w_in[:, a:b].astype(BF16)
    wfqT, wfk, wfvT = col(offs[0], offs[1]).T, col(offs[1], offs[2]), col(offs[2], offs[3]).T
    wdqT, wdk, wdvT = col(offs[4], offs[5]).T, col(offs[5], offs[6]), col(offs[6], offs[7]).T
    wf = jnp.zeros((d, LANES), BF16).at[:, :N_FOX_HEADS].set(col(offs[3], offs[4]))
    bf = jnp.zeros((1, LANES), F32).at[0, :N_FOX_HEADS].set(prm['b_f'])
    scale = HEAD_DIM ** -0.5 * LOG2E
    rep = GROUP // HEAD_DIM
    heads = jnp.arange(N_FOX_HEADS)
    place = jnp.stack([jnp.zeros((LANES, GROUP), BF16)
                       .at[heads, (heads // 2) * LANES + (heads % 2) * N_SPLIT + t].set(1.0)
                       for t in range(N_SPLIT)])
    gfqT = jnp.broadcast_to((jnp.tile(prm['g_fox_q'], rep) * scale)[:, None], (GROUP, TM))
    gdqT = jnp.broadcast_to((jnp.tile(prm['g_diff_q'], rep) * scale)[:, None], (GROUP, TM))
    gfk = jnp.tile(prm['g_fox_k'], rep)[None, :]
    gdk = jnp.tile(prm['g_diff_k'], rep)[None, :]
    gi = jnp.arange(GROUP) // HEAD_DIM
    seg = jnp.where(gi[:, None] == gi[None, :], 1.0 / HEAD_DIM, 0.0).astype(BF16)
    ti = jnp.arange(TM)
    tri_incl = (ti[None, :] <= ti[:, None]).astype(BF16)

    w_spec_t = _full((GROUP, d))
    w_spec = _full((d, GROUP))
    qT_shape = jax.ShapeDtypeStruct((bsz, GROUP, seq), BF16)
    k_shape = jax.ShapeDtypeStruct((n_tok, GROUP), BF16)
    vT_shape = jax.ShapeDtypeStruct((bsz, GROUP // LANES, nk, LANES, TK), BF16)
    qT_spec = pl.BlockSpec((None, GROUP, TM), lambda b, s: (b, 0, s))
    k_spec = pl.BlockSpec((TM, GROUP), lambda b, s: (b * ns + s, 0))
    vT_spec = pl.BlockSpec((None, GROUP // LANES, TM // TK, LANES, TK), lambda b, s: (b, 0, s, 0, 0))
    fk_shape = jax.ShapeDtypeStruct((n_tok, 2 * GROUP), BF16)
    fk_spec = pl.BlockSpec((TM, 2 * GROUP), lambda b, s: (b * ns + s, 0))
    fqT, fk, fvT, dqT, dk, dvT = pl.pallas_call(
        _inproj_kernel,
        grid=(bsz, ns),
        in_specs=[pl.BlockSpec((TM, d), lambda b, s: (b * ns + s, 0)), _full((1, d)),
                  w_spec_t, w_spec, w_spec_t, w_spec_t, w_spec, w_spec_t, _full((d, LANES)), _full((1, LANES)),
                  _full((GROUP, TM)), _full((1, GROUP)), _full((GROUP, TM)), _full((1, GROUP)),
                  _full((GROUP, GROUP)), _full((TM, TM)), _full((N_SPLIT, LANES, GROUP))],
        out_specs=[qT_spec, fk_spec, vT_spec, qT_spec, k_spec, vT_spec],
        out_shape=[qT_shape, fk_shape, vT_shape, qT_shape, k_shape, vT_shape],
        scratch_shapes=[pltpu.VMEM((1, LANES), F32)],
        compiler_params=_params(2),
        name="inproj",
    )(x2, prm['g_attn'][None, :], wfqT, wfk, wfvT, wdqT, wdk, wdvT, wf, bf, gfqT, gfk, gdqT, gdk, seg, tri_incl,
      place)

    n_hb = GROUP // LANES
    att_q = pl.BlockSpec((None, LANES, seq), lambda b, h: (b, h, 0))
    att_k = pl.BlockSpec((seq, LANES), lambda b, h: (b, h))
    att_v = pl.BlockSpec((None, None, nk, LANES, TK), lambda b, h: (b, h, 0, 0, 0))
    att_o = pl.BlockSpec((seq, LANES), lambda b, h: (b, h))
    fox_out = pl.pallas_call(
        _fox_kernel,
        grid=(bsz, n_hb),
        in_specs=[att_q, pl.BlockSpec((seq, 2 * LANES), lambda b, h: (b, h)), att_v],
        out_specs=att_o,
        out_shape=jax.ShapeDtypeStruct((n_tok, GROUP), BF16),
        compiler_params=_params(2),
        name="fox_attention",
    )(fqT, fk, fvT)

    kpos = jnp.arange(TK)[:, None]
    qpos = jnp.arange(TQ)[None, :]
    rel = jnp.stack([qpos - kpos, qpos - kpos + TK])
    assert TK + 1 >= MAX_DISTANCE and TQ == TK
    table = (rel_bias.astype(F32) - rel_bias[NUM_BUCKETS - 1].astype(F32)[None, :]) * LOG2E
    onehot = _t5_bucket(rel)[None, ..., None] == jnp.arange(NUM_BUCKETS, dtype=jnp.int32)
    biasT = jnp.sum(jnp.where(onehot, table.T[:, None, None, None, :], 0.0), axis=-1)
    biasT = jnp.where((rel >= 0)[None], biasT, NEG_INF)
    lam_init = 0.8 - 0.6 * math.exp(-0.3 * i)
    lam_p = jnp.stack([prm['lambda_q1'], prm['lambda_k1'], prm['lambda_q2'], prm['lambda_k2']]).astype(F32)
    gsubT = jnp.broadcast_to((prm['g_subln'] * (1.0 - lam_init))[:, None], (LANES, TQ))
    diff_out = pl.pallas_call(
        functools.partial(_diff_kernel, lam_init),
        grid=(bsz, N_DIFF_HEADS),
        in_specs=[att_q, att_k, att_v, pl.BlockSpec((None, 2, TK, TQ), lambda b, h: (h, 0, 0, 0)),
                  _full((4, HEAD_DIM)), _full((LANES, TQ))],
        out_specs=att_o,
        out_shape=jax.ShapeDtypeStruct((n_tok, GROUP), BF16),
        compiler_params=_params(2),
        name="diff_attention",
    )(dqT, dk, dvT, biasT, lam_p, gsubT)

    w_out = prm['w_out'].astype(BF16)
    wr = jnp.zeros((d, LANES), F32).at[:, :N_EXPERTS].set(prm['w_router'])
    wr_hi = wr.astype(BF16)
    wr_lo = (wr - wr_hi.astype(F32)).astype(BF16)
    br = jnp.full((1, LANES), NEG_INF, F32).at[0, :N_EXPERTS].set(prm['b_router'])
    row_spec = lambda w: pl.BlockSpec((TM, w), lambda t: (t, 0))
    h1, xm, topi, gates, counts = pl.pallas_call(
        _mix_kernel,
        grid=(n_tok // TM,),
        in_specs=[row_spec(d), row_spec(GROUP), row_spec(GROUP), _full((GROUP, d)), _full((GROUP, d)),
                  _full((1, d)), _full((d, LANES)), _full((d, LANES)), _full((1, LANES))],
        out_specs=[row_spec(d), pl.BlockSpec((TM, 1, d), lambda t: (t, 0, 0)), row_spec(LANES),
                   row_spec(LANES), _full((1, LANES))],
        out_shape=[jax.ShapeDtypeStruct((n_tok, d), F32), jax.ShapeDtypeStruct((n_tok, 1, d), F32),
                   jax.ShapeDtypeStruct((n_tok, LANES), jnp.int32), jax.ShapeDtypeStruct((n_tok, LANES), F32),
                   jax.ShapeDtypeStruct((1, LANES), F32)],
        scratch_shapes=[pltpu.VMEM((1, LANES), F32)],
        compiler_params=_params(1),
        name="mix_router",
    )(x2, fox_out, diff_out, w_out[:GROUP], w_out[GROUP:], prm['g_mlp'][None, :], wr_hi, wr_lo, br)

    m = n_tok * TOP_K
    n_blocks = (m + N_EXPERTS * MOE_BLOCK + MOE_BLOCK - 1) // MOE_BLOCK
    p_rows = n_blocks * MOE_BLOCK
    cnt = counts[0, :N_EXPERTS].astype(jnp.int32)
    padded = ((cnt + MOE_BLOCK - 1) // MOE_BLOCK) * MOE_BLOCK
    pad_ends = jnp.cumsum(padded)
    pad_starts = pad_ends - padded
    block_start = jnp.arange(n_blocks, dtype=jnp.int32) * MOE_BLOCK
    block_expert = jnp.minimum(jnp.sum(pad_ends[None, :] <= block_start[:, None], axis=1),
                               N_EXPERTS - 1).astype(jnp.int32)
    n_used = (pad_ends[-1] // MOE_BLOCK).astype(jnp.int32)[None]
    experts = jnp.arange(N_EXPERTS, dtype=jnp.int32)
    real = topi[:, :TOP_K] * (2 * m) + jnp.arange(m, dtype=jnp.int32).reshape(n_tok, TOP_K)
    fill = jnp.arange(IDX_ALIGN - 1, dtype=jnp.int32)[None, :]
    need = (-cnt) % IDX_ALIGN
    filler = jnp.where(fill < need[:, None], experts[:, None] * (2 * m) + m + fill,
                       N_EXPERTS * (2 * m) + experts[:, None] * IDX_ALIGN + fill)
    keys = jnp.sort(jnp.concatenate([real.reshape(m), filler.reshape(-1)]))
    flat = jnp.concatenate([(keys % (2 * m)) % m, jnp.zeros((MOE_BLOCK,), jnp.int32)])
    grouped_tok = flat // TOP_K
    grouped_row = (flat % TOP_K) * n_tok + grouped_tok
    onehot_e = block_expert[:, None] == experts[None, :]
    pick = lambda v: jnp.sum(jnp.where(onehot_e, v[None, :], 0), axis=1)
    in_group = block_start - pick(pad_starts)
    n_valid = jnp.clip(pick(cnt) - in_group, 0, MOE_BLOCK)
    group_start = jnp.cumsum(cnt + need) - (cnt + need)
    first = jnp.clip(pick(group_start) + in_group, 0, keys.shape[0]) // IDX_ALIGN * IDX_ALIGN

    d_ff = prm['w2'].shape[1]
    n_grp = 2 * d_ff // (2 * LANES)
    b1p = prm['b1'].reshape(N_EXPERTS, n_grp, LANES, 2).transpose(0, 1, 3, 2).reshape(N_EXPERTS, 1, 2 * d_ff)
    b2 = prm['b2'].reshape(N_EXPERTS, 1, d)
    pj = jnp.arange(LANES)
    perm = (jnp.zeros((2 * LANES, 2 * LANES), BF16).at[2 * pj, pj].set(1.0)
            .at[2 * pj + 1, LANES + pj].set(1.0))
    any_spec = pl.BlockSpec(memory_space=pl.ANY)
    ys = pl.pallas_call(
        _expert_kernel,
        grid_spec=pltpu.PrefetchScalarGridSpec(
            num_scalar_prefetch=4,
            grid=(n_blocks,),
            in_specs=[any_spec, any_spec, any_spec,
                      pl.BlockSpec((None, d, 2 * d_ff), lambda t, be, *_:(be[t], 0, 0)),
                      pl.BlockSpec((None, 1, 2 * d_ff), lambda t, be, *_:(be[t], 0, 0)),
                      pl.BlockSpec((None, d_ff, d), lambda t, be, *_:(be[t], 0, 0)),
                      pl.BlockSpec((None, 1, d), lambda t, be, *_:(be[t], 0, 0)),
                      pl.BlockSpec((2 * LANES, 2 * LANES), lambda t, be, *_:(0, 0))],
            out_specs=any_spec,
            scratch_shapes=[pltpu.SMEM((2, MOE_BLOCK), jnp.int32), pltpu.SMEM((2, MOE_BLOCK), jnp.int32)]
            + [pltpu.VMEM((MOE_BLOCK, d), F32)] * 4
            + [pltpu.VMEM((d, 2 * d_ff), BF16), pltpu.VMEM((d_ff, d), BF16)]
            + [pltpu.SemaphoreType.DMA((2, 2)), pltpu.SemaphoreType.DMA((2,)), pltpu.SemaphoreType.DMA((2,))],
        ),
        out_shape=jax.ShapeDtypeStruct((m + MOE_BLOCK, 1, d), F32),
        compiler_params=pltpu.CompilerParams(dimension_semantics=("arbitrary",),
                                             vmem_limit_bytes=EXPERT_VMEM_LIMIT),
        name="experts",
    )(block_expert, n_used, first, n_valid, grouped_tok, grouped_row, xm, prm['w1'], b1p, prm['w2'], b2, perm)

    ple = p_i.shape[-1]
    out = pl.pallas_call(
        _final_kernel,
        grid=(n_tok // TM,),
        in_specs=[row_spec(d)]
        + [pl.BlockSpec((TM, 1, d), functools.partial(lambda k, t: (k * (n_tok // TM) + t, 0, 0), k))
           for k in range(TOP_K)]
        + [row_spec(LANES),
                  _full((1, d)), _full((d, d)), row_spec(ple), _full((ple, d))],
        out_specs=row_spec(d),
        out_shape=jax.ShapeDtypeStruct((n_tok, d), F32),
        compiler_params=_params(1),
        name="combine_ple",
    )(h1, ys, ys, ys, ys, gates, prm['g_ple'][None, :],
      prm['w_ple_gate'].astype(BF16), p_i.reshape(n_tok, ple), prm['w_ple_proj'].astype(BF16))
    return out.reshape(bsz, seq, d)


def kernel(x, p, rel_bias, g_attn, w_in, b_f, g_fox_q, g_fox_k, g_diff_q, g_diff_k, lambda_q1, lambda_k1,
           lambda_q2, lambda_k2, g_subln, w_out, g_mlp, w_router, b_router, w1, b1, w2, b2, g_ple,
           w_ple_gate, w_ple_proj):
    stacked = dict(g_attn=g_attn, w_in=w_in, b_f=b_f, g_fox_q=g_fox_q, g_fox_k=g_fox_k, g_diff_q=g_diff_q,
                   g_diff_k=g_diff_k, lambda_q1=lambda_q1, lambda_k1=lambda_k1, lambda_q2=lambda_q2,
                   lambda_k2=lambda_k2, g_subln=g_subln, w_out=w_out, g_mlp=g_mlp, w_router=w_router,
                   b_router=b_router, w1=w1, b1=b1, w2=w2, b2=b2, g_ple=g_ple, w_ple_gate=w_ple_gate,
                   w_ple_proj=w_ple_proj)
    h = x
    for i in range(p.shape[0]):
        h = _layer(i, h, p[i], rel_bias, {name: v[i] for name, v in stacked.items()})
    return h
```

```python
import functools
import math

import jax
import jax.numpy as jnp
from jax import lax
from jax.experimental import pallas as pl
from jax.experimental.pallas import tpu as pltpu

F32 = jnp.float32
BF16 = jnp.bfloat16

HEAD_DIM = 64
N_FOX_HEADS = 8
N_DIFF_HEADS = 4
GROUP = 512
LANES = 128
NUM_BUCKETS = 32
MAX_DISTANCE = 128
N_EXPERTS = 32
TOP_K = 4
SWIGLU_ALPHA = 1.702
SWIGLU_LIMIT = 7.0
MOE_BLOCK = 256
IDX_ALIGN = 128
NORM_EPS = 1e-6
NEG_INF = -1e30
LOG2E = math.log2(math.e)
N_SPLIT = 3

TM = 512
TQ = 256
TK = 256
LOOKAHEAD = 2
VMEM_LIMIT = 48 * 1024 * 1024
EXPERT_VMEM_LIMIT = 58 * 1024 * 1024


def _params(n_axes):
    return pltpu.CompilerParams(dimension_semantics=("arbitrary",) * n_axes,
                                vmem_limit_bytes=VMEM_LIMIT)


def _dot(a, b):
    return jnp.dot(a, b, preferred_element_type=F32)


def _dot_nt(a, b):
    return lax.dot_general(a, b, (((1,), (1,)), ((), ())), preferred_element_type=F32)


def _split3(v):
    hi = v.astype(BF16)
    r = v - hi.astype(F32)
    mid = r.astype(BF16)
    lo = (r - mid.astype(F32)).astype(BF16)
    return hi, mid, lo


def _inproj_kernel(x_ref, g_ref, wfq_ref, wfk_ref, wfv_ref, wdq_ref, wdk_ref, wdv_ref, wf_ref, bf_ref,
                   gfq_ref, gfk_ref, gdq_ref, gdk_ref, seg_ref, tri_ref, place_ref,
                   fq_ref, fk_ref, fv_ref, dq_ref, dk_ref, dv_ref, carry_ref):
    @pl.when(pl.program_id(1) == 0)
    def _():
        carry_ref[...] = jnp.zeros_like(carry_ref)

    xf = x_ref[...]
    a = (xf * lax.rsqrt(jnp.mean(xf * xf, axis=-1, keepdims=True) + NORM_EPS) * g_ref[...]).astype(BF16)
    seg = seg_ref[...]

    def q_group(w_ref, gain_ref, out_ref):
        acc = _dot_nt(w_ref[...], a)
        ms = _dot(seg, (acc * acc).astype(BF16))
        out_ref[...] = (acc * lax.rsqrt(ms + NORM_EPS) * gain_ref[...]).astype(BF16)

    def k_group(w_ref, gain_ref):
        acc = _dot(a, w_ref[...])
        ms = _dot((acc * acc).astype(BF16), seg)
        return (acc * lax.rsqrt(ms + NORM_EPS) * gain_ref[...]).astype(BF16)

    def v_group(w_ref, out_ref):
        acc = _dot_nt(w_ref[...], a).astype(BF16)
        for hb in range(GROUP // LANES):
            for j in range(TM // TK):
                out_ref[hb, j] = acc[hb * LANES:(hb + 1) * LANES, j * TK:(j + 1) * TK]

    q_group(wfq_ref, gfq_ref, fq_ref)
    v_group(wfv_ref, fv_ref)
    q_group(wdq_ref, gdq_ref, dq_ref)
    dk_ref[...] = k_group(wdk_ref, gdk_ref)
    v_group(wdv_ref, dv_ref)

    fl = _dot(a, wf_ref[...]) + bf_ref[...]
    logf = jnp.minimum(fl, 0.0) - jnp.log1p(jnp.exp(-jnp.abs(fl)))
    tri = tri_ref[...]
    cs = carry_ref[...]
    for part in _split3(logf):
        cs = cs + _dot(tri, part)
    carry_ref[...] = cs[TM - 1:TM, :]

    fk = k_group(wfk_ref, gfk_ref)
    extra = jnp.zeros((TM, GROUP), F32)
    for t, part in enumerate(_split3(cs * LOG2E)):
        extra = extra + _dot(part, place_ref[t])
    extra = extra.astype(BF16)
    for hb in range(GROUP // LANES):
        fk_ref[:, 2 * hb * LANES:(2 * hb + 1) * LANES] = fk[:, hb * LANES:(hb + 1) * LANES]
        fk_ref[:, (2 * hb + 1) * LANES:(2 * hb + 2) * LANES] = extra[:, hb * LANES:(hb + 1) * LANES]


def _two_map_queries(qT_blk):
    row = lax.broadcasted_iota(jnp.int32, qT_blk.shape, 0)
    zero = jnp.zeros_like(qT_blk)
    return jnp.concatenate([jnp.where(row < HEAD_DIM, qT_blk, zero),
                            jnp.where(row >= HEAD_DIM, qT_blk, zero)], axis=1)


def _softmax_step(sT, vT_blk, carry):
    m, l, acc = carry
    m_new = jnp.maximum(m, jnp.max(sT, axis=0, keepdims=True))
    alpha = jnp.exp2(m - m_new)
    pT = jnp.exp2(sT - m_new)
    l = alpha * l + jnp.sum(pT, axis=0, keepdims=True)
    acc = alpha * acc + _dot(vT_blk, pT.astype(BF16))
    return m_new, l, acc


def _causal_keep():
    key = lax.broadcasted_iota(jnp.int32, (TK, 2 * TQ), 0)
    col = lax.broadcasted_iota(jnp.int32, (TK, 2 * TQ), 1)
    qry = jnp.where(col >= TQ, col - TQ, col)
    return key <= qry


def _init_carry():
    return (jnp.full((1, 2 * TQ), NEG_INF, F32), jnp.zeros((1, 2 * TQ), F32),
            jnp.zeros((LANES, 2 * TQ), F32))


def _causal_sweep(seq, k_ref, vT_ref, queries, adjust, finish):
    pairs = [(qi, kj) for qi in range(seq // TQ) for kj in range(qi + 1)]
    q_ops = {}

    def score(qi, kj):
        if qi not in q_ops:
            q_ops[qi] = queries(qi)
        return _dot(k_ref[kj * TK:(kj + 1) * TK, :], q_ops[qi])

    pending = [score(*pr) for pr in pairs[:LOOKAHEAD]]
    carry = None
    for n, (qi, kj) in enumerate(pairs):
        if n + LOOKAHEAD < len(pairs):
            pending.append(score(*pairs[n + LOOKAHEAD]))
        if kj == 0:
            carry = _init_carry()
        carry = _softmax_step(adjust(pending.pop(0), qi, kj), vT_ref[kj], carry)
        if kj == qi:
            m, l, acc = carry
            finish(qi, acc * (1.0 / l))


def _fox_kernel(qT_ref, k_ref, vT_ref, o_ref):
    seq = k_ref.shape[0]
    keep = _causal_keep()
    row = lax.broadcasted_iota(jnp.int32, (LANES, TQ), 0)
    arow = lax.broadcasted_iota(jnp.int32, (LANES, 2 * TQ), 0)
    acol = lax.broadcasted_iota(jnp.int32, (LANES, 2 * TQ), 1)
    first = (acol < TQ) & (arow < N_SPLIT)
    second = (acol >= TQ) & (arow >= N_SPLIT) & (arow < 2 * N_SPLIT)
    minus_c = jnp.where(first | second, -1.0, 0.0).astype(BF16)

    def queries(qi):
        return jnp.concatenate([_two_map_queries(qT_ref[:, qi * TQ:(qi + 1) * TQ]), minus_c], axis=0)

    def adjust(sT, qi, kj):
        return jnp.where(keep, sT, NEG_INF) if kj == qi else sT

    def finish(qi, oT):
        o = jnp.where(row < HEAD_DIM, oT[:, :TQ], oT[:, TQ:])
        o_ref[qi * TQ:(qi + 1) * TQ, :] = o.T.astype(BF16)

    _causal_sweep(seq, k_ref, vT_ref, queries, adjust, finish)


def _diff_kernel(lam_init, qT_ref, k_ref, vT_ref, bias_ref, lam_ref, gsub_ref, o_ref):
    seq = k_ref.shape[0]
    lp = lam_ref[...]
    lam = (jnp.exp(jnp.sum(lp[0:1] * lp[1:2], axis=1, keepdims=True))
           - jnp.exp(jnp.sum(lp[2:3] * lp[3:4], axis=1, keepdims=True)) + lam_init)

    def queries(qi):
        return _two_map_queries(qT_ref[:, qi * TQ:(qi + 1) * TQ])

    def adjust(sT, qi, kj):
        if kj >= qi - 1:
            b = bias_ref[qi - kj]
            sT = jnp.concatenate([sT[:, :TQ] + b, sT[:, TQ:] + b], axis=1)
        return sT

    def finish(qi, oT):
        o = oT[:, :TQ] - lam * oT[:, TQ:]
        y = o * lax.rsqrt(jnp.mean(o * o, axis=0, keepdims=True) + NORM_EPS) * gsub_ref[...]
        o_ref[qi * TQ:(qi + 1) * TQ, :] = y.T.astype(BF16)

    _causal_sweep(seq, k_ref, vT_ref, queries, adjust, finish)


def _mix_kernel(x_ref, fox_ref, dif_ref, wa_ref, wb_ref, g_ref, wr_hi_ref, wr_lo_ref, br_ref,
                h_ref, xm_ref, topi_ref, gate_ref, cnt_ref, carry_ref):
    @pl.when(pl.program_id(0) == 0)
    def _():
        carry_ref[...] = jnp.zeros_like(carry_ref)

    h = x_ref[...] + _dot(fox_ref[...], wa_ref[...]) + _dot(dif_ref[...], wb_ref[...])
    h_ref[...] = h
    xm = h * lax.rsqrt(jnp.mean(h * h, axis=-1, keepdims=True) + NORM_EPS) * g_ref[...]
    xm_ref[:, 0, :] = xm

    x_hi = xm.astype(BF16)
    x_lo = (xm - x_hi.astype(F32)).astype(BF16)
    logits = (_dot(x_hi, wr_hi_ref[...]) + _dot(x_hi, wr_lo_ref[...]) + _dot(x_lo, wr_hi_ref[...])
              + br_ref[...])

    lane = lax.broadcasted_iota(jnp.int32, logits.shape, 1)
    vals, sels, idxs = [], [], []
    for _ in range(TOP_K):
        mx = jnp.max(logits, axis=1, keepdims=True)
        idx = jnp.min(jnp.where(logits == mx, lane, LANES), axis=1, keepdims=True)
        sel = lane == idx
        logits = jnp.where(sel, -jnp.inf, logits)
        vals.append(mx)
        sels.append(sel)
        idxs.append(idx)
    exps = [jnp.exp(v - vals[0]) for v in vals]
    denom = exps[0] + exps[1] + exps[2] + exps[3]

    multi_hot = sum(s.astype(F32) for s in sels)
    topi = jnp.zeros(logits.shape, jnp.int32)
    gate = jnp.zeros(logits.shape, F32)
    for k in range(TOP_K):
        topi = jnp.where(lane == k, idxs[k], topi)
        gate = jnp.where(lane == k, exps[k] / denom, gate)
    topi_ref[...] = topi
    gate_ref[...] = gate
    carry_ref[...] = carry_ref[...] + jnp.sum(multi_hot, axis=0, keepdims=True)
    cnt_ref[...] = carry_ref[...]


def _expert_kernel(be_ref, nu_ref, off_ref, nv_ref, tok_hbm, asg_hbm, x_hbm, w1f_ref, b1_ref, w2f_ref, b2_ref,
                   perm_ref, y_hbm, tok_ref, asg_ref, x0, x1, y0, y1, w1_ref, w2_ref, idx_sem, g_sem, s_sem):
    i = pl.program_id(0)
    n_used = nu_ref[0]
    n_blocks = be_ref.shape[0]
    spare_row = y_hbm.shape[0] - MOE_BLOCK
    xbufs, ybufs = (x0, x1), (y0, y1)

    def tok_copy(blk, slot):
        start = pl.multiple_of(off_ref[blk], IDX_ALIGN)
        return pltpu.make_async_copy(tok_hbm.at[pl.ds(start, MOE_BLOCK)], tok_ref.at[slot], idx_sem.at[0, slot])

    def asg_copy(blk, slot):
        start = pl.multiple_of(off_ref[blk], IDX_ALIGN)
        return pltpu.make_async_copy(asg_hbm.at[pl.ds(start, MOE_BLOCK)], asg_ref.at[slot], idx_sem.at[1, slot])

    def asg_ready(blk, slot):
        asg_copy(0, slot).wait()

        def fix(r, carry):
            asg_ref[slot, r] = spare_row + r
            return carry
        lax.fori_loop(nv_ref[blk], MOE_BLOCK, fix, 0)

    def row_loop(fn):
        for r in range(MOE_BLOCK):
            fn(r)

    def gather_issue(slot):
        row_loop(lambda r: pltpu.make_async_copy(x_hbm.at[tok_ref[slot, r]], xbufs[slot].at[pl.ds(r, 1)],
                                                 g_sem.at[slot]).start())

    def gather_wait(slot):
        pltpu.make_async_copy(xbufs[slot], xbufs[slot], g_sem.at[slot]).wait()

    def scatter_issue(slot):
        row_loop(lambda r: pltpu.make_async_copy(ybufs[slot].at[pl.ds(r, 1)], y_hbm.at[asg_ref[slot, r]],
                                                 s_sem.at[slot]).start())

    def scatter_wait(slot):
        pltpu.make_async_copy(ybufs[slot], ybufs[slot], s_sem.at[slot]).wait()

    @pl.when(i == 0)
    def _():
        y1[...] = jnp.zeros(y1.shape, y1.dtype)
        for r in range(MOE_BLOCK):
            asg_ref[1, r] = spare_row + r
        tok_copy(0, 0).start()
        tok_copy(0, 0).wait()
        tok_copy(jnp.minimum(1, n_blocks - 1), 1).start()
        gather_issue(0)

    @pl.when((i < n_used) & ((i == 0) | (be_ref[i] != be_ref[jnp.maximum(i - 1, 0)])))
    def _():
        for g in range(w1f_ref.shape[1] // (2 * LANES)):
            cols = slice(2 * g * LANES, 2 * (g + 1) * LANES)
            w1_ref[:, cols] = _dot(w1f_ref[:, cols].astype(BF16), perm_ref[...]).astype(BF16)
        w2_ref[...] = w2f_ref[...].astype(BF16)

    def step(cur):
        nxt = 1 - cur
        asg_copy(i, cur).start()
        tok_copy(0, nxt).wait()
        gather_issue(nxt)
        tok_copy(jnp.minimum(i + 2, n_blocks - 1), cur).start()

        @pl.when(i >= 1)
        def _():
            asg_ready(i - 1, nxt)
            scatter_wait(cur)

        pl.when(n_used > 0)(functools.partial(scatter_issue, nxt))
        gather_wait(cur)
        xbuf, ybuf = xbufs[cur], ybufs[cur]
        xs = xbuf[...].astype(BF16)
        h = _dot(xs, w1_ref[...]) + b1_ref[...]
        acts = []
        for c in range(h.shape[1] // (2 * LANES)):
            glu = jnp.minimum(h[:, 2 * c * LANES:(2 * c + 1) * LANES], SWIGLU_LIMIT)
            lin = jnp.clip(h[:, (2 * c + 1) * LANES:(2 * c + 2) * LANES], -SWIGLU_LIMIT, SWIGLU_LIMIT)
            acts.append((glu * jax.nn.sigmoid(SWIGLU_ALPHA * glu) * (lin + 1.0)).astype(BF16))
        out = _dot(jnp.concatenate(acts, axis=1), w2_ref[...]) + b2_ref[...]
        ybuf[...] = out

        @pl.when(i == n_used - 1)
        def _():
            asg_ready(i, cur)
            scatter_issue(cur)
            scatter_wait(nxt)
            scatter_wait(cur)
            gather_wait(nxt)
            tok_copy(0, cur).wait()

    for parity in range(2):
        pl.when((i < n_used) & (i % 2 == parity))(functools.partial(step, parity))


def _final_kernel(h_ref, *rest):
    y_refs, (gate_ref, g_ref, wg_ref, p_ref, wp_ref, o_ref) = rest[:TOP_K], rest[TOP_K:]
    gates = gate_ref[...]
    moe = sum(gates[:, k:k + 1] * y_ref[:, 0, :] for k, y_ref in enumerate(y_refs))
    h = h_ref[...] + moe
    n = (h * lax.rsqrt(jnp.mean(h * h, axis=-1, keepdims=True) + NORM_EPS) * g_ref[...]).astype(BF16)
    gate = jax.nn.sigmoid(_dot(n, wg_ref[...]))
    o_ref[...] = h + gate * _dot(p_ref[...].astype(BF16), wp_ref[...])


def _t5_bucket(rel):
    n = jnp.maximum(rel, 0)
    max_exact = NUM_BUCKETS // 2
    nf = jnp.maximum(n, 1).astype(F32)
    large = max_exact + (jnp.log(nf / max_exact) / math.log(MAX_DISTANCE / max_exact)
                         * (NUM_BUCKETS - max_exact)).astype(jnp.int32)
    large = jnp.minimum(large, NUM_BUCKETS - 1)
    return jnp.where(n < max_exact, n, large)


def _full(shape):
    return pl.BlockSpec(shape, lambda *_: (0,) * len(shape))


def _layer(i, h_in, p_i, rel_bias, prm):
    bsz, seq, d = h_in.shape
    n_tok = bsz * seq
    ns = seq // TM
    nk = seq // TK
    x2 = h_in.reshape(n_tok, d)

    w_in = prm['w_in']
    fox_w = N_FOX_HEADS * HEAD_DIM
    offs = [0, fox_w, 2 * fox_w, 3 * fox_w, 3 * fox_w + N_FOX_HEADS]
    offs += [offs[4] + GROUP, offs[4] + 2 * GROUP, offs[4] + 3 * GROUP]
    col = lambda a, b: w_in[:, a:b].astype(BF16)
    wfqT, wfk, wfvT = col(offs[0], offs[1]).T, col(offs[1], offs[2]), col(offs[2], offs[3]).T
    wdqT, wdk, wdvT = col(offs[4], offs[5]).T, col(offs[5], offs[6]), col(offs[6], offs[7]).T
    wf = jnp.zeros((d, LANES), BF16).at[:, :N_FOX_HEADS].set(col(offs[3], offs[4]))
    bf = jnp.zeros((1, LANES), F32).at[0, :N_FOX_HEADS].set(prm['b_f'])
    scale = HEAD_DIM ** -0.5 * LOG2E
    rep = GROUP // HEAD_DIM
    heads = jnp.arange(N_FOX_HEADS)
    place = jnp.stack([jnp.zeros((LANES, GROUP), BF16)
                       .at[heads, (heads // 2) * LANES + (heads % 2) * N_SPLIT + t].set(1.0)
                       for t in range(N_SPLIT)])
    gfqT = jnp.broadcast_to((jnp.tile(prm['g_fox_q'], rep) * scale)[:, None], (GROUP, TM))
    gdqT = jnp.broadcast_to((jnp.tile(prm['g_diff_q'], rep) * scale)[:, None], (GROUP, TM))
    gfk = jnp.tile(prm['g_fox_k'], rep)[None, :]
    gdk = jnp.tile(prm['g_diff_k'], rep)[None, :]
    gi = jnp.arange(GROUP) // HEAD_DIM
    seg = jnp.where(gi[:, None] == gi[None, :], 1.0 / HEAD_DIM, 0.0).astype(BF16)
    ti = jnp.arange(TM)
    tri_incl = (ti[None, :] <= ti[:, None]).astype(BF16)

    w_spec_t = _full((GROUP, d))
    w_spec = _full((d, GROUP))
    qT_shape = jax.ShapeDtypeStruct((bsz, GROUP, seq), BF16)
    k_shape = jax.ShapeDtypeStruct((n_tok, GROUP), BF16)
    vT_shape = jax.ShapeDtypeStruct((bsz, GROUP // LANES, nk, LANES, TK), BF16)
    qT_spec = pl.BlockSpec((None, GROUP, TM), lambda b, s: (b, 0, s))
    k_spec = pl.BlockSpec((TM, GROUP), lambda b, s: (b * ns + s, 0))
    vT_spec = pl.BlockSpec((None, GROUP // LANES, TM // TK, LANES, TK), lambda b, s: (b, 0, s, 0, 0))
    fk_shape = jax.ShapeDtypeStruct((n_tok, 2 * GROUP), BF16)
    fk_spec = pl.BlockSpec((TM, 2 * GROUP), lambda b, s: (b * ns + s, 0))
    fqT, fk, fvT, dqT, dk, dvT = pl.pallas_call(
        _inproj_kernel,
        grid=(bsz, ns),
        in_specs=[pl.BlockSpec((TM, d), lambda b, s: (b * ns + s, 0)), _full((1, d)),
                  w_spec_t, w_spec, w_spec_t, w_spec_t, w_spec, w_spec_t, _full((d, LANES)), _full((1, LANES)),
                  _full((GROUP, TM)), _full((1, GROUP)), _full((GROUP, TM)), _full((1, GROUP)),
                  _full((GROUP, GROUP)), _full((TM, TM)), _full((N_SPLIT, LANES, GROUP))],
        out_specs=[qT_spec, fk_spec, vT_spec, qT_spec, k_spec, vT_spec],
        out_shape=[qT_shape, fk_shape, vT_shape, qT_shape, k_shape, vT_shape],
        scratch_shapes=[pltpu.VMEM((1, LANES), F32)],
        compiler_params=_params(2),
        name="inproj",
    )(x2, prm['g_attn'][None, :], wfqT, wfk, wfvT, wdqT, wdk, wdvT, wf, bf, gfqT, gfk, gdqT, gdk, seg, tri_incl,
      place)

    n_hb = GROUP // LANES
    att_q = pl.BlockSpec((None, LANES, seq), lambda b, h: (b, h, 0))
    att_k = pl.BlockSpec((seq, LANES), lambda b, h: (b, h))
    att_v = pl.BlockSpec((None, None, nk, LANES, TK), lambda b, h: (b, h, 0, 0, 0))
    att_o = pl.BlockSpec((seq, LANES), lambda b, h: (b, h))
    fox_out = pl.pallas_call(
        _fox_kernel,
        grid=(bsz, n_hb),
        in_specs=[att_q, pl.BlockSpec((seq, 2 * LANES), lambda b, h: (b, h)), att_v],
        out_specs=att_o,
        out_shape=jax.ShapeDtypeStruct((n_tok, GROUP), BF16),
        compiler_params=_params(2),
        name="fox_attention",
    )(fqT, fk, fvT)

    kpos = jnp.arange(TK)[:, None]
    qpos = jnp.arange(TQ)[None, :]
    rel = jnp.stack([qpos - kpos, qpos - kpos + TK])
    assert TK + 1 >= MAX_DISTANCE and TQ == TK
    table = (rel_bias.astype(F32) - rel_bias[NUM_BUCKETS - 1].astype(F32)[None, :]) * LOG2E
    onehot = _t5_bucket(rel)[None, ..., None] == jnp.arange(NUM_BUCKETS, dtype=jnp.int32)
    biasT = jnp.sum(jnp.where(onehot, table.T[:, None, None, None, :], 0.0), axis=-1)
    biasT = jnp.where((rel >= 0)[None], biasT, NEG_INF)
    lam_init = 0.8 - 0.6 * math.exp(-0.3 * i)
    lam_p = jnp.stack([prm['lambda_q1'], prm['lambda_k1'], prm['lambda_q2'], prm['lambda_k2']]).astype(F32)
    gsubT = jnp.broadcast_to((prm['g_subln'] * (1.0 - lam_init))[:, None], (LANES, TQ))
    diff_out = pl.pallas_call(
        functools.partial(_diff_kernel, lam_init),
        grid=(bsz, N_DIFF_HEADS),
        in_specs=[att_q, att_k, att_v, pl.BlockSpec((None, 2, TK, TQ), lambda b, h: (h, 0, 0, 0)),
                  _full((4, HEAD_DIM)), _full((LANES, TQ))],
        out_specs=att_o,
        out_shape=jax.ShapeDtypeStruct((n_tok, GROUP), BF16),
        compiler_params=_params(2),
        name="diff_attention",
    )(dqT, dk, dvT, biasT, lam_p, gsubT)

    w_out = prm['w_out'].astype(BF16)
    wr = jnp.zeros((d, LANES), F32).at[:, :N_EXPERTS].set(prm['w_router'])
    wr_hi = wr.astype(BF16)
    wr_lo = (wr - wr_hi.astype(F32)).astype(BF16)
    br = jnp.full((1, LANES), NEG_INF, F32).at[0, :N_EXPERTS].set(prm['b_router'])
    row_spec = lambda w: pl.BlockSpec((TM, w), lambda t: (t, 0))
    h1, xm, topi, gates, counts = pl.pallas_call(
        _mix_kernel,
        grid=(n_tok // TM,),
        in_specs=[row_spec(d), row_spec(GROUP), row_spec(GROUP), _full((GROUP, d)), _full((GROUP, d)),
                  _full((1, d)), _full((d, LANES)), _full((d, LANES)), _full((1, LANES))],
        out_specs=[row_spec(d), pl.BlockSpec((TM, 1, d), lambda t: (t, 0, 0)), row_spec(LANES),
                   row_spec(LANES), _full((1, LANES))],
        out_shape=[jax.ShapeDtypeStruct((n_tok, d), F32), jax.ShapeDtypeStruct((n_tok, 1, d), F32),
                   jax.ShapeDtypeStruct((n_tok, LANES), jnp.int32), jax.ShapeDtypeStruct((n_tok, LANES), F32),
                   jax.ShapeDtypeStruct((1, LANES), F32)],
        scratch_shapes=[pltpu.VMEM((1, LANES), F32)],
        compiler_params=_params(1),
        name="mix_router",
    )(x2, fox_out, diff_out, w_out[:GROUP], w_out[GROUP:], prm['g_mlp'][None, :], wr_hi, wr_lo, br)

    m = n_tok * TOP_K
    n_blocks = (m + N_EXPERTS * MOE_BLOCK + MOE_BLOCK - 1) // MOE_BLOCK
    p_rows = n_blocks * MOE_BLOCK
    cnt = counts[0, :N_EXPERTS].astype(jnp.int32)
    padded = ((cnt + MOE_BLOCK - 1) // MOE_BLOCK) * MOE_BLOCK
    pad_ends = jnp.cumsum(padded)
    pad_starts = pad_ends - padded
    block_start = jnp.arange(n_blocks, dtype=jnp.int32) * MOE_BLOCK
    block_expert = jnp.minimum(jnp.sum(pad_ends[None, :] <= block_start[:, None], axis=1),
                               N_EXPERTS - 1).astype(jnp.int32)
    n_used = (pad_ends[-1] // MOE_BLOCK).astype(jnp.int32)[None]
    experts = jnp.arange(N_EXPERTS, dtype=jnp.int32)
    real = topi[:, :TOP_K] * (2 * m) + jnp.arange(m, dtype=jnp.int32).reshape(n_tok, TOP_K)
    fill = jnp.arange(IDX_ALIGN - 1, dtype=jnp.int32)[None, :]
    need = (-cnt) % IDX_ALIGN
    filler = jnp.where(fill < need[:, None], experts[:, None] * (2 * m) + m + fill,
                       N_EXPERTS * (2 * m) + experts[:, None] * IDX_ALIGN + fill)
    keys = jnp.sort(jnp.concatenate([real.reshape(m), filler.reshape(-1)]))
    flat = jnp.concatenate([(keys % (2 * m)) % m, jnp.zeros((MOE_BLOCK,), jnp.int32)])
    grouped_tok = flat // TOP_K
    grouped_row = (flat % TOP_K) * n_tok + grouped_tok
    onehot_e = block_expert[:, None] == experts[None, :]
    pick = lambda v: jnp.sum(jnp.where(onehot_e, v[None, :], 0), axis=1)
    in_group = block_start - pick(pad_starts)
    n_valid = jnp.clip(pick(cnt) - in_group, 0, MOE_BLOCK)
    group_start = jnp.cumsum(cnt + need) - (cnt + need)
    first = jnp.clip(pick(group_start) + in_group, 0, keys.shape[0]) // IDX_ALIGN * IDX_ALIGN

    d_ff = prm['w2'].shape[1]
    n_grp = 2 * d_ff // (2 * LANES)
    b1p = prm['b1'].reshape(N_EXPERTS, n_grp, LANES, 2).transpose(0, 1, 3, 2).reshape(N_EXPERTS, 1, 2 * d_ff)
    b2 = prm['b2'].reshape(N_EXPERTS, 1, d)
    pj = jnp.arange(LANES)
    perm = (jnp.zeros((2 * LANES, 2 * LANES), BF16).at[2 * pj, pj].set(1.0)
            .at[2 * pj + 1, LANES + pj].set(1.0))
    any_spec = pl.BlockSpec(memory_space=pl.ANY)
    ys = pl.pallas_call(
        _expert_kernel,
        grid_spec=pltpu.PrefetchScalarGridSpec(
            num_scalar_prefetch=4,
            grid=(n_blocks,),
            in_specs=[any_spec, any_spec, any_spec,
                      pl.BlockSpec((None, d, 2 * d_ff), lambda t, be, *_:(be[t], 0, 0)),
                      pl.BlockSpec((None, 1, 2 * d_ff), lambda t, be, *_:(be[t], 0, 0)),
                      pl.BlockSpec((None, d_ff, d), lambda t, be, *_:(be[t], 0, 0)),
                      pl.BlockSpec((None, 1, d), lambda t, be, *_:(be[t], 0, 0)),
                      pl.BlockSpec((2 * LANES, 2 * LANES), lambda t, be, *_:(0, 0))],
            out_specs=any_spec,
            scratch_shapes=[pltpu.SMEM((2, MOE_BLOCK), jnp.int32), pltpu.SMEM((2, MOE_BLOCK), jnp.int32)]
            + [pltpu.VMEM((MOE_BLOCK, d), F32)] * 4
            + [pltpu.VMEM((d, 2 * d_ff), BF16), pltpu.VMEM((d_ff, d), BF16)]
            + [pltpu.SemaphoreType.DMA((2, 2)), pltpu.SemaphoreType.DMA((2,)), pltpu.SemaphoreType.DMA((2,))],
        ),
        out_shape=jax.ShapeDtypeStruct((m + MOE_BLOCK, 1, d), F32),
        compiler_params=pltpu.CompilerParams(dimension_semantics=("arbitrary",),
                                             vmem_limit_bytes=EXPERT_VMEM_LIMIT),
        name="experts",
    )(block_expert, n_used, first, n_valid, grouped_tok, grouped_row, xm, prm['w1'], b1p, prm['w2'], b2, perm)

    ple = p_i.shape[-1]
    out = pl.pallas_call(
        _final_kernel,
        grid=(n_tok // TM,),
        in_specs=[row_spec(d)]
        + [pl.BlockSpec((TM, 1, d), functools.partial(lambda k, t: (k * (n_tok // TM) + t, 0, 0), k))
           for k in range(TOP_K)]
        + [row_spec(LANES),
                  _full((1, d)), _full((d, d)), row_spec(ple), _full((ple, d))],
        out_specs=row_spec(d),
        out_shape=jax.ShapeDtypeStruct((n_tok, d), F32),
        compiler_params=_params(1),
        name="combine_ple",
    )(h1, ys, ys, ys, ys, gates, prm['g_ple'][None, :],
      prm['w_ple_gate'].astype(BF16), p_i.reshape(n_tok, ple), prm['w_ple_proj'].astype(BF16))
    return out.reshape(bsz, seq, d)


def kernel(x, p, rel_bias, g_attn, w_in, b_f, g_fox_q, g_fox_k, g_diff_q, g_diff_k, lambda_q1, lambda_k1,
           lambda_q2, lambda_k2, g_subln, w_out, g_mlp, w_router, b_router, w1, b1, w2, b2, g_ple,
           w_ple_gate, w_ple_proj):
    stacked = dict(g_attn=g_attn, w_in=w_in, b_f=b_f, g_fox_q=g_fox_q, g_fox_k=g_fox_k, g_diff_q=g_diff_q,
                   g_diff_k=g_diff_k, lambda_q1=lambda_q1, lambda_k1=lambda_k1, lambda_q2=lambda_q2,
                   lambda_k2=lambda_k2, g_subln=g_subln, w_out=w_out, g_mlp=g_mlp, w_router=w_router,
                   b_router=b_router, w1=w1, b1=b1, w2=w2, b2=b2, g_ple=g_ple, w_ple_gate=w_ple_gate,
                   w_ple_proj=w_ple_proj)
    h = x
    for i in range(p.shape[0]):
        h = _layer(i, h, p[i], rel_bias, {name: v[i] for name, v in stacked.items()})
    return h
```

```python
import functools
import math

import jax
import jax.numpy as jnp
from jax import lax
from jax.experimental import pallas as pl
from jax.experimental.pallas import tpu as pltpu

F32 = jnp.float32
BF16 = jnp.bfloat16

HEAD_DIM = 64
N_FOX_HEADS = 8
N_DIFF_HEADS = 4
GROUP = 512
LANES = 128
NUM_BUCKETS = 32
MAX_DISTANCE = 128
N_EXPERTS = 32
TOP_K = 4
SWIGLU_ALPHA = 1.702
SWIGLU_LIMIT = 7.0
MOE_BLOCK = 256
IDX_ALIGN = 128
NORM_EPS = 1e-6
NEG_INF = -1e30
LOG2E = math.log2(math.e)
N_SPLIT = 3

TM = 512
TQ = 256
TK = 256
LOOKAHEAD = 2
VMEM_LIMIT = 48 * 1024 * 1024
EXPERT_VMEM_LIMIT = 58 * 1024 * 1024


def _params(n_axes):
    return pltpu.CompilerParams(dimension_semantics=("arbitrary",) * n_axes,
                                vmem_limit_bytes=VMEM_LIMIT)


def _dot(a, b):
    return jnp.dot(a, b, preferred_element_type=F32)


def _dot_nt(a, b):
    return lax.dot_general(a, b, (((1,), (1,)), ((), ())), preferred_element_type=F32)


def _split3(v):
    hi = v.astype(BF16)
    r = v - hi.astype(F32)
    mid = r.astype(BF16)
    lo = (r - mid.astype(F32)).astype(BF16)
    return hi, mid, lo


def _inproj_kernel(x_ref, g_ref, wfq_ref, wfk_ref, wfv_ref, wdq_ref, wdk_ref, wdv_ref, wf_ref, bf_ref,
                   gfq_ref, gfk_ref, gdq_ref, gdk_ref, seg_ref, tri_ref, place_ref,
                   fq_ref, fk_ref, fv_ref, dq_ref, dk_ref, dv_ref, carry_ref):
    @pl.when(pl.program_id(1) == 0)
    def _():
        carry_ref[...] = jnp.zeros_like(carry_ref)

    xf = x_ref[...]
    a = (xf * lax.rsqrt(jnp.mean(xf * xf, axis=-1, keepdims=True) + NORM_EPS) * g_ref[...]).astype(BF16)
    seg = seg_ref[...]

    fl = _dot(a, wf_ref[...]) + bf_ref[...]
    fq_acc = _dot_nt(wfq_ref[...], a)
    dq_acc = _dot_nt(wdq_ref[...], a)
    fk_acc = _dot(a, wfk_ref[...])
    dk_acc = _dot(a, wdk_ref[...])
    fv_acc = _dot_nt(wfv_ref[...], a)
    dv_acc = _dot_nt(wdv_ref[...], a)

    half = seg.shape[0]

    def q_norm(acc, gain_ref):
        sq = (acc * acc).astype(BF16)
        ms = jnp.concatenate([_dot(seg, sq[:half]), _dot(seg, sq[half:])], axis=0)
        return (acc * lax.rsqrt(ms + NORM_EPS) * gain_ref[...]).astype(BF16)

    def k_norm(acc, gain_ref):
        sq = (acc * acc).astype(BF16)
        ms = jnp.concatenate([_dot(sq[:, :half], seg), _dot(sq[:, half:], seg)], axis=1)
        return (acc * lax.rsqrt(ms + NORM_EPS) * gain_ref[...]).astype(BF16)

    def v_store(acc, out_ref):
        acc = acc.astype(BF16)
        for hb in range(GROUP // LANES):
            for j in range(TM // TK):
                out_ref[hb, j] = acc[hb * LANES:(hb + 1) * LANES, j * TK:(j + 1) * TK]

    logf = jnp.minimum(fl, 0.0) - jnp.log1p(jnp.exp(-jnp.abs(fl)))
    sums = _dot(tri_ref[...], jnp.concatenate(_split3(logf), axis=1))
    cs = carry_ref[...] + sums[:, :LANES] + sums[:, LANES:2 * LANES] + sums[:, 2 * LANES:]
    carry_ref[...] = cs[TM - 1:TM, :]

    fq_ref[...] = q_norm(fq_acc, gfq_ref)
    dq_ref[...] = q_norm(dq_acc, gdq_ref)
    fk = k_norm(fk_acc, gfk_ref)
    dk_ref[...] = k_norm(dk_acc, gdk_ref)
    v_store(fv_acc, fv_ref)
    v_store(dv_acc, dv_ref)

    extra = _dot(jnp.concatenate(_split3(cs * LOG2E), axis=1), place_ref[...]).astype(BF16)
    for hb in range(GROUP // LANES):
        fk_ref[:, 2 * hb * LANES:(2 * hb + 1) * LANES] = fk[:, hb * LANES:(hb + 1) * LANES]
        fk_ref[:, (2 * hb + 1) * LANES:(2 * hb + 2) * LANES] = extra[:, hb * LANES:(hb + 1) * LANES]


def _two_map_queries(qT_blk):
    row = lax.broadcasted_iota(jnp.int32, qT_blk.shape, 0)
    zero = jnp.zeros_like(qT_blk)
    return jnp.concatenate([jnp.where(row < HEAD_DIM, qT_blk, zero),
                            jnp.where(row >= HEAD_DIM, qT_blk, zero)], axis=1)


def _softmax_step(sT, vT_blk, carry):
    m, l, acc = carry
    m_new = jnp.maximum(m, jnp.max(sT, axis=0, keepdims=True))
    alpha = jnp.exp2(m - m_new)
    pT = jnp.exp2(sT - m_new)
    l = alpha * l + jnp.sum(pT, axis=0, keepdims=True)
    acc = alpha * acc + _dot(vT_blk, pT.astype(BF16))
    return m_new, l, acc


def _causal_keep():
    key = lax.broadcasted_iota(jnp.int32, (TK, 2 * TQ), 0)
    col = lax.broadcasted_iota(jnp.int32, (TK, 2 * TQ), 1)
    qry = jnp.where(col >= TQ, col - TQ, col)
    return key <= qry


def _init_carry():
    return (jnp.full((1, 2 * TQ), NEG_INF, F32), jnp.zeros((1, 2 * TQ), F32),
            jnp.zeros((LANES, 2 * TQ), F32))


def _causal_sweep(seq, k_ref, vT_ref, queries, adjust, finish):
    pairs = [(qi, kj) for qi in range(seq // TQ) for kj in range(qi + 1)]
    q_ops = {}

    def score(qi, kj):
        if qi not in q_ops:
            q_ops[qi] = queries(qi)
        return _dot(k_ref[kj * TK:(kj + 1) * TK, :], q_ops[qi])

    pending = [score(*pr) for pr in pairs[:LOOKAHEAD]]
    carry = None
    for n, (qi, kj) in enumerate(pairs):
        if n + LOOKAHEAD < len(pairs):
            pending.append(score(*pairs[n + LOOKAHEAD]))
        if kj == 0:
            carry = _init_carry()
        carry = _softmax_step(adjust(pending.pop(0), qi, kj), vT_ref[kj], carry)
        if kj == qi:
            m, l, acc = carry
            finish(qi, acc * (1.0 / l))


def _fox_kernel(qT_ref, k_ref, vT_ref, o_ref):
    seq = k_ref.shape[0]
    keep = _causal_keep()
    row = lax.broadcasted_iota(jnp.int32, (LANES, TQ), 0)
    arow = lax.broadcasted_iota(jnp.int32, (LANES, 2 * TQ), 0)
    acol = lax.broadcasted_iota(jnp.int32, (LANES, 2 * TQ), 1)
    first = (acol < TQ) & (arow < N_SPLIT)
    second = (acol >= TQ) & (arow >= N_SPLIT) & (arow < 2 * N_SPLIT)
    minus_c = jnp.where(first | second, -1.0, 0.0).astype(BF16)

    def queries(qi):
        return jnp.concatenate([_two_map_queries(qT_ref[:, qi * TQ:(qi + 1) * TQ]), minus_c], axis=0)

    def adjust(sT, qi, kj):
        return jnp.where(keep, sT, NEG_INF) if kj == qi else sT

    def finish(qi, oT):
        o = jnp.where(row < HEAD_DIM, oT[:, :TQ], oT[:, TQ:])
        o_ref[qi * TQ:(qi + 1) * TQ, :] = o.T.astype(BF16)

    _causal_sweep(seq, k_ref, vT_ref, queries, adjust, finish)


def _diff_kernel(lam_init, qT_ref, k_ref, vT_ref, bias_ref, lam_ref, gsub_ref, o_ref):
    seq = k_ref.shape[0]
    lp = lam_ref[...]
    lam = (jnp.exp(jnp.sum(lp[0:1] * lp[1:2], axis=1, keepdims=True))
           - jnp.exp(jnp.sum(lp[2:3] * lp[3:4], axis=1, keepdims=True)) + lam_init)

    def queries(qi):
        return _two_map_queries(qT_ref[:, qi * TQ:(qi + 1) * TQ])

    def adjust(sT, qi, kj):
        if kj >= qi - 1:
            b = bias_ref[qi - kj]
            sT = jnp.concatenate([sT[:, :TQ] + b, sT[:, TQ:] + b], axis=1)
        return sT

    def finish(qi, oT):
        o = oT[:, :TQ] - lam * oT[:, TQ:]
        y = o * lax.rsqrt(jnp.mean(o * o, axis=0, keepdims=True) + NORM_EPS) * gsub_ref[...]
        o_ref[qi * TQ:(qi + 1) * TQ, :] = y.T.astype(BF16)

    _causal_sweep(seq, k_ref, vT_ref, queries, adjust, finish)


def _mix_kernel(x_ref, fox_ref, dif_ref, wa_ref, wb_ref, g_ref, wr_hi_ref, wr_lo_ref, br_ref,
                h_ref, xm_ref, topi_ref, gate_ref, cnt_ref, carry_ref):
    @pl.when(pl.program_id(0) == 0)
    def _():
        carry_ref[...] = jnp.zeros_like(carry_ref)

    h = x_ref[...] + _dot(fox_ref[...], wa_ref[...]) + _dot(dif_ref[...], wb_ref[...])
    h_ref[...] = h
    xm = h * lax.rsqrt(jnp.mean(h * h, axis=-1, keepdims=True) + NORM_EPS) * g_ref[...]
    xm_ref[:, 0, :] = xm

    x_hi = xm.astype(BF16)
    x_lo = (xm - x_hi.astype(F32)).astype(BF16)
    logits = (_dot(x_hi, wr_hi_ref[...]) + _dot(x_hi, wr_lo_ref[...]) + _dot(x_lo, wr_hi_ref[...])
              + br_ref[...])

    lane = lax.broadcasted_iota(jnp.int32, logits.shape, 1)
    vals, sels, idxs = [], [], []
    for _ in range(TOP_K):
        mx = jnp.max(logits, axis=1, keepdims=True)
        idx = jnp.min(jnp.where(logits == mx, lane, LANES), axis=1, keepdims=True)
        sel = lane == idx
        logits = jnp.where(sel, -jnp.inf, logits)
        vals.append(mx)
        sels.append(sel)
        idxs.append(idx)
    exps = [jnp.exp(v - vals[0]) for v in vals]
    denom = exps[0] + exps[1] + exps[2] + exps[3]

    multi_hot = sum(s.astype(F32) for s in sels)
    topi = jnp.zeros(logits.shape, jnp.int32)
    gate = jnp.zeros(logits.shape, F32)
    for k in range(TOP_K):
        topi = jnp.where(lane == k, idxs[k], topi)
        gate = jnp.where(lane == k, exps[k] / denom, gate)
    topi_ref[...] = topi
    gate_ref[...] = gate
    carry_ref[...] = carry_ref[...] + jnp.sum(multi_hot, axis=0, keepdims=True)
    cnt_ref[...] = carry_ref[...]


def _expert_kernel(be_ref, nu_ref, off_ref, nv_ref, tok_hbm, asg_hbm, x_hbm, w1f_ref, b1_ref, w2f_ref, b2_ref,
                   perm_ref, y_hbm, tok_ref, asg_ref, x0, x1, y0, y1, w1_ref, w2_ref, idx_sem, g_sem, s_sem):
    i = pl.program_id(0)
    n_used = nu_ref[0]
    n_blocks = be_ref.shape[0]
    spare_row = y_hbm.shape[0] - MOE_BLOCK
    xbufs, ybufs = (x0, x1), (y0, y1)

    def tok_copy(blk, slot):
        start = pl.multiple_of(off_ref[blk], IDX_ALIGN)
        return pltpu.make_async_copy(tok_hbm.at[pl.ds(start, MOE_BLOCK)], tok_ref.at[slot], idx_sem.at[0, slot])

    def asg_copy(blk, slot):
        start = pl.multiple_of(off_ref[blk], IDX_ALIGN)
        return pltpu.make_async_copy(asg_hbm.at[pl.ds(start, MOE_BLOCK)], asg_ref.at[slot], idx_sem.at[1, slot])

    def asg_ready(blk, slot):
        asg_copy(0, slot).wait()

        def fix(r, carry):
            asg_ref[slot, r] = spare_row + r
            return carry
        lax.fori_loop(nv_ref[blk], MOE_BLOCK, fix, 0)

    def row_loop(fn):
        for r in range(MOE_BLOCK):
            fn(r)

    def gather_issue(slot):
        row_loop(lambda r: pltpu.make_async_copy(x_hbm.at[tok_ref[slot, r]], xbufs[slot].at[pl.ds(r, 1)],
                                                 g_sem.at[slot]).start())

    def gather_wait(slot):
        pltpu.make_async_copy(xbufs[slot], xbufs[slot], g_sem.at[slot]).wait()

    def scatter_issue(slot):
        row_loop(lambda r: pltpu.make_async_copy(ybufs[slot].at[pl.ds(r, 1)], y_hbm.at[asg_ref[slot, r]],
                                                 s_sem.at[slot]).start())

    def scatter_wait(slot):
        pltpu.make_async_copy(ybufs[slot], ybufs[slot], s_sem.at[slot]).wait()

    @pl.when(i == 0)
    def _():
        y1[...] = jnp.zeros(y1.shape, y1.dtype)
        for r in range(MOE_BLOCK):
            asg_ref[1, r] = spare_row + r
        tok_copy(0, 0).start()
        tok_copy(0, 0).wait()
        tok_copy(jnp.minimum(1, n_blocks - 1), 1).start()
        gather_issue(0)

    @pl.when((i < n_used) & ((i == 0) | (be_ref[i] != be_ref[jnp.maximum(i - 1, 0)])))
    def _():
        for g in range(w1f_ref.shape[1] // (2 * LANES)):
            cols = slice(2 * g * LANES, 2 * (g + 1) * LANES)
            w1_ref[:, cols] = _dot(w1f_ref[:, cols].astype(BF16), perm_ref[...]).astype(BF16)
        w2_ref[...] = w2f_ref[...].astype(BF16)

    def step(cur):
        nxt = 1 - cur
        asg_copy(i, cur).start()
        tok_copy(0, nxt).wait()
        gather_issue(nxt)
        tok_copy(jnp.minimum(i + 2, n_blocks - 1), cur).start()

        @pl.when(i >= 1)
        def _():
            asg_ready(i - 1, nxt)
            scatter_wait(cur)

        pl.when(n_used > 0)(functools.partial(scatter_issue, nxt))
        gather_wait(cur)
        xbuf, ybuf = xbufs[cur], ybufs[cur]
        xs = xbuf[...].astype(BF16)
        h = _dot(xs, w1_ref[...]) + b1_ref[...]
        acts = []
        for c in range(h.shape[1] // (2 * LANES)):
            glu = jnp.minimum(h[:, 2 * c * LANES:(2 * c + 1) * LANES], SWIGLU_LIMIT)
            lin = jnp.clip(h[:, (2 * c + 1) * LANES:(2 * c + 2) * LANES], -SWIGLU_LIMIT, SWIGLU_LIMIT)
            acts.append((glu * jax.nn.sigmoid(SWIGLU_ALPHA * glu) * (lin + 1.0)).astype(BF16))
        out = _dot(jnp.concatenate(acts, axis=1), w2_ref[...]) + b2_ref[...]
        ybuf[...] = out

        @pl.when(i == n_used - 1)
        def _():
            asg_ready(i, cur)
            scatter_issue(cur)
            scatter_wait(nxt)
            scatter_wait(cur)
            gather_wait(nxt)
            tok_copy(0, cur).wait()

    for parity in range(2):
        pl.when((i < n_used) & (i % 2 == parity))(functools.partial(step, parity))


def _final_kernel(h_ref, *rest):
    y_refs, (gate_ref, g_ref, wg_ref, p_ref, wp_ref, o_ref) = rest[:TOP_K], rest[TOP_K:]
    gates = gate_ref[...]
    moe = sum(gates[:, k:k + 1] * y_ref[:, 0, :] for k, y_ref in enumerate(y_refs))
    h = h_ref[...] + moe
    n = (h * lax.rsqrt(jnp.mean(h * h, axis=-1, keepdims=True) + NORM_EPS) * g_ref[...]).astype(BF16)
    gate = jax.nn.sigmoid(_dot(n, wg_ref[...]))
    o_ref[...] = h + gate * _dot(p_ref[...].astype(BF16), wp_ref[...])


def _t5_bucket(rel):
    n = jnp.maximum(rel, 0)
    max_exact = NUM_BUCKETS // 2
    nf = jnp.maximum(n, 1).astype(F32)
    large = max_exact + (jnp.log(nf / max_exact) / math.log(MAX_DISTANCE / max_exact)
                         * (NUM_BUCKETS - max_exact)).astype(jnp.int32)
    large = jnp.minimum(large, NUM_BUCKETS - 1)
    return jnp.where(n < max_exact, n, large)


def _full(shape):
    return pl.BlockSpec(shape, lambda *_: (0,) * len(shape))


def _layer(i, h_in, p_i, rel_bias, prm):
    bsz, seq, d = h_in.shape
    n_tok = bsz * seq
    ns = seq // TM
    nk = seq // TK
    x2 = h_in.reshape(n_tok, d)

    w_in = prm['w_in']
    fox_w = N_FOX_HEADS * HEAD_DIM
    offs = [0, fox_w, 2 * fox_w, 3 * fox_w, 3 * fox_w + N_FOX_HEADS]
    offs += [offs[4] + GROUP, offs[4] + 2 * GROUP, offs[4] + 3 * GROUP]
    col = lambda a, b: w_in[:, a:b].astype(BF16)
    wfqT, wfk, wfvT = col(offs[0], offs[1]).T, col(offs[1], offs[2]), col(offs[2], offs[3]).T
    wdqT, wdk, wdvT = col(offs[4], offs[5]).T, col(offs[5], offs[6]), col(offs[6], offs[7]).T
    wf = jnp.zeros((d, LANES), BF16).at[:, :N_FOX_HEADS].set(col(offs[3], offs[4]))
    bf = jnp.zeros((1, LANES), F32).at[0, :N_FOX_HEADS].set(prm['b_f'])
    scale = HEAD_DIM ** -0.5 * LOG2E
    rep = GROUP // HEAD_DIM
    heads = jnp.arange(N_FOX_HEADS)
    place = jnp.concatenate([jnp.zeros((LANES, GROUP), BF16)
                             .at[heads, (heads // 2) * LANES + (heads % 2) * N_SPLIT + t].set(1.0)
                             for t in range(N_SPLIT)])
    gfqT = jnp.broadcast_to((jnp.tile(prm['g_fox_q'], rep) * scale)[:, None], (GROUP, TM))
    gdqT = jnp.broadcast_to((jnp.tile(prm['g_diff_q'], rep) * scale)[:, None], (GROUP, TM))
    gfk = jnp.tile(prm['g_fox_k'], rep)[None, :]
    gdk = jnp.tile(prm['g_diff_k'], rep)[None, :]
    gi = jnp.arange(GROUP // 2) // HEAD_DIM
    seg = jnp.where(gi[:, None] == gi[None, :], 1.0 / HEAD_DIM, 0.0).astype(BF16)
    ti = jnp.arange(TM)
    tri_incl = (ti[None, :] <= ti[:, None]).astype(BF16)

    w_spec_t = _full((GROUP, d))
    w_spec = _full((d, GROUP))
    qT_shape = jax.ShapeDtypeStruct((bsz, GROUP, seq), BF16)
    k_shape = jax.ShapeDtypeStruct((n_tok, GROUP), BF16)
    vT_shape = jax.ShapeDtypeStruct((bsz, GROUP // LANES, nk, LANES, TK), BF16)
    qT_spec = pl.BlockSpec((None, GROUP, TM), lambda b, s: (b, 0, s))
    k_spec = pl.BlockSpec((TM, GROUP), lambda b, s: (b * ns + s, 0))
    vT_spec = pl.BlockSpec((None, GROUP // LANES, TM // TK, LANES, TK), lambda b, s: (b, 0, s, 0, 0))
    fk_shape = jax.ShapeDtypeStruct((n_tok, 2 * GROUP), BF16)
    fk_spec = pl.BlockSpec((TM, 2 * GROUP), lambda b, s: (b * ns + s, 0))
    fqT, fk, fvT, dqT, dk, dvT = pl.pallas_call(
        _inproj_kernel,
        grid=(bsz, ns),
        in_specs=[pl.BlockSpec((TM, d), lambda b, s: (b * ns + s, 0)), _full((1, d)),
                  w_spec_t, w_spec, w_spec_t, w_spec_t, w_spec, w_spec_t, _full((d, LANES)), _full((1, LANES)),
                  _full((GROUP, TM)), _full((1, GROUP)), _full((GROUP, TM)), _full((1, GROUP)),
                  _full((GROUP // 2, GROUP // 2)), _full((TM, TM)), _full((N_SPLIT * LANES, GROUP))],
        out_specs=[qT_spec, fk_spec, vT_spec, qT_spec, k_spec, vT_spec],
        out_shape=[qT_shape, fk_shape, vT_shape, qT_shape, k_shape, vT_shape],
        scratch_shapes=[pltpu.VMEM((1, LANES), F32)],
        compiler_params=_params(2),
        name="inproj",
    )(x2, prm['g_attn'][None, :], wfqT, wfk, wfvT, wdqT, wdk, wdvT, wf, bf, gfqT, gfk, gdqT, gdk, seg, tri_incl,
      place)

    n_hb = GROUP // LANES
    att_q = pl.BlockSpec((None, LANES, seq), lambda b, h: (b, h, 0))
    att_k = pl.BlockSpec((seq, LANES), lambda b, h: (b, h))
    att_v = pl.BlockSpec((None, None, nk, LANES, TK), lambda b, h: (b, h, 0, 0, 0))
    att_o = pl.BlockSpec((seq, LANES), lambda b, h: (b, h))
    fox_out = pl.pallas_call(
        _fox_kernel,
        grid=(bsz, n_hb),
        in_specs=[att_q, pl.BlockSpec((seq, 2 * LANES), lambda b, h: (b, h)), att_v],
        out_specs=att_o,
        out_shape=jax.ShapeDtypeStruct((n_tok, GROUP), BF16),
        compiler_params=_params(2),
        name="fox_attention",
    )(fqT, fk, fvT)

    kpos = jnp.arange(TK)[:, None]
    qpos = jnp.arange(TQ)[None, :]
    rel = jnp.stack([qpos - kpos, qpos - kpos + TK])
    assert TK + 1 >= MAX_DISTANCE and TQ == TK
    table = (rel_bias.astype(F32) - rel_bias[NUM_BUCKETS - 1].astype(F32)[None, :]) * LOG2E
    onehot = _t5_bucket(rel)[None, ..., None] == jnp.arange(NUM_BUCKETS, dtype=jnp.int32)
    biasT = jnp.sum(jnp.where(onehot, table.T[:, None, None, None, :], 0.0), axis=-1)
    biasT = jnp.where((rel >= 0)[None], biasT, NEG_INF)
    lam_init = 0.8 - 0.6 * math.exp(-0.3 * i)
    lam_p = jnp.stack([prm['lambda_q1'], prm['lambda_k1'], prm['lambda_q2'], prm['lambda_k2']]).astype(F32)
    gsubT = jnp.broadcast_to((prm['g_subln'] * (1.0 - lam_init))[:, None], (LANES, TQ))
    diff_out = pl.pallas_call(
        functools.partial(_diff_kernel, lam_init),
        grid=(bsz, N_DIFF_HEADS),
        in_specs=[att_q, att_k, att_v, pl.BlockSpec((None, 2, TK, TQ), lambda b, h: (h, 0, 0, 0)),
                  _full((4, HEAD_DIM)), _full((LANES, TQ))],
        out_specs=att_o,
        out_shape=jax.ShapeDtypeStruct((n_tok, GROUP), BF16),
        compiler_params=_params(2),
        name="diff_attention",
    )(dqT, dk, dvT, biasT, lam_p, gsubT)

    w_out = prm['w_out'].astype(BF16)
    wr = jnp.zeros((d, LANES), F32).at[:, :N_EXPERTS].set(prm['w_router'])
    wr_hi = wr.astype(BF16)
    wr_lo = (wr - wr_hi.astype(F32)).astype(BF16)
    br = jnp.full((1, LANES), NEG_INF, F32).at[0, :N_EXPERTS].set(prm['b_router'])
    row_spec = lambda w: pl.BlockSpec((TM, w), lambda t: (t, 0))
    h1, xm, topi, gates, counts = pl.pallas_call(
        _mix_kernel,
        grid=(n_tok // TM,),
        in_specs=[row_spec(d), row_spec(GROUP), row_spec(GROUP), _full((GROUP, d)), _full((GROUP, d)),
                  _full((1, d)), _full((d, LANES)), _full((d, LANES)), _full((1, LANES))],
        out_specs=[row_spec(d), pl.BlockSpec((TM, 1, d), lambda t: (t, 0, 0)), row_spec(LANES),
                   row_spec(LANES), _full((1, LANES))],
        out_shape=[jax.ShapeDtypeStruct((n_tok, d), F32), jax.ShapeDtypeStruct((n_tok, 1, d), F32),
                   jax.ShapeDtypeStruct((n_tok, LANES), jnp.int32), jax.ShapeDtypeStruct((n_tok, LANES), F32),
                   jax.ShapeDtypeStruct((1, LANES), F32)],
        scratch_shapes=[pltpu.VMEM((1, LANES), F32)],
        compiler_params=_params(1),
        name="mix_router",
    )(x2, fox_out, diff_out, w_out[:GROUP], w_out[GROUP:], prm['g_mlp'][None, :], wr_hi, wr_lo, br)

    m = n_tok * TOP_K
    n_blocks = (m + N_EXPERTS * MOE_BLOCK + MOE_BLOCK - 1) // MOE_BLOCK
    p_rows = n_blocks * MOE_BLOCK
    cnt = counts[0, :N_EXPERTS].astype(jnp.int32)
    padded = ((cnt + MOE_BLOCK - 1) // MOE_BLOCK) * MOE_BLOCK
    pad_ends = jnp.cumsum(padded)
    pad_starts = pad_ends - padded
    block_start = jnp.arange(n_blocks, dtype=jnp.int32) * MOE_BLOCK
    block_expert = jnp.minimum(jnp.sum(pad_ends[None, :] <= block_start[:, None], axis=1),
                               N_EXPERTS - 1).astype(jnp.int32)
    n_used = (pad_ends[-1] // MOE_BLOCK).astype(jnp.int32)[None]
    experts = jnp.arange(N_EXPERTS, dtype=jnp.int32)
    real = topi[:, :TOP_K] * (2 * m) + jnp.arange(m, dtype=jnp.int32).reshape(n_tok, TOP_K)
    fill = jnp.arange(IDX_ALIGN - 1, dtype=jnp.int32)[None, :]
    need = (-cnt) % IDX_ALIGN
    filler = jnp.where(fill < need[:, None], experts[:, None] * (2 * m) + m + fill,
                       N_EXPERTS * (2 * m) + experts[:, None] * IDX_ALIGN + fill)
    keys = jnp.sort(jnp.concatenate([real.reshape(m), filler.reshape(-1)]))
    flat = jnp.concatenate([(keys % (2 * m)) % m, jnp.zeros((MOE_BLOCK,), jnp.int32)])
    grouped_tok = flat // TOP_K
    grouped_row = (flat % TOP_K) * n_tok + grouped_tok
    onehot_e = block_expert[:, None] == experts[None, :]
    pick = lambda v: jnp.sum(jnp.where(onehot_e, v[None, :], 0), axis=1)
    in_group = block_start - pick(pad_starts)
    n_valid = jnp.clip(pick(cnt) - in_group, 0, MOE_BLOCK)
    group_start = jnp.cumsum(cnt + need) - (cnt + need)
    first = jnp.clip(pick(group_start) + in_group, 0, keys.shape[0]) // IDX_ALIGN * IDX_ALIGN

    d_ff = prm['w2'].shape[1]
    n_grp = 2 * d_ff // (2 * LANES)
    b1p = prm['b1'].reshape(N_EXPERTS, n_grp, LANES, 2).transpose(0, 1, 3, 2).reshape(N_EXPERTS, 1, 2 * d_ff)
    b2 = prm['b2'].reshape(N_EXPERTS, 1, d)
    pj = jnp.arange(LANES)
    perm = (jnp.zeros((2 * LANES, 2 * LANES), BF16).at[2 * pj, pj].set(1.0)
            .at[2 * pj + 1, LANES + pj].set(1.0))
    any_spec = pl.BlockSpec(memory_space=pl.ANY)
    ys = pl.pallas_call(
        _expert_kernel,
        grid_spec=pltpu.PrefetchScalarGridSpec(
            num_scalar_prefetch=4,
            grid=(n_blocks,),
            in_specs=[any_spec, any_spec, any_spec,
                      pl.BlockSpec((None, d, 2 * d_ff), lambda t, be, *_:(be[t], 0, 0)),
                      pl.BlockSpec((None, 1, 2 * d_ff), lambda t, be, *_:(be[t], 0, 0)),
                      pl.BlockSpec((None, d_ff, d), lambda t, be, *_:(be[t], 0, 0)),
                      pl.BlockSpec((None, 1, d), lambda t, be, *_:(be[t], 0, 0)),
                      pl.BlockSpec((2 * LANES, 2 * LANES), lambda t, be, *_:(0, 0))],
            out_specs=any_spec,
            scratch_shapes=[pltpu.SMEM((2, MOE_BLOCK), jnp.int32), pltpu.SMEM((2, MOE_BLOCK), jnp.int32)]
            + [pltpu.VMEM((MOE_BLOCK, d), F32)] * 4
            + [pltpu.VMEM((d, 2 * d_ff), BF16), pltpu.VMEM((d_ff, d), BF16)]
            + [pltpu.SemaphoreType.DMA((2, 2)), pltpu.SemaphoreType.DMA((2,)), pltpu.SemaphoreType.DMA((2,))],
        ),
        out_shape=jax.ShapeDtypeStruct((m + MOE_BLOCK, 1, d), F32),
        compiler_params=pltpu.CompilerParams(dimension_semantics=("arbitrary",),
                                             vmem_limit_bytes=EXPERT_VMEM_LIMIT),
        name="experts",
    )(block_expert, n_used, first, n_valid, grouped_tok, grouped_row, xm, prm['w1'], b1p, prm['w2'], b2, perm)

    ple = p_i.shape[-1]
    out = pl.pallas_call(
        _final_kernel,
        grid=(n_tok // TM,),
        in_specs=[row_spec(d)]
        + [pl.BlockSpec((TM, 1, d), functools.partial(lambda k, t: (k * (n_tok // TM) + t, 0, 0), k))
           for k in range(TOP_K)]
        + [row_spec(LANES),
                  _full((1, d)), _full((d, d)), row_spec(ple), _full((ple, d))],
        out_specs=row_spec(d),
        out_shape=jax.ShapeDtypeStruct((n_tok, d), F32),
        compiler_params=_params(1),
        name="combine_ple",
    )(h1, ys, ys, ys, ys, gates, prm['g_ple'][None, :],
      prm['w_ple_gate'].astype(BF16), p_i.reshape(n_tok, ple), prm['w_ple_proj'].astype(BF16))
    return out.reshape(bsz, seq, d)


def kernel(x, p, rel_bias, g_attn, w_in, b_f, g_fox_q, g_fox_k, g_diff_q, g_diff_k, lambda_q1, lambda_k1,
           lambda_q2, lambda_k2, g_subln, w_out, g_mlp, w_router, b_router, w1, b1, w2, b2, g_ple,
           w_ple_gate, w_ple_proj):
    stacked = dict(g_attn=g_attn, w_in=w_in, b_f=b_f, g_fox_q=g_fox_q, g_fox_k=g_fox_k, g_diff_q=g_diff_q,
                   g_diff_k=g_diff_k, lambda_q1=lambda_q1, lambda_k1=lambda_k1, lambda_q2=lambda_q2,
                   lambda_k2=lambda_k2, g_subln=g_subln, w_out=w_out, g_mlp=g_mlp, w_router=w_router,
                   b_router=b_router, w1=w1, b1=b1, w2=w2, b2=b2, g_ple=g_ple, w_ple_gate=w_ple_gate,
                   w_ple_proj=w_ple_proj)
    h = x
    for i in range(p.shape[0]):
        h = _layer(i, h, p[i], rel_bias, {name: v[i] for name, v in stacked.items()})
    return h
```

```python
import functools
import math

import jax
import jax.numpy as jnp
from jax import lax
from jax.experimental import pallas as pl
from jax.experimental.pallas import tpu as pltpu

F32 = jnp.float32
BF16 = jnp.bfloat16

HEAD_DIM = 64
N_FOX_HEADS = 8
N_DIFF_HEADS = 4
GROUP = 512
LANES = 128
NUM_BUCKETS = 32
MAX_DISTANCE = 128
N_EXPERTS = 32
TOP_K = 4
SWIGLU_ALPHA = 1.702
SWIGLU_LIMIT = 7.0
MOE_BLOCK = 256
IDX_ALIGN = 128
NORM_EPS = 1e-6
NEG_INF = -1e30
LOG2E = math.log2(math.e)
N_SPLIT = 3

TM = 512
TQ = 256
TK = 256
LOOKAHEAD = 2
VMEM_LIMIT = 48 * 1024 * 1024
EXPERT_VMEM_LIMIT = 58 * 1024 * 1024


def _params(n_axes):
    return pltpu.CompilerParams(dimension_semantics=("arbitrary",) * n_axes,
                                vmem_limit_bytes=VMEM_LIMIT)


def _dot(a, b):
    return jnp.dot(a, b, preferred_element_type=F32)


def _dot_nt(a, b):
    return lax.dot_general(a, b, (((1,), (1,)), ((), ())), preferred_element_type=F32)


def _split3(v):
    hi = v.astype(BF16)
    r = v - hi.astype(F32)
    mid = r.astype(BF16)
    lo = (r - mid.astype(F32)).astype(BF16)
    return hi, mid, lo


def _inproj_kernel(x_ref, g_ref, wfq_ref, wfk_ref, wfv_ref, wdq_ref, wdk_ref, wdv_ref, wf_ref, bf_ref,
                   gfq_ref, gfk_ref, gdq_ref, gdk_ref, seg_ref, tri_ref, place_ref,
                   fq_ref, fk_ref, fv_ref, dq_ref, dk_ref, dv_ref, carry_ref):
    @pl.when(pl.program_id(1) == 0)
    def _():
        carry_ref[...] = jnp.zeros_like(carry_ref)

    xf = x_ref[...]
    a = (xf * lax.rsqrt(jnp.mean(xf * xf, axis=-1, keepdims=True) + NORM_EPS) * g_ref[...]).astype(BF16)
    seg = seg_ref[...]

    fl = _dot(a, wf_ref[...]) + bf_ref[...]
    fq_acc = _dot_nt(wfq_ref[...], a)
    dq_acc = _dot_nt(wdq_ref[...], a)
    fk_acc = _dot(a, wfk_ref[...])
    dk_acc = _dot(a, wdk_ref[...])
    fv_acc = _dot_nt(wfv_ref[...], a)
    dv_acc = _dot_nt(wdv_ref[...], a)

    half = seg.shape[0]

    def q_norm(acc, gain_ref):
        sq = (acc * acc).astype(BF16)
        ms = jnp.concatenate([_dot(seg, sq[:half]), _dot(seg, sq[half:])], axis=0)
        return (acc * lax.rsqrt(ms + NORM_EPS) * gain_ref[...]).astype(BF16)

    def k_norm(acc, gain_ref):
        sq = (acc * acc).astype(BF16)
        ms = jnp.concatenate([_dot(sq[:, :half], seg), _dot(sq[:, half:], seg)], axis=1)
        return (acc * lax.rsqrt(ms + NORM_EPS) * gain_ref[...]).astype(BF16)

    def v_store(acc, out_ref):
        acc = acc.astype(BF16)
        for hb in range(GROUP // LANES):
            for j in range(TM // TK):
                out_ref[hb, j] = acc[hb * LANES:(hb + 1) * LANES, j * TK:(j + 1) * TK]

    logf = jnp.minimum(fl, 0.0) - jnp.log1p(jnp.exp(-jnp.abs(fl)))
    sums = _dot(tri_ref[...], jnp.concatenate(_split3(logf), axis=1))
    cs = carry_ref[...] + sums[:, :LANES] + sums[:, LANES:2 * LANES] + sums[:, 2 * LANES:]
    carry_ref[...] = cs[TM - 1:TM, :]

    fq_ref[...] = q_norm(fq_acc, gfq_ref)
    dq_ref[...] = q_norm(dq_acc, gdq_ref)
    fk = k_norm(fk_acc, gfk_ref)
    dk_ref[...] = k_norm(dk_acc, gdk_ref)
    v_store(fv_acc, fv_ref)
    v_store(dv_acc, dv_ref)

    extra = _dot(jnp.concatenate(_split3(cs * LOG2E), axis=1), place_ref[...]).astype(BF16)
    for hb in range(GROUP // LANES):
        fk_ref[:, 2 * hb * LANES:(2 * hb + 1) * LANES] = fk[:, hb * LANES:(hb + 1) * LANES]
        fk_ref[:, (2 * hb + 1) * LANES:(2 * hb + 2) * LANES] = extra[:, hb * LANES:(hb + 1) * LANES]


def _two_map_queries(qT_blk):
    row = lax.broadcasted_iota(jnp.int32, qT_blk.shape, 0)
    zero = jnp.zeros_like(qT_blk)
    return jnp.concatenate([jnp.where(row < HEAD_DIM, qT_blk, zero),
                            jnp.where(row >= HEAD_DIM, qT_blk, zero)], axis=1)


def _softmax_step(sT, vT_blk, carry):
    m, l, acc = carry
    m_new = jnp.maximum(m, jnp.max(sT, axis=0, keepdims=True))
    alpha = jnp.exp2(m - m_new)
    pT = jnp.exp2(sT - m_new)
    l = alpha * l + jnp.sum(pT, axis=0, keepdims=True)
    acc = alpha * acc + _dot(vT_blk, pT.astype(BF16))
    return m_new, l, acc


def _causal_keep():
    key = lax.broadcasted_iota(jnp.int32, (TK, 2 * TQ), 0)
    col = lax.broadcasted_iota(jnp.int32, (TK, 2 * TQ), 1)
    qry = jnp.where(col >= TQ, col - TQ, col)
    return key <= qry


def _init_carry():
    return (jnp.full((1, 2 * TQ), NEG_INF, F32), jnp.zeros((1, 2 * TQ), F32),
            jnp.zeros((LANES, 2 * TQ), F32))


def _causal_sweep(seq, k_ref, vT_ref, queries, adjust, finish):
    pairs = [(qi, kj) for qi in range(seq // TQ) for kj in range(qi + 1)]
    q_ops = {}

    def score(qi, kj):
        if qi not in q_ops:
            q_ops[qi] = queries(qi)
        return _dot(k_ref[kj * TK:(kj + 1) * TK, :], q_ops[qi])

    pending = [score(*pr) for pr in pairs[:LOOKAHEAD]]
    carry = None
    for n, (qi, kj) in enumerate(pairs):
        if n + LOOKAHEAD < len(pairs):
            pending.append(score(*pairs[n + LOOKAHEAD]))
        if kj == 0:
            carry = _init_carry()
        carry = _softmax_step(adjust(pending.pop(0), qi, kj), vT_ref[kj], carry)
        if kj == qi:
            m, l, acc = carry
            finish(qi, acc * (1.0 / l))


def _fox_kernel(qT_ref, k_ref, vT_ref, o_ref):
    seq = k_ref.shape[0]
    keep = _causal_keep()
    row = lax.broadcasted_iota(jnp.int32, (LANES, TQ), 0)
    arow = lax.broadcasted_iota(jnp.int32, (LANES, 2 * TQ), 0)
    acol = lax.broadcasted_iota(jnp.int32, (LANES, 2 * TQ), 1)
    first = (acol < TQ) & (arow < N_SPLIT)
    second = (acol >= TQ) & (arow >= N_SPLIT) & (arow < 2 * N_SPLIT)
    minus_c = jnp.where(first | second, -1.0, 0.0).astype(BF16)

    def queries(qi):
        return jnp.concatenate([_two_map_queries(qT_ref[:, qi * TQ:(qi + 1) * TQ]), minus_c], axis=0)

    def adjust(sT, qi, kj):
        return jnp.where(keep, sT, NEG_INF) if kj == qi else sT

    def finish(qi, oT):
        o = jnp.where(row < HEAD_DIM, oT[:, :TQ], oT[:, TQ:])
        o_ref[qi * TQ:(qi + 1) * TQ, :] = o.T.astype(BF16)

    _causal_sweep(seq, k_ref, vT_ref, queries, adjust, finish)


def _diff_kernel(lam_init, qT_ref, k_ref, vT_ref, bias_ref, lam_ref, gsub_ref, o_ref):
    seq = k_ref.shape[0]
    lp = lam_ref[...]
    lam = (jnp.exp(jnp.sum(lp[0:1] * lp[1:2], axis=1, keepdims=True))
           - jnp.exp(jnp.sum(lp[2:3] * lp[3:4], axis=1, keepdims=True)) + lam_init)

    def queries(qi):
        return _two_map_queries(qT_ref[:, qi * TQ:(qi + 1) * TQ])

    def adjust(sT, qi, kj):
        if kj >= qi - 1:
            b = bias_ref[qi - kj]
            sT = jnp.concatenate([sT[:, :TQ] + b, sT[:, TQ:] + b], axis=1)
        return sT

    def finish(qi, oT):
        o = oT[:, :TQ] - lam * oT[:, TQ:]
        y = o * lax.rsqrt(jnp.mean(o * o, axis=0, keepdims=True) + NORM_EPS) * gsub_ref[...]
        o_ref[qi * TQ:(qi + 1) * TQ, :] = y.T.astype(BF16)

    _causal_sweep(seq, k_ref, vT_ref, queries, adjust, finish)


def _mix_kernel(x_ref, fox_ref, dif_ref, wa_ref, wb_ref, g_ref, wr_hi_ref, wr_lo_ref, br_ref,
                h_ref, xm_ref, topi_ref, gate_ref, cnt_ref, carry_ref):
    @pl.when(pl.program_id(0) == 0)
    def _():
        carry_ref[...] = jnp.zeros_like(carry_ref)

    h = x_ref[...] + _dot(fox_ref[...], wa_ref[...]) + _dot(dif_ref[...], wb_ref[...])
    h_ref[...] = h
    xm = h * lax.rsqrt(jnp.mean(h * h, axis=-1, keepdims=True) + NORM_EPS) * g_ref[...]
    xm_ref[:, 0, :] = xm

    x_hi = xm.astype(BF16)
    x_lo = (xm - x_hi.astype(F32)).astype(BF16)
    logits = (_dot(x_hi, wr_hi_ref[...]) + _dot(x_hi, wr_lo_ref[...]) + _dot(x_lo, wr_hi_ref[...])
              + br_ref[...])

    lane = lax.broadcasted_iota(jnp.int32, logits.shape, 1)
    vals, sels, idxs = [], [], []
    for _ in range(TOP_K):
        mx = jnp.max(logits, axis=1, keepdims=True)
        idx = jnp.min(jnp.where(logits == mx, lane, LANES), axis=1, keepdims=True)
        sel = lane == idx
        logits = jnp.where(sel, -jnp.inf, logits)
        vals.append(mx)
        sels.append(sel)
        idxs.append(idx)
    exps = [jnp.exp(v - vals[0]) for v in vals]
    denom = exps[0] + exps[1] + exps[2] + exps[3]

    multi_hot = sum(s.astype(F32) for s in sels)
    topi = jnp.zeros(logits.shape, jnp.int32)
    gate = jnp.zeros(logits.shape, F32)
    for k in range(TOP_K):
        topi = jnp.where(lane == k, idxs[k], topi)
        gate = jnp.where(lane == k, exps[k] / denom, gate)
    topi_ref[...] = topi
    gate_ref[...] = gate
    carry_ref[...] = carry_ref[...] + jnp.sum(multi_hot, axis=0, keepdims=True)
    cnt_ref[...] = carry_ref[...]


def _expert_kernel(be_ref, nu_ref, off_ref, nv_ref, tok_hbm, asg_hbm, x_hbm, w1f_ref, b1_ref, w2f_ref, b2_ref,
                   perm_ref, y_hbm, tok_ref, asg_ref, x0, x1, y0, y1, w1_ref, w2_ref, idx_sem, g_sem, s_sem):
    i = pl.program_id(0)
    n_used = nu_ref[0]
    n_blocks = be_ref.shape[0]
    spare_row = y_hbm.shape[0] - MOE_BLOCK
    xbufs, ybufs = (x0, x1), (y0, y1)

    def tok_copy(blk, slot):
        start = pl.multiple_of(off_ref[blk], IDX_ALIGN)
        return pltpu.make_async_copy(tok_hbm.at[pl.ds(start, MOE_BLOCK)], tok_ref.at[slot], idx_sem.at[0, slot])

    def asg_copy(blk, slot):
        start = pl.multiple_of(off_ref[blk], IDX_ALIGN)
        return pltpu.make_async_copy(asg_hbm.at[pl.ds(start, MOE_BLOCK)], asg_ref.at[slot], idx_sem.at[1, slot])

    def asg_ready(blk, slot):
        asg_copy(0, slot).wait()

        def fix(r, carry):
            asg_ref[slot, r] = spare_row + r
            return carry
        lax.fori_loop(nv_ref[blk], MOE_BLOCK, fix, 0)

    def row_loop(fn):
        for r in range(MOE_BLOCK):
            fn(r)

    def gather_issue(slot):
        row_loop(lambda r: pltpu.make_async_copy(x_hbm.at[tok_ref[slot, r]], xbufs[slot].at[pl.ds(r, 1)],
                                                 g_sem.at[slot]).start())

    def gather_wait(slot):
        pltpu.make_async_copy(xbufs[slot], xbufs[slot], g_sem.at[slot]).wait()

    def scatter_issue(slot):
        row_loop(lambda r: pltpu.make_async_copy(ybufs[slot].at[pl.ds(r, 1)], y_hbm.at[asg_ref[slot, r]],
                                                 s_sem.at[slot]).start())

    def scatter_wait(slot):
        pltpu.make_async_copy(ybufs[slot], ybufs[slot], s_sem.at[slot]).wait()

    @pl.when(i == 0)
    def _():
        y1[...] = jnp.zeros(y1.shape, y1.dtype)
        for r in range(MOE_BLOCK):
            asg_ref[1, r] = spare_row + r
        tok_copy(0, 0).start()
        tok_copy(0, 0).wait()
        tok_copy(jnp.minimum(1, n_blocks - 1), 1).start()
        gather_issue(0)

    @pl.when((i < n_used) & ((i == 0) | (be_ref[i] != be_ref[jnp.maximum(i - 1, 0)])))
    def _():
        for g in range(w1f_ref.shape[1] // (2 * LANES)):
            cols = slice(2 * g * LANES, 2 * (g + 1) * LANES)
            w1_ref[:, cols] = _dot(w1f_ref[:, cols].astype(BF16), perm_ref[...]).astype(BF16)
        w2_ref[...] = w2f_ref[...].astype(BF16)

    def step(cur):
        nxt = 1 - cur
        asg_copy(i, cur).start()
        tok_copy(0, nxt).wait()

        @pl.when(i >= 1)
        def _():
            asg_ready(i - 1, nxt)
            scatter_wait(cur)

        gather_wait(cur)
        gather_issue(nxt)
        scatter_issue(nxt)
        tok_copy(jnp.minimum(i + 2, n_blocks - 1), cur).start()
        xbuf, ybuf = xbufs[cur], ybufs[cur]
        xs = xbuf[...].astype(BF16)
        h = _dot(xs, w1_ref[...]) + b1_ref[...]
        acts = []
        for c in range(h.shape[1] // (2 * LANES)):
            glu = jnp.minimum(h[:, 2 * c * LANES:(2 * c + 1) * LANES], SWIGLU_LIMIT)
            lin = jnp.clip(h[:, (2 * c + 1) * LANES:(2 * c + 2) * LANES], -SWIGLU_LIMIT, SWIGLU_LIMIT)
            acts.append((glu * jax.nn.sigmoid(SWIGLU_ALPHA * glu) * (lin + 1.0)).astype(BF16))
        out = _dot(jnp.concatenate(acts, axis=1), w2_ref[...]) + b2_ref[...]
        ybuf[...] = out

        @pl.when(i == n_used - 1)
        def _():
            asg_ready(i, cur)
            scatter_issue(cur)
            scatter_wait(nxt)
            scatter_wait(cur)
            gather_wait(nxt)
            tok_copy(0, cur).wait()

    for parity in range(2):
        pl.when((i < n_used) & (i % 2 == parity))(functools.partial(step, parity))


def _final_kernel(h_ref, *rest):
    y_refs, (gate_ref, g_ref, wg_ref, p_ref, wp_ref, o_ref) = rest[:TOP_K], rest[TOP_K:]
    gates = gate_ref[...]
    moe = sum(gates[:, k:k + 1] * y_ref[:, 0, :] for k, y_ref in enumerate(y_refs))
    h = h_ref[...] + moe
    n = (h * lax.rsqrt(jnp.mean(h * h, axis=-1, keepdims=True) + NORM_EPS) * g_ref[...]).astype(BF16)
    gate = jax.nn.sigmoid(_dot(n, wg_ref[...]))
    o_ref[...] = h + gate * _dot(p_ref[...].astype(BF16), wp_ref[...])


def _t5_bucket(rel):
    n = jnp.maximum(rel, 0)
    max_exact = NUM_BUCKETS // 2
    nf = jnp.maximum(n, 1).astype(F32)
    large = max_exact + (jnp.log(nf / max_exact) / math.log(MAX_DISTANCE / max_exact)
                         * (NUM_BUCKETS - max_exact)).astype(jnp.int32)
    large = jnp.minimum(large, NUM_BUCKETS - 1)
    return jnp.where(n < max_exact, n, large)


def _full(shape):
    return pl.BlockSpec(shape, lambda *_: (0,) * len(shape))


def _layer(i, h_in, p_i, rel_bias, prm):
    bsz, seq, d = h_in.shape
    n_tok = bsz * seq
    ns = seq // TM
    nk = seq // TK
    x2 = h_in.reshape(n_tok, d)

    w_in = prm['w_in']
    fox_w = N_FOX_HEADS * HEAD_DIM
    offs = [0, fox_w, 2 * fox_w, 3 * fox_w, 3 * fox_w + N_FOX_HEADS]
    offs += [offs[4] + GROUP, offs[4] + 2 * GROUP, offs[4] + 3 * GROUP]
    col = lambda a, b: w_in[:, a:b].astype(BF16)
    wfqT, wfk, wfvT = col(offs[0], offs[1]).T, col(offs[1], offs[2]), col(offs[2], offs[3]).T
    wdqT, wdk, wdvT = col(offs[4], offs[5]).T, col(offs[5], offs[6]), col(offs[6], offs[7]).T
    wf = jnp.zeros((d, LANES), BF16).at[:, :N_FOX_HEADS].set(col(offs[3], offs[4]))
    bf = jnp.zeros((1, LANES), F32).at[0, :N_FOX_HEADS].set(prm['b_f'])
    scale = HEAD_DIM ** -0.5 * LOG2E
    rep = GROUP // HEAD_DIM
    heads = jnp.arange(N_FOX_HEADS)
    place = jnp.concatenate([jnp.zeros((LANES, GROUP), BF16)
                             .at[heads, (heads // 2) * LANES + (heads % 2) * N_SPLIT + t].set(1.0)
                             for t in range(N_SPLIT)])
    gfqT = jnp.broadcast_to((jnp.tile(prm['g_fox_q'], rep) * scale)[:, None], (GROUP, TM))
    gdqT = jnp.broadcast_to((jnp.tile(prm['g_diff_q'], rep) * scale)[:, None], (GROUP, TM))
    gfk = jnp.tile(prm['g_fox_k'], rep)[None, :]
    gdk = jnp.tile(prm['g_diff_k'], rep)[None, :]
    gi = jnp.arange(GROUP // 2) // HEAD_DIM
    seg = jnp.where(gi[:, None] == gi[None, :], 1.0 / HEAD_DIM, 0.0).astype(BF16)
    ti = jnp.arange(TM)
    tri_incl = (ti[None, :] <= ti[:, None]).astype(BF16)

    w_spec_t = _full((GROUP, d))
    w_spec = _full((d, GROUP))
    qT_shape = jax.ShapeDtypeStruct((bsz, GROUP, seq), BF16)
    k_shape = jax.ShapeDtypeStruct((n_tok, GROUP), BF16)
    vT_shape = jax.ShapeDtypeStruct((bsz, GROUP // LANES, nk, LANES, TK), BF16)
    qT_spec = pl.BlockSpec((None, GROUP, TM), lambda b, s: (b, 0, s))
    k_spec = pl.BlockSpec((TM, GROUP), lambda b, s: (b * ns + s, 0))
    vT_spec = pl.BlockSpec((None, GROUP // LANES, TM // TK, LANES, TK), lambda b, s: (b, 0, s, 0, 0))
    fk_shape = jax.ShapeDtypeStruct((n_tok, 2 * GROUP), BF16)
    fk_spec = pl.BlockSpec((TM, 2 * GROUP), lambda b, s: (b * ns + s, 0))
    fqT, fk, fvT, dqT, dk, dvT = pl.pallas_call(
        _inproj_kernel,
        grid=(bsz, ns),
        in_specs=[pl.BlockSpec((TM, d), lambda b, s: (b * ns + s, 0)), _full((1, d)),
                  w_spec_t, w_spec, w_spec_t, w_spec_t, w_spec, w_spec_t, _full((d, LANES)), _full((1, LANES)),
                  _full((GROUP, TM)), _full((1, GROUP)), _full((GROUP, TM)), _full((1, GROUP)),
                  _full((GROUP // 2, GROUP // 2)), _full((TM, TM)), _full((N_SPLIT * LANES, GROUP))],
        out_specs=[qT_spec, fk_spec, vT_spec, qT_spec, k_spec, vT_spec],
        out_shape=[qT_shape, fk_shape, vT_shape, qT_shape, k_shape, vT_shape],
        scratch_shapes=[pltpu.VMEM((1, LANES), F32)],
        compiler_params=_params(2),
        name="inproj",
    )(x2, prm['g_attn'][None, :], wfqT, wfk, wfvT, wdqT, wdk, wdvT, wf, bf, gfqT, gfk, gdqT, gdk, seg, tri_incl,
      place)

    n_hb = GROUP // LANES
    att_q = pl.BlockSpec((None, LANES, seq), lambda b, h: (b, h, 0))
    att_k = pl.BlockSpec((seq, LANES), lambda b, h: (b, h))
    att_v = pl.BlockSpec((None, None, nk, LANES, TK), lambda b, h: (b, h, 0, 0, 0))
    att_o = pl.BlockSpec((seq, LANES), lambda b, h: (b, h))
    fox_out = pl.pallas_call(
        _fox_kernel,
        grid=(bsz, n_hb),
        in_specs=[att_q, pl.BlockSpec((seq, 2 * LANES), lambda b, h: (b, h)), att_v],
        out_specs=att_o,
        out_shape=jax.ShapeDtypeStruct((n_tok, GROUP), BF16),
        compiler_params=_params(2),
        name="fox_attention",
    )(fqT, fk, fvT)

    kpos = jnp.arange(TK)[:, None]
    qpos = jnp.arange(TQ)[None, :]
    rel = jnp.stack([qpos - kpos, qpos - kpos + TK])
    assert TK + 1 >= MAX_DISTANCE and TQ == TK
    table = (rel_bias.astype(F32) - rel_bias[NUM_BUCKETS - 1].astype(F32)[None, :]) * LOG2E
    onehot = _t5_bucket(rel)[None, ..., None] == jnp.arange(NUM_BUCKETS, dtype=jnp.int32)
    biasT = jnp.sum(jnp.where(onehot, table.T[:, None, None, None, :], 0.0), axis=-1)
    biasT = jnp.where((rel >= 0)[None], biasT, NEG_INF)
    lam_init = 0.8 - 0.6 * math.exp(-0.3 * i)
    lam_p = jnp.stack([prm['lambda_q1'], prm['lambda_k1'], prm['lambda_q2'], prm['lambda_k2']]).astype(F32)
    gsubT = jnp.broadcast_to((prm['g_subln'] * (1.0 - lam_init))[:, None], (LANES, TQ))
    diff_out = pl.pallas_call(
        functools.partial(_diff_kernel, lam_init),
        grid=(bsz, N_DIFF_HEADS),
        in_specs=[att_q, att_k, att_v, pl.BlockSpec((None, 2, TK, TQ), lambda b, h: (h, 0, 0, 0)),
                  _full((4, HEAD_DIM)), _full((LANES, TQ))],
        out_specs=att_o,
        out_shape=jax.ShapeDtypeStruct((n_tok, GROUP), BF16),
        compiler_params=_params(2),
        name="diff_attention",
    )(dqT, dk, dvT, biasT, lam_p, gsubT)

    w_out = prm['w_out'].astype(BF16)
    wr = jnp.zeros((d, LANES), F32).at[:, :N_EXPERTS].set(prm['w_router'])
    wr_hi = wr.astype(BF16)
    wr_lo = (wr - wr_hi.astype(F32)).astype(BF16)
    br = jnp.full((1, LANES), NEG_INF, F32).at[0, :N_EXPERTS].set(prm['b_router'])
    row_spec = lambda w: pl.BlockSpec((TM, w), lambda t: (t, 0))
    h1, xm, topi, gates, counts = pl.pallas_call(
        _mix_kernel,
        grid=(n_tok // TM,),
        in_specs=[row_spec(d), row_spec(GROUP), row_spec(GROUP), _full((GROUP, d)), _full((GROUP, d)),
                  _full((1, d)), _full((d, LANES)), _full((d, LANES)), _full((1, LANES))],
        out_specs=[row_spec(d), pl.BlockSpec((TM, 1, d), lambda t: (t, 0, 0)), row_spec(LANES),
                   row_spec(LANES), _full((1, LANES))],
        out_shape=[jax.ShapeDtypeStruct((n_tok, d), F32), jax.ShapeDtypeStruct((n_tok, 1, d), F32),
                   jax.ShapeDtypeStruct((n_tok, LANES), jnp.int32), jax.ShapeDtypeStruct((n_tok, LANES), F32),
                   jax.ShapeDtypeStruct((1, LANES), F32)],
        scratch_shapes=[pltpu.VMEM((1, LANES), F32)],
        compiler_params=_params(1),
        name="mix_router",
    )(x2, fox_out, diff_out, w_out[:GROUP], w_out[GROUP:], prm['g_mlp'][None, :], wr_hi, wr_lo, br)

    m = n_tok * TOP_K
    n_blocks = (m + N_EXPERTS * MOE_BLOCK + MOE_BLOCK - 1) // MOE_BLOCK
    p_rows = n_blocks * MOE_BLOCK
    cnt = counts[0, :N_EXPERTS].astype(jnp.int32)
    padded = ((cnt + MOE_BLOCK - 1) // MOE_BLOCK) * MOE_BLOCK
    pad_ends = jnp.cumsum(padded)
    pad_starts = pad_ends - padded
    block_start = jnp.arange(n_blocks, dtype=jnp.int32) * MOE_BLOCK
    block_expert = jnp.minimum(jnp.sum(pad_ends[None, :] <= block_start[:, None], axis=1),
                               N_EXPERTS - 1).astype(jnp.int32)
    n_used = (pad_ends[-1] // MOE_BLOCK).astype(jnp.int32)[None]
    experts = jnp.arange(N_EXPERTS, dtype=jnp.int32)
    real = topi[:, :TOP_K] * (2 * m) + jnp.arange(m, dtype=jnp.int32).reshape(n_tok, TOP_K)
    fill = jnp.arange(IDX_ALIGN - 1, dtype=jnp.int32)[None, :]
    need = (-cnt) % IDX_ALIGN
    filler = jnp.where(fill < need[:, None], experts[:, None] * (2 * m) + m + fill,
                       N_EXPERTS * (2 * m) + experts[:, None] * IDX_ALIGN + fill)
    keys = jnp.sort(jnp.concatenate([real.reshape(m), filler.reshape(-1)]))
    flat = jnp.concatenate([(keys % (2 * m)) % m, jnp.zeros((MOE_BLOCK,), jnp.int32)])
    grouped_tok = flat // TOP_K
    grouped_row = (flat % TOP_K) * n_tok + grouped_tok
    onehot_e = block_expert[:, None] == experts[None, :]
    pick = lambda v: jnp.sum(jnp.where(onehot_e, v[None, :], 0), axis=1)
    in_group = block_start - pick(pad_starts)
    n_valid = jnp.clip(pick(cnt) - in_group, 0, MOE_BLOCK)
    group_start = jnp.cumsum(cnt + need) - (cnt + need)
    first = jnp.clip(pick(group_start) + in_group, 0, keys.shape[0]) // IDX_ALIGN * IDX_ALIGN

    d_ff = prm['w2'].shape[1]
    n_grp = 2 * d_ff // (2 * LANES)
    b1p = prm['b1'].reshape(N_EXPERTS, n_grp, LANES, 2).transpose(0, 1, 3, 2).reshape(N_EXPERTS, 1, 2 * d_ff)
    b2 = prm['b2'].reshape(N_EXPERTS, 1, d)
    pj = jnp.arange(LANES)
    perm = (jnp.zeros((2 * LANES, 2 * LANES), BF16).at[2 * pj, pj].set(1.0)
            .at[2 * pj + 1, LANES + pj].set(1.0))
    any_spec = pl.BlockSpec(memory_space=pl.ANY)
    ys = pl.pallas_call(
        _expert_kernel,
        grid_spec=pltpu.PrefetchScalarGridSpec(
            num_scalar_prefetch=4,
            grid=(n_blocks,),
            in_specs=[any_spec, any_spec, any_spec,
                      pl.BlockSpec((None, d, 2 * d_ff), lambda t, be, *_:(be[t], 0, 0)),
                      pl.BlockSpec((None, 1, 2 * d_ff), lambda t, be, *_:(be[t], 0, 0)),
                      pl.BlockSpec((None, d_ff, d), lambda t, be, *_:(be[t], 0, 0)),
                      pl.BlockSpec((None, 1, d), lambda t, be, *_:(be[t], 0, 0)),
                      pl.BlockSpec((2 * LANES, 2 * LANES), lambda t, be, *_:(0, 0))],
            out_specs=any_spec,
            scratch_shapes=[pltpu.SMEM((2, MOE_BLOCK), jnp.int32), pltpu.SMEM((2, MOE_BLOCK), jnp.int32)]
            + [pltpu.VMEM((MOE_BLOCK, d), F32)] * 4
            + [pltpu.VMEM((d, 2 * d_ff), BF16), pltpu.VMEM((d_ff, d), BF16)]
            + [pltpu.SemaphoreType.DMA((2, 2)), pltpu.SemaphoreType.DMA((2,)), pltpu.SemaphoreType.DMA((2,))],
        ),
        out_shape=jax.ShapeDtypeStruct((m + MOE_BLOCK, 1, d), F32),
        compiler_params=pltpu.CompilerParams(dimension_semantics=("arbitrary",),
                                             vmem_limit_bytes=EXPERT_VMEM_LIMIT),
        name="experts",
    )(block_expert, n_used, first, n_valid, grouped_tok, grouped_row, xm, prm['w1'], b1p, prm['w2'], b2, perm)

    ple = p_i.shape[-1]
    out = pl.pallas_call(
        _final_kernel,
        grid=(n_tok // TM,),
        in_specs=[row_spec(d)]
        + [pl.BlockSpec((TM, 1, d), functools.partial(lambda k, t: (k * (n_tok // TM) + t, 0, 0), k))
           for k in range(TOP_K)]
        + [row_spec(LANES),
                  _full((1, d)), _full((d, d)), row_spec(ple), _full((ple, d))],
        out_specs=row_spec(d),
        out_shape=jax.ShapeDtypeStruct((n_tok, d), F32),
        compiler_params=_params(1),
        name="combine_ple",
    )(h1, ys, ys, ys, ys, gates, prm['g_ple'][None, :],
      prm['w_ple_gate'].astype(BF16), p_i.reshape(n_tok, ple), prm['w_ple_proj'].astype(BF16))
    return out.reshape(bsz, seq, d)


def kernel(x, p, rel_bias, g_attn, w_in, b_f, g_fox_q, g_fox_k, g_diff_q, g_diff_k, lambda_q1, lambda_k1,
           lambda_q2, lambda_k2, g_subln, w_out, g_mlp, w_router, b_router, w1, b1, w2, b2, g_ple,
           w_ple_gate, w_ple_proj):
    stacked = dict(g_attn=g_attn, w_in=w_in, b_f=b_f, g_fox_q=g_fox_q, g_fox_k=g_fox_k, g_diff_q=g_diff_q,
                   g_diff_k=g_diff_k, lambda_q1=lambda_q1, lambda_k1=lambda_k1, lambda_q2=lambda_q2,
                   lambda_k2=lambda_k2, g_subln=g_subln, w_out=w_out, g_mlp=g_mlp, w_router=w_router,
                   b_router=b_router, w1=w1, b1=b1, w2=w2, b2=b2, g_ple=g_ple, w_ple_gate=w_ple_gate,
                   w_ple_proj=w_ple_proj)
    h = x
    for i in range(p.shape[0]):
        h = _layer(i, h, p[i], rel_bias, {name: v[i] for name, v in stacked.items()})
    return h
```

```python
import functools
import math

import jax
import jax.numpy as jnp
from jax import lax
from jax.experimental import pallas as pl
from jax.experimental.pallas import tpu as pltpu

F32 = jnp.float32
BF16 = jnp.bfloat16

HEAD_DIM = 64
N_FOX_HEADS = 8
N_DIFF_HEADS = 4
GROUP = 512
LANES = 128
NUM_BUCKETS = 32
MAX_DISTANCE = 128
N_EXPERTS = 32
TOP_K = 4
SWIGLU_ALPHA = 1.702
SWIGLU_LIMIT = 7.0
MOE_BLOCK = 256
IDX_ALIGN = 128
NORM_EPS = 1e-6
NEG_INF = -1e30
LOG2E = math.log2(math.e)
N_SPLIT = 3

TM = 512
TQ = 256
TK = 256
LOOKAHEAD = 2
VMEM_LIMIT = 48 * 1024 * 1024
EXPERT_VMEM_LIMIT = 58 * 1024 * 1024


def _params(n_axes):
    return pltpu.CompilerParams(dimension_semantics=("arbitrary",) * n_axes,
                                vmem_limit_bytes=VMEM_LIMIT)


def _dot(a, b):
    return jnp.dot(a, b, preferred_element_type=F32)


def _dot_nt(a, b):
    return lax.dot_general(a, b, (((1,), (1,)), ((), ())), preferred_element_type=F32)


def _split3(v):
    hi = v.astype(BF16)
    r = v - hi.astype(F32)
    mid = r.astype(BF16)
    lo = (r - mid.astype(F32)).astype(BF16)
    return hi, mid, lo


def _inproj_kernel(x_ref, g_ref, wfq_ref, wfk_ref, wfv_ref, wdq_ref, wdk_ref, wdv_ref, wf_ref, bf_ref,
                   gfq_ref, gfk_ref, gdq_ref, gdk_ref, seg_ref, tri_ref, place_ref,
                   fq_ref, fk_ref, fv_ref, dq_ref, dk_ref, dv_ref, carry_ref):
    @pl.when(pl.program_id(1) == 0)
    def _():
        carry_ref[...] = jnp.zeros_like(carry_ref)

    xf = x_ref[...]
    a = (xf * lax.rsqrt(jnp.mean(xf * xf, axis=-1, keepdims=True) + NORM_EPS) * g_ref[...]).astype(BF16)
    seg = seg_ref[...]

    fl = _dot(a, wf_ref[...]) + bf_ref[...]
    fq_acc = _dot_nt(wfq_ref[...], a)
    dq_acc = _dot_nt(wdq_ref[...], a)
    fk_acc = _dot(a, wfk_ref[...])
    dk_acc = _dot(a, wdk_ref[...])
    fv_acc = _dot_nt(wfv_ref[...], a)
    dv_acc = _dot_nt(wdv_ref[...], a)

    half = seg.shape[0]

    def q_norm(acc, gain_ref):
        sq = (acc * acc).astype(BF16)
        ms = jnp.concatenate([_dot(seg, sq[:half]), _dot(seg, sq[half:])], axis=0)
        return (acc * lax.rsqrt(ms + NORM_EPS) * gain_ref[...]).astype(BF16)

    def k_norm(acc, gain_ref):
        sq = (acc * acc).astype(BF16)
        ms = jnp.concatenate([_dot(sq[:, :half], seg), _dot(sq[:, half:], seg)], axis=1)
        return (acc * lax.rsqrt(ms + NORM_EPS) * gain_ref[...]).astype(BF16)

    def v_store(acc, out_ref):
        acc = acc.astype(BF16)
        for hb in range(GROUP // LANES):
            for j in range(TM // TK):
                out_ref[hb, j] = acc[hb * LANES:(hb + 1) * LANES, j * TK:(j + 1) * TK]

    logf = jnp.minimum(fl, 0.0) - jnp.log1p(jnp.exp(-jnp.abs(fl)))
    sums = _dot(tri_ref[...], jnp.concatenate(_split3(logf), axis=1))
    cs = carry_ref[...] + sums[:, :LANES] + sums[:, LANES:2 * LANES] + sums[:, 2 * LANES:]
    carry_ref[...] = cs[TM - 1:TM, :]

    fq_ref[...] = q_norm(fq_acc, gfq_ref)
    dq_ref[...] = q_norm(dq_acc, gdq_ref)
    fk = k_norm(fk_acc, gfk_ref)
    dk_ref[...] = k_norm(dk_acc, gdk_ref)
    v_store(fv_acc, fv_ref)
    v_store(dv_acc, dv_ref)

    extra = _dot(jnp.concatenate(_split3(cs * LOG2E), axis=1), place_ref[...]).astype(BF16)
    for hb in range(GROUP // LANES):
        fk_ref[:, 2 * hb * LANES:(2 * hb + 1) * LANES] = fk[:, hb * LANES:(hb + 1) * LANES]
        fk_ref[:, (2 * hb + 1) * LANES:(2 * hb + 2) * LANES] = extra[:, hb * LANES:(hb + 1) * LANES]


def _two_map_queries(qT_blk):
    row = lax.broadcasted_iota(jnp.int32, qT_blk.shape, 0)
    zero = jnp.zeros_like(qT_blk)
    return jnp.concatenate([jnp.where(row < HEAD_DIM, qT_blk, zero),
                            jnp.where(row >= HEAD_DIM, qT_blk, zero)], axis=1)


def _softmax_step(sT, vT_blk, carry):
    m, l, acc = carry
    m_new = jnp.maximum(m, jnp.max(sT, axis=0, keepdims=True))
    alpha = jnp.exp2(m - m_new)
    pT = jnp.exp2(sT - m_new)
    l = alpha * l + jnp.sum(pT, axis=0, keepdims=True)
    acc = alpha * acc + _dot(vT_blk, pT.astype(BF16))
    return m_new, l, acc


def _causal_keep():
    key = lax.broadcasted_iota(jnp.int32, (TK, 2 * TQ), 0)
    col = lax.broadcasted_iota(jnp.int32, (TK, 2 * TQ), 1)
    qry = jnp.where(col >= TQ, col - TQ, col)
    return key <= qry


def _init_carry():
    return (jnp.full((1, 2 * TQ), NEG_INF, F32), jnp.zeros((1, 2 * TQ), F32),
            jnp.zeros((LANES, 2 * TQ), F32))


def _causal_sweep(seq, k_ref, vT_ref, queries, adjust, finish):
    pairs = [(qi, kj) for qi in range(seq // TQ) for kj in range(qi + 1)]
    q_ops = {}

    def score(qi, kj):
        if qi not in q_ops:
            q_ops[qi] = queries(qi)
        return _dot(k_ref[kj * TK:(kj + 1) * TK, :], q_ops[qi])

    pending = [score(*pr) for pr in pairs[:LOOKAHEAD]]
    carry = None
    for n, (qi, kj) in enumerate(pairs):
        if n + LOOKAHEAD < len(pairs):
            pending.append(score(*pairs[n + LOOKAHEAD]))
        if kj == 0:
            carry = _init_carry()
        carry = _softmax_step(adjust(pending.pop(0), qi, kj), vT_ref[kj], carry)
        if kj == qi:
            m, l, acc = carry
            finish(qi, acc * (1.0 / l))


def _fox_kernel(qT_ref, k_ref, vT_ref, o_ref):
    seq = k_ref.shape[0]
    keep = _causal_keep()
    row = lax.broadcasted_iota(jnp.int32, (LANES, TQ), 0)
    arow = lax.broadcasted_iota(jnp.int32, (LANES, 2 * TQ), 0)
    acol = lax.broadcasted_iota(jnp.int32, (LANES, 2 * TQ), 1)
    first = (acol < TQ) & (arow < N_SPLIT)
    second = (acol >= TQ) & (arow >= N_SPLIT) & (arow < 2 * N_SPLIT)
    minus_c = jnp.where(first | second, -1.0, 0.0).astype(BF16)

    def queries(qi):
        return jnp.concatenate([_two_map_queries(qT_ref[:, qi * TQ:(qi + 1) * TQ]), minus_c], axis=0)

    def adjust(sT, qi, kj):
        return jnp.where(keep, sT, NEG_INF) if kj == qi else sT

    def finish(qi, oT):
        o = jnp.where(row < HEAD_DIM, oT[:, :TQ], oT[:, TQ:])
        o_ref[qi * TQ:(qi + 1) * TQ, :] = o.T.astype(BF16)

    _causal_sweep(seq, k_ref, vT_ref, queries, adjust, finish)


def _diff_kernel(lam_init, qT_ref, k_ref, vT_ref, bias_ref, lam_ref, gsub_ref, o_ref):
    seq = k_ref.shape[0]
    lp = lam_ref[...]
    lam = (jnp.exp(jnp.sum(lp[0:1] * lp[1:2], axis=1, keepdims=True))
           - jnp.exp(jnp.sum(lp[2:3] * lp[3:4], axis=1, keepdims=True)) + lam_init)

    def queries(qi):
        return _two_map_queries(qT_ref[:, qi * TQ:(qi + 1) * TQ])

    def adjust(sT, qi, kj):
        if kj >= qi - 1:
            b = bias_ref[qi - kj]
            sT = jnp.concatenate([sT[:, :TQ] + b, sT[:, TQ:] + b], axis=1)
        return sT

    def finish(qi, oT):
        o = oT[:, :TQ] - lam * oT[:, TQ:]
        y = o * lax.rsqrt(jnp.mean(o * o, axis=0, keepdims=True) + NORM_EPS) * gsub_ref[...]
        o_ref[qi * TQ:(qi + 1) * TQ, :] = y.T.astype(BF16)

    _causal_sweep(seq, k_ref, vT_ref, queries, adjust, finish)


def _mix_kernel(x_ref, fox_ref, dif_ref, wa_ref, wb_ref, g_ref, wr_hi_ref, wr_lo_ref, br_ref,
                h_ref, xm_ref, topi_ref, gate_ref, cnt_ref, carry_ref):
    @pl.when(pl.program_id(0) == 0)
    def _():
        carry_ref[...] = jnp.zeros_like(carry_ref)

    h = x_ref[...] + _dot(fox_ref[...], wa_ref[...]) + _dot(dif_ref[...], wb_ref[...])
    h_ref[...] = h
    xm = h * lax.rsqrt(jnp.mean(h * h, axis=-1, keepdims=True) + NORM_EPS) * g_ref[...]
    xm_ref[:, 0, :] = xm

    x_hi = xm.astype(BF16)
    x_lo = (xm - x_hi.astype(F32)).astype(BF16)
    logits = (_dot(x_hi, wr_hi_ref[...]) + _dot(x_hi, wr_lo_ref[...]) + _dot(x_lo, wr_hi_ref[...])
              + br_ref[...])

    lane = lax.broadcasted_iota(jnp.int32, logits.shape, 1)
    vals, sels, idxs = [], [], []
    for _ in range(TOP_K):
        mx = jnp.max(logits, axis=1, keepdims=True)
        idx = jnp.min(jnp.where(logits == mx, lane, LANES), axis=1, keepdims=True)
        sel = lane == idx
        logits = jnp.where(sel, -jnp.inf, logits)
        vals.append(mx)
        sels.append(sel)
        idxs.append(idx)
    exps = [jnp.exp(v - vals[0]) for v in vals]
    denom = exps[0] + exps[1] + exps[2] + exps[3]

    multi_hot = sum(s.astype(F32) for s in sels)
    topi = jnp.zeros(logits.shape, jnp.int32)
    gate = jnp.zeros(logits.shape, F32)
    for k in range(TOP_K):
        topi = jnp.where(lane == k, idxs[k], topi)
        gate = jnp.where(lane == k, exps[k] / denom, gate)
    topi_ref[...] = topi
    gate_ref[...] = gate
    carry_ref[...] = carry_ref[...] + jnp.sum(multi_hot, axis=0, keepdims=True)
    cnt_ref[...] = carry_ref[...]


def _expert_kernel(be_ref, nu_ref, off_ref, nv_ref, tok_hbm, asg_hbm, x_hbm, w1f_ref, b1_ref, w2f_ref, b2_ref,
                   perm_ref, y_hbm, tok_ref, asg_ref, x0, x1, y0, y1, w1_ref, w2_ref, h_ref, idx_sem, g_sem, s_sem):
    i = pl.program_id(0)
    n_used = nu_ref[0]
    n_blocks = be_ref.shape[0]
    spare_row = y_hbm.shape[0] - MOE_BLOCK
    xbufs, ybufs = (x0, x1), (y0, y1)

    def tok_copy(blk, slot):
        start = pl.multiple_of(off_ref[blk], IDX_ALIGN)
        return pltpu.make_async_copy(tok_hbm.at[pl.ds(start, MOE_BLOCK)], tok_ref.at[slot], idx_sem.at[0, slot])

    def asg_copy(blk, slot):
        start = pl.multiple_of(off_ref[blk], IDX_ALIGN)
        return pltpu.make_async_copy(asg_hbm.at[pl.ds(start, MOE_BLOCK)], asg_ref.at[slot], idx_sem.at[1, slot])

    def asg_ready(blk, slot):
        asg_copy(0, slot).wait()

        def fix(r, carry):
            asg_ref[slot, r] = spare_row + r
            return carry
        lax.fori_loop(nv_ref[blk], MOE_BLOCK, fix, 0)

    def row_loop(fn):
        for r in range(MOE_BLOCK):
            fn(r)

    def gather_issue(slot):
        row_loop(lambda r: pltpu.make_async_copy(x_hbm.at[tok_ref[slot, r]], xbufs[slot].at[pl.ds(r, 1)],
                                                 g_sem.at[slot]).start())

    def gather_wait(slot):
        pltpu.make_async_copy(xbufs[slot], xbufs[slot], g_sem.at[slot]).wait()

    def scatter_issue(slot):
        row_loop(lambda r: pltpu.make_async_copy(ybufs[slot].at[pl.ds(r, 1)], y_hbm.at[asg_ref[slot, r]],
                                                 s_sem.at[slot]).start())

    def scatter_wait(slot):
        pltpu.make_async_copy(ybufs[slot], ybufs[slot], s_sem.at[slot]).wait()

    @pl.when(i == 0)
    def _():
        y1[...] = jnp.zeros(y1.shape, y1.dtype)
        for r in range(MOE_BLOCK):
            asg_ref[1, r] = spare_row + r
        tok_copy(0, 0).start()
        tok_copy(0, 0).wait()
        tok_copy(jnp.minimum(1, n_blocks - 1), 1).start()
        gather_issue(0)

    @pl.when((i < n_used) & ((i == 0) | (be_ref[i] != be_ref[jnp.maximum(i - 1, 0)])))
    def _():
        for g in range(w1f_ref.shape[1] // (2 * LANES)):
            cols = slice(2 * g * LANES, 2 * (g + 1) * LANES)
            w1_ref[:, cols] = _dot(w1f_ref[:, cols].astype(BF16), perm_ref[...]).astype(BF16)
        w2_ref[...] = w2f_ref[...].astype(BF16)

    def step(cur):
        nxt = 1 - cur
        asg_copy(i, cur).start()
        tok_copy(0, nxt).wait()

        @pl.when(i >= 1)
        def _():
            asg_ready(i - 1, nxt)
            scatter_wait(cur)

        gather_wait(cur)
        gather_issue(nxt)
        scatter_issue(nxt)
        tok_copy(jnp.minimum(i + 2, n_blocks - 1), cur).start()
        xbuf, ybuf = xbufs[cur], ybufs[cur]
        h_ref[...] = _dot(xbuf[...].astype(BF16), w1_ref[...]) + b1_ref[...]

        @pl.when(n_used > 0)
        def _():
            acts = []
            for c in range(h_ref.shape[1] // (2 * LANES)):
                glu = jnp.minimum(h_ref[:, 2 * c * LANES:(2 * c + 1) * LANES], SWIGLU_LIMIT)
                lin = jnp.clip(h_ref[:, (2 * c + 1) * LANES:(2 * c + 2) * LANES], -SWIGLU_LIMIT, SWIGLU_LIMIT)
                acts.append((glu * jax.nn.sigmoid(SWIGLU_ALPHA * glu) * (lin + 1.0)).astype(BF16))
            ybuf[...] = _dot(jnp.concatenate(acts, axis=1), w2_ref[...]) + b2_ref[...]

        @pl.when(i == n_used - 1)
        def _():
            asg_ready(i, cur)
            scatter_issue(cur)
            scatter_wait(nxt)
            scatter_wait(cur)
            gather_wait(nxt)
            tok_copy(0, cur).wait()

    for parity in range(2):
        pl.when((i < n_used) & (i % 2 == parity))(functools.partial(step, parity))


def _final_kernel(h_ref, *rest):
    y_refs, (gate_ref, g_ref, wg_ref, p_ref, wp_ref, o_ref) = rest[:TOP_K], rest[TOP_K:]
    gates = gate_ref[...]
    moe = sum(gates[:, k:k + 1] * y_ref[:, 0, :] for k, y_ref in enumerate(y_refs))
    h = h_ref[...] + moe
    n = (h * lax.rsqrt(jnp.mean(h * h, axis=-1, keepdims=True) + NORM_EPS) * g_ref[...]).astype(BF16)
    gate = jax.nn.sigmoid(_dot(n, wg_ref[...]))
    o_ref[...] = h + gate * _dot(p_ref[...].astype(BF16), wp_ref[...])


def _t5_bucket(rel):
    n = jnp.maximum(rel, 0)
    max_exact = NUM_BUCKETS // 2
    nf = jnp.maximum(n, 1).astype(F32)
    large = max_exact + (jnp.log(nf / max_exact) / math.log(MAX_DISTANCE / max_exact)
                         * (NUM_BUCKETS - max_exact)).astype(jnp.int32)
    large = jnp.minimum(large, NUM_BUCKETS - 1)
    return jnp.where(n < max_exact, n, large)


def _full(shape):
    return pl.BlockSpec(shape, lambda *_: (0,) * len(shape))


def _layer(i, h_in, p_i, rel_bias, prm):
    bsz, seq, d = h_in.shape
    n_tok = bsz * seq
    ns = seq // TM
    nk = seq // TK
    x2 = h_in.reshape(n_tok, d)

    w_in = prm['w_in']
    fox_w = N_FOX_HEADS * HEAD_DIM
    offs = [0, fox_w, 2 * fox_w, 3 * fox_w, 3 * fox_w + N_FOX_HEADS]
    offs += [offs[4] + GROUP, offs[4] + 2 * GROUP, offs[4] + 3 * GROUP]
    col = lambda a, b: w_in[:, a:b].astype(BF16)
    wfqT, wfk, wfvT = col(offs[0], offs[1]).T, col(offs[1], offs[2]), col(offs[2], offs[3]).T
    wdqT, wdk, wdvT = col(offs[4], offs[5]).T, col(offs[5], offs[6]), col(offs[6], offs[7]).T
    wf = jnp.zeros((d, LANES), BF16).at[:, :N_FOX_HEADS].set(col(offs[3], offs[4]))
    bf = jnp.zeros((1, LANES), F32).at[0, :N_FOX_HEADS].set(prm['b_f'])
    scale = HEAD_DIM ** -0.5 * LOG2E
    rep = GROUP // HEAD_DIM
    heads = jnp.arange(N_FOX_HEADS)
    place = jnp.concatenate([jnp.zeros((LANES, GROUP), BF16)
                             .at[heads, (heads // 2) * LANES + (heads % 2) * N_SPLIT + t].set(1.0)
                             for t in range(N_SPLIT)])
    gfqT = jnp.broadcast_to((jnp.tile(prm['g_fox_q'], rep) * scale)[:, None], (GROUP, TM))
    gdqT = jnp.broadcast_to((jnp.tile(prm['g_diff_q'], rep) * scale)[:, None], (GROUP, TM))
    gfk = jnp.tile(prm['g_fox_k'], rep)[None, :]
    gdk = jnp.tile(prm['g_diff_k'], rep)[None, :]
    gi = jnp.arange(GROUP // 2) // HEAD_DIM
    seg = jnp.where(gi[:, None] == gi[None, :], 1.0 / HEAD_DIM, 0.0).astype(BF16)
    ti = jnp.arange(TM)
    tri_incl = (ti[None, :] <= ti[:, None]).astype(BF16)

    w_spec_t = _full((GROUP, d))
    w_spec = _full((d, GROUP))
    qT_shape = jax.ShapeDtypeStruct((bsz, GROUP, seq), BF16)
    k_shape = jax.ShapeDtypeStruct((n_tok, GROUP), BF16)
    vT_shape = jax.ShapeDtypeStruct((bsz, GROUP // LANES, nk, LANES, TK), BF16)
    qT_spec = pl.BlockSpec((None, GROUP, TM), lambda b, s: (b, 0, s))
    k_spec = pl.BlockSpec((TM, GROUP), lambda b, s: (b * ns + s, 0))
    vT_spec = pl.BlockSpec((None, GROUP // LANES, TM // TK, LANES, TK), lambda b, s: (b, 0, s, 0, 0))
    fk_shape = jax.ShapeDtypeStruct((n_tok, 2 * GROUP), BF16)
    fk_spec = pl.BlockSpec((TM, 2 * GROUP), lambda b, s: (b * ns + s, 0))
    fqT, fk, fvT, dqT, dk, dvT = pl.pallas_call(
        _inproj_kernel,
        grid=(bsz, ns),
        in_specs=[pl.BlockSpec((TM, d), lambda b, s: (b * ns + s, 0)), _full((1, d)),
                  w_spec_t, w_spec, w_spec_t, w_spec_t, w_spec, w_spec_t, _full((d, LANES)), _full((1, LANES)),
                  _full((GROUP, TM)), _full((1, GROUP)), _full((GROUP, TM)), _full((1, GROUP)),
                  _full((GROUP // 2, GROUP // 2)), _full((TM, TM)), _full((N_SPLIT * LANES, GROUP))],
        out_specs=[qT_spec, fk_spec, vT_spec, qT_spec, k_spec, vT_spec],
        out_shape=[qT_shape, fk_shape, vT_shape, qT_shape, k_shape, vT_shape],
        scratch_shapes=[pltpu.VMEM((1, LANES), F32)],
        compiler_params=_params(2),
        name="inproj",
    )(x2, prm['g_attn'][None, :], wfqT, wfk, wfvT, wdqT, wdk, wdvT, wf, bf, gfqT, gfk, gdqT, gdk, seg, tri_incl,
      place)

    n_hb = GROUP // LANES
    att_q = pl.BlockSpec((None, LANES, seq), lambda b, h: (b, h, 0))
    att_k = pl.BlockSpec((seq, LANES), lambda b, h: (b, h))
    att_v = pl.BlockSpec((None, None, nk, LANES, TK), lambda b, h: (b, h, 0, 0, 0))
    att_o = pl.BlockSpec((seq, LANES), lambda b, h: (b, h))
    fox_out = pl.pallas_call(
        _fox_kernel,
        grid=(bsz, n_hb),
        in_specs=[att_q, pl.BlockSpec((seq, 2 * LANES), lambda b, h: (b, h)), att_v],
        out_specs=att_o,
        out_shape=jax.ShapeDtypeStruct((n_tok, GROUP), BF16),
        compiler_params=_params(2),
        name="fox_attention",
    )(fqT, fk, fvT)

    kpos = jnp.arange(TK)[:, None]
    qpos = jnp.arange(TQ)[None, :]
    rel = jnp.stack([qpos - kpos, qpos - kpos + TK])
    assert TK + 1 >= MAX_DISTANCE and TQ == TK
    table = (rel_bias.astype(F32) - rel_bias[NUM_BUCKETS - 1].astype(F32)[None, :]) * LOG2E
    onehot = _t5_bucket(rel)[None, ..., None] == jnp.arange(NUM_BUCKETS, dtype=jnp.int32)
    biasT = jnp.sum(jnp.where(onehot, table.T[:, None, None, None, :], 0.0), axis=-1)
    biasT = jnp.where((rel >= 0)[None], biasT, NEG_INF)
    lam_init = 0.8 - 0.6 * math.exp(-0.3 * i)
    lam_p = jnp.stack([prm['lambda_q1'], prm['lambda_k1'], prm['lambda_q2'], prm['lambda_k2']]).astype(F32)
    gsubT = jnp.broadcast_to((prm['g_subln'] * (1.0 - lam_init))[:, None], (LANES, TQ))
    diff_out = pl.pallas_call(
        functools.partial(_diff_kernel, lam_init),
        grid=(bsz, N_DIFF_HEADS),
        in_specs=[att_q, att_k, att_v, pl.BlockSpec((None, 2, TK, TQ), lambda b, h: (h, 0, 0, 0)),
                  _full((4, HEAD_DIM)), _full((LANES, TQ))],
        out_specs=att_o,
        out_shape=jax.ShapeDtypeStruct((n_tok, GROUP), BF16),
        compiler_params=_params(2),
        name="diff_attention",
    )(dqT, dk, dvT, biasT, lam_p, gsubT)

    w_out = prm['w_out'].astype(BF16)
    wr = jnp.zeros((d, LANES), F32).at[:, :N_EXPERTS].set(prm['w_router'])
    wr_hi = wr.astype(BF16)
    wr_lo = (wr - wr_hi.astype(F32)).astype(BF16)
    br = jnp.full((1, LANES), NEG_INF, F32).at[0, :N_EXPERTS].set(prm['b_router'])
    row_spec = lambda w: pl.BlockSpec((TM, w), lambda t: (t, 0))
    h1, xm, topi, gates, counts = pl.pallas_call(
        _mix_kernel,
        grid=(n_tok // TM,),
        in_specs=[row_spec(d), row_spec(GROUP), row_spec(GROUP), _full((GROUP, d)), _full((GROUP, d)),
                  _full((1, d)), _full((d, LANES)), _full((d, LANES)), _full((1, LANES))],
        out_specs=[row_spec(d), pl.BlockSpec((TM, 1, d), lambda t: (t, 0, 0)), row_spec(LANES),
                   row_spec(LANES), _full((1, LANES))],
        out_shape=[jax.ShapeDtypeStruct((n_tok, d), F32), jax.ShapeDtypeStruct((n_tok, 1, d), F32),
                   jax.ShapeDtypeStruct((n_tok, LANES), jnp.int32), jax.ShapeDtypeStruct((n_tok, LANES), F32),
                   jax.ShapeDtypeStruct((1, LANES), F32)],
        scratch_shapes=[pltpu.VMEM((1, LANES), F32)],
        compiler_params=_params(1),
        name="mix_router",
    )(x2, fox_out, diff_out, w_out[:GROUP], w_out[GROUP:], prm['g_mlp'][None, :], wr_hi, wr_lo, br)

    m = n_tok * TOP_K
    n_blocks = (m + N_EXPERTS * MOE_BLOCK + MOE_BLOCK - 1) // MOE_BLOCK
    p_rows = n_blocks * MOE_BLOCK
    cnt = counts[0, :N_EXPERTS].astype(jnp.int32)
    padded = ((cnt + MOE_BLOCK - 1) // MOE_BLOCK) * MOE_BLOCK
    pad_ends = jnp.cumsum(padded)
    pad_starts = pad_ends - padded
    block_start = jnp.arange(n_blocks, dtype=jnp.int32) * MOE_BLOCK
    block_expert = jnp.minimum(jnp.sum(pad_ends[None, :] <= block_start[:, None], axis=1),
                               N_EXPERTS - 1).astype(jnp.int32)
    n_used = (pad_ends[-1] // MOE_BLOCK).astype(jnp.int32)[None]
    experts = jnp.arange(N_EXPERTS, dtype=jnp.int32)
    real = topi[:, :TOP_K] * (2 * m) + jnp.arange(m, dtype=jnp.int32).reshape(n_tok, TOP_K)
    fill = jnp.arange(IDX_ALIGN - 1, dtype=jnp.int32)[None, :]
    need = (-cnt) % IDX_ALIGN
    filler = jnp.where(fill < need[:, None], experts[:, None] * (2 * m) + m + fill,
                       N_EXPERTS * (2 * m) + experts[:, None] * IDX_ALIGN + fill)
    keys = jnp.sort(jnp.concatenate([real.reshape(m), filler.reshape(-1)]))
    flat = jnp.concatenate([(keys % (2 * m)) % m, jnp.zeros((MOE_BLOCK,), jnp.int32)])
    grouped_tok = flat // TOP_K
    grouped_row = (flat % TOP_K) * n_tok + grouped_tok
    onehot_e = block_expert[:, None] == experts[None, :]
    pick = lambda v: jnp.sum(jnp.where(onehot_e, v[None, :], 0), axis=1)
    in_group = block_start - pick(pad_starts)
    n_valid = jnp.clip(pick(cnt) - in_group, 0, MOE_BLOCK)
    group_start = jnp.cumsum(cnt + need) - (cnt + need)
    first = jnp.clip(pick(group_start) + in_group, 0, keys.shape[0]) // IDX_ALIGN * IDX_ALIGN

    d_ff = prm['w2'].shape[1]
    n_grp = 2 * d_ff // (2 * LANES)
    b1p = prm['b1'].reshape(N_EXPERTS, n_grp, LANES, 2).transpose(0, 1, 3, 2).reshape(N_EXPERTS, 1, 2 * d_ff)
    b2 = prm['b2'].reshape(N_EXPERTS, 1, d)
    pj = jnp.arange(LANES)
    perm = (jnp.zeros((2 * LANES, 2 * LANES), BF16).at[2 * pj, pj].set(1.0)
            .at[2 * pj + 1, LANES + pj].set(1.0))
    any_spec = pl.BlockSpec(memory_space=pl.ANY)
    ys = pl.pallas_call(
        _expert_kernel,
        grid_spec=pltpu.PrefetchScalarGridSpec(
            num_scalar_prefetch=4,
            grid=(n_blocks,),
            in_specs=[any_spec, any_spec, any_spec,
                      pl.BlockSpec((None, d, 2 * d_ff), lambda t, be, *_:(be[t], 0, 0)),
                      pl.BlockSpec((None, 1, 2 * d_ff), lambda t, be, *_:(be[t], 0, 0)),
                      pl.BlockSpec((None, d_ff, d), lambda t, be, *_:(be[t], 0, 0)),
                      pl.BlockSpec((None, 1, d), lambda t, be, *_:(be[t], 0, 0)),
                      pl.BlockSpec((2 * LANES, 2 * LANES), lambda t, be, *_:(0, 0))],
            out_specs=any_spec,
            scratch_shapes=[pltpu.SMEM((2, MOE_BLOCK), jnp.int32), pltpu.SMEM((2, MOE_BLOCK), jnp.int32)]
            + [pltpu.VMEM((MOE_BLOCK, d), F32)] * 4
            + [pltpu.VMEM((d, 2 * d_ff), BF16), pltpu.VMEM((d_ff, d), BF16), pltpu.VMEM((MOE_BLOCK, 2 * d_ff), F32)]
            + [pltpu.SemaphoreType.DMA((2, 2)), pltpu.SemaphoreType.DMA((2,)), pltpu.SemaphoreType.DMA((2,))],
        ),
        out_shape=jax.ShapeDtypeStruct((m + MOE_BLOCK, 1, d), F32),
        compiler_params=pltpu.CompilerParams(dimension_semantics=("arbitrary",),
                                             vmem_limit_bytes=EXPERT_VMEM_LIMIT),
        name="experts",
    )(block_expert, n_used, first, n_valid, grouped_tok, grouped_row, xm, prm['w1'], b1p, prm['w2'], b2, perm)

    ple = p_i.shape[-1]
    out = pl.pallas_call(
        _final_kernel,
        grid=(n_tok // TM,),
        in_specs=[row_spec(d)]
        + [pl.BlockSpec((TM, 1, d), functools.partial(lambda k, t: (k * (n_tok // TM) + t, 0, 0), k))
           for k in range(TOP_K)]
        + [row_spec(LANES),
                  _full((1, d)), _full((d, d)), row_spec(ple), _full((ple, d))],
        out_specs=row_spec(d),
        out_shape=jax.ShapeDtypeStruct((n_tok, d), F32),
        compiler_params=_params(1),
        name="combine_ple",
    )(h1, ys, ys, ys, ys, gates, prm['g_ple'][None, :],
      prm['w_ple_gate'].astype(BF16), p_i.reshape(n_tok, ple), prm['w_ple_proj'].astype(BF16))
    return out.reshape(bsz, seq, d)


def kernel(x, p, rel_bias, g_attn, w_in, b_f, g_fox_q, g_fox_k, g_diff_q, g_diff_k, lambda_q1, lambda_k1,
           lambda_q2, lambda_k2, g_subln, w_out, g_mlp, w_router, b_router, w1, b1, w2, b2, g_ple,
           w_ple_gate, w_ple_proj):
    stacked = dict(g_attn=g_attn, w_in=w_in, b_f=b_f, g_fox_q=g_fox_q, g_fox_k=g_fox_k, g_diff_q=g_diff_q,
                   g_diff_k=g_diff_k, lambda_q1=lambda_q1, lambda_k1=lambda_k1, lambda_q2=lambda_q2,
                   lambda_k2=lambda_k2, g_subln=g_subln, w_out=w_out, g_mlp=g_mlp, w_router=w_router,
                   b_router=b_router, w1=w1, b1=b1, w2=w2, b2=b2, g_ple=g_ple, w_ple_gate=w_ple_gate,
                   w_ple_proj=w_ple_proj)
    h = x
    for i in range(p.shape[0]):
        h = _layer(i, h, p[i], rel_bias, {name: v[i] for name, v in stacked.items()})
    return h
```

```python
import functools
import math

import jax
import jax.numpy as jnp
from jax import lax
from jax.experimental import pallas as pl
from jax.experimental.pallas import tpu as pltpu

F32 = jnp.float32
BF16 = jnp.bfloat16

HEAD_DIM = 64
N_FOX_HEADS = 8
N_DIFF_HEADS = 4
GROUP = 512
LANES = 128
NUM_BUCKETS = 32
MAX_DISTANCE = 128
N_EXPERTS = 32
TOP_K = 4
SWIGLU_ALPHA = 1.702
SWIGLU_LIMIT = 7.0
MOE_BLOCK = 256
IDX_ALIGN = 128
NORM_EPS = 1e-6
NEG_INF = -1e30
LOG2E = math.log2(math.e)
N_SPLIT = 3

TM = 512
TQ = 256
TK = 256
LOOKAHEAD = 2
VMEM_LIMIT = 48 * 1024 * 1024
EXPERT_VMEM_LIMIT = 58 * 1024 * 1024


def _params(n_axes):
    return pltpu.CompilerParams(dimension_semantics=("arbitrary",) * n_axes,
                                vmem_limit_bytes=VMEM_LIMIT)


def _dot(a, b):
    return jnp.dot(a, b, preferred_element_type=F32)


def _dot_nt(a, b):
    return lax.dot_general(a, b, (((1,), (1,)), ((), ())), preferred_element_type=F32)


def _split3(v):
    hi = v.astype(BF16)
    r = v - hi.astype(F32)
    mid = r.astype(BF16)
    lo = (r - mid.astype(F32)).astype(BF16)
    return hi, mid, lo


def _inproj_kernel(x_ref, g_ref, wfq_ref, wfk_ref, wfv_ref, wdq_ref, wdk_ref, wdv_ref, wf_ref, bf_ref,
                   gfq_ref, gfk_ref, gdq_ref, gdk_ref, seg_ref, tri_ref, place_ref,
                   fq_ref, fk_ref, fv_ref, dq_ref, dk_ref, dv_ref, carry_ref):
    @pl.when(pl.program_id(1) == 0)
    def _():
        carry_ref[...] = jnp.zeros_like(carry_ref)

    xf = x_ref[...]
    a = (xf * lax.rsqrt(jnp.mean(xf * xf, axis=-1, keepdims=True) + NORM_EPS) * g_ref[...]).astype(BF16)
    seg = seg_ref[...]

    fl = _dot(a, wf_ref[...]) + bf_ref[...]
    fq_acc = _dot_nt(wfq_ref[...], a)
    dq_acc = _dot_nt(wdq_ref[...], a)
    fk_acc = _dot(a, wfk_ref[...])
    dk_acc = _dot(a, wdk_ref[...])
    fv_acc = _dot_nt(wfv_ref[...], a)
    dv_acc = _dot_nt(wdv_ref[...], a)

    half = seg.shape[0]

    def q_norm(acc, gain_ref):
        sq = (acc * acc).astype(BF16)
        ms = jnp.concatenate([_dot(seg, sq[:half]), _dot(seg, sq[half:])], axis=0)
        return (acc * lax.rsqrt(ms + NORM_EPS) * gain_ref[...]).astype(BF16)

    def k_norm(acc, gain_ref):
        sq = (acc * acc).astype(BF16)
        ms = jnp.concatenate([_dot(sq[:, :half], seg), _dot(sq[:, half:], seg)], axis=1)
        return (acc * lax.rsqrt(ms + NORM_EPS) * gain_ref[...]).astype(BF16)

    def v_store(acc, out_ref):
        acc = acc.astype(BF16)
        for hb in range(GROUP // LANES):
            for j in range(TM // TK):
                out_ref[hb, j] = acc[hb * LANES:(hb + 1) * LANES, j * TK:(j + 1) * TK]

    logf = jnp.minimum(fl, 0.0) - jnp.log1p(jnp.exp(-jnp.abs(fl)))
    sums = _dot(tri_ref[...], jnp.concatenate(_split3(logf), axis=1))
    cs = carry_ref[...] + sums[:, :LANES] + sums[:, LANES:2 * LANES] + sums[:, 2 * LANES:]
    carry_ref[...] = cs[TM - 1:TM, :]

    fq_ref[...] = q_norm(fq_acc, gfq_ref)
    dq_ref[...] = q_norm(dq_acc, gdq_ref)
    fk = k_norm(fk_acc, gfk_ref)
    dk_ref[...] = k_norm(dk_acc, gdk_ref)
    v_store(fv_acc, fv_ref)
    v_store(dv_acc, dv_ref)

    extra = _dot(jnp.concatenate(_split3(cs * LOG2E), axis=1), place_ref[...]).astype(BF16)
    for hb in range(GROUP // LANES):
        fk_ref[:, 2 * hb * LANES:(2 * hb + 1) * LANES] = fk[:, hb * LANES:(hb + 1) * LANES]
        fk_ref[:, (2 * hb + 1) * LANES:(2 * hb + 2) * LANES] = extra[:, hb * LANES:(hb + 1) * LANES]


def _two_map_queries(qT_blk):
    row = lax.broadcasted_iota(jnp.int32, qT_blk.shape, 0)
    zero = jnp.zeros_like(qT_blk)
    return jnp.concatenate([jnp.where(row < HEAD_DIM, qT_blk, zero),
                            jnp.where(row >= HEAD_DIM, qT_blk, zero)], axis=1)


def _softmax_step(sT, vT_blk, carry):
    m, l, acc = carry
    m_new = jnp.maximum(m, jnp.max(sT, axis=0, keepdims=True))
    alpha = jnp.exp2(m - m_new)
    pT = jnp.exp2(sT - m_new)
    l = alpha * l + jnp.sum(pT, axis=0, keepdims=True)
    acc = alpha * acc + _dot(vT_blk, pT.astype(BF16))
    return m_new, l, acc


def _causal_keep():
    key = lax.broadcasted_iota(jnp.int32, (TK, 2 * TQ), 0)
    col = lax.broadcasted_iota(jnp.int32, (TK, 2 * TQ), 1)
    qry = jnp.where(col >= TQ, col - TQ, col)
    return key <= qry


def _init_carry():
    return (jnp.full((1, 2 * TQ), NEG_INF, F32), jnp.zeros((1, 2 * TQ), F32),
            jnp.zeros((LANES, 2 * TQ), F32))


def _causal_sweep(seq, k_ref, vT_ref, queries, adjust, finish):
    pairs = [(qi, kj) for qi in range(seq // TQ) for kj in range(qi + 1)]
    q_ops = {}

    def score(qi, kj):
        if qi not in q_ops:
            q_ops[qi] = queries(qi)
        return _dot(k_ref[kj * TK:(kj + 1) * TK, :], q_ops[qi])

    pending = [score(*pr) for pr in pairs[:LOOKAHEAD]]
    carry = None
    for n, (qi, kj) in enumerate(pairs):
        if n + LOOKAHEAD < len(pairs):
            pending.append(score(*pairs[n + LOOKAHEAD]))
        if kj == 0:
            carry = _init_carry()
        carry = _softmax_step(adjust(pending.pop(0), qi, kj), vT_ref[kj], carry)
        if kj == qi:
            m, l, acc = carry
            finish(qi, acc * (1.0 / l))


def _fox_kernel(qT_ref, k_ref, vT_ref, o_ref):
    seq = k_ref.shape[0]
    keep = _causal_keep()
    row = lax.broadcasted_iota(jnp.int32, (LANES, TQ), 0)
    arow = lax.broadcasted_iota(jnp.int32, (LANES, 2 * TQ), 0)
    acol = lax.broadcasted_iota(jnp.int32, (LANES, 2 * TQ), 1)
    first = (acol < TQ) & (arow < N_SPLIT)
    second = (acol >= TQ) & (arow >= N_SPLIT) & (arow < 2 * N_SPLIT)
    minus_c = jnp.where(first | second, -1.0, 0.0).astype(BF16)

    def queries(qi):
        return jnp.concatenate([_two_map_queries(qT_ref[:, qi * TQ:(qi + 1) * TQ]), minus_c], axis=0)

    def adjust(sT, qi, kj):
        return jnp.where(keep, sT, NEG_INF) if kj == qi else sT

    def finish(qi, oT):
        o = jnp.where(row < HEAD_DIM, oT[:, :TQ], oT[:, TQ:])
        o_ref[qi * TQ:(qi + 1) * TQ, :] = o.T.astype(BF16)

    _causal_sweep(seq, k_ref, vT_ref, queries, adjust, finish)


def _diff_kernel(lam_init, qT_ref, k_ref, vT_ref, bias_ref, lam_ref, gsub_ref, o_ref):
    seq = k_ref.shape[0]
    lp = lam_ref[...]
    lam = (jnp.exp(jnp.sum(lp[0:1] * lp[1:2], axis=1, keepdims=True))
           - jnp.exp(jnp.sum(lp[2:3] * lp[3:4], axis=1, keepdims=True)) + lam_init)

    def queries(qi):
        return _two_map_queries(qT_ref[:, qi * TQ:(qi + 1) * TQ])

    def adjust(sT, qi, kj):
        if kj >= qi - 1:
            b = bias_ref[qi - kj]
            sT = jnp.concatenate([sT[:, :TQ] + b, sT[:, TQ:] + b], axis=1)
        return sT

    def finish(qi, oT):
        o = oT[:, :TQ] - lam * oT[:, TQ:]
        y = o * lax.rsqrt(jnp.mean(o * o, axis=0, keepdims=True) + NORM_EPS) * gsub_ref[...]
        o_ref[qi * TQ:(qi + 1) * TQ, :] = y.T.astype(BF16)

    _causal_sweep(seq, k_ref, vT_ref, queries, adjust, finish)


def _mix_kernel(x_ref, fox_ref, dif_ref, wa_ref, wb_ref, g_ref, wr_hi_ref, wr_lo_ref, br_ref,
                h_ref, xm_ref, topi_ref, gate_ref, cnt_ref, carry_ref):
    @pl.when(pl.program_id(0) == 0)
    def _():
        carry_ref[...] = jnp.zeros_like(carry_ref)

    halves = [slice(0, TM // 2), slice(TM // 2, TM)]
    hs = [x_ref[rows, :] + _dot(fox_ref[rows, :], wa_ref[...]) + _dot(dif_ref[rows, :], wb_ref[...])
          for rows in halves]
    counts = carry_ref[...]
    for rows, h in zip(halves, hs):
        h_ref[rows, :] = h
        xm = h * lax.rsqrt(jnp.mean(h * h, axis=-1, keepdims=True) + NORM_EPS) * g_ref[...]
        xm_ref[rows, 0, :] = xm

        x_hi = xm.astype(BF16)
        x_lo = (xm - x_hi.astype(F32)).astype(BF16)
        logits = (_dot(x_hi, wr_hi_ref[...]) + _dot(x_hi, wr_lo_ref[...]) + _dot(x_lo, wr_hi_ref[...])
                  + br_ref[...])

        lane = lax.broadcasted_iota(jnp.int32, logits.shape, 1)
        vals, sels, idxs = [], [], []
        for _ in range(TOP_K):
            mx = jnp.max(logits, axis=1, keepdims=True)
            idx = jnp.min(jnp.where(logits == mx, lane, LANES), axis=1, keepdims=True)
            sel = lane == idx
            logits = jnp.where(sel, -jnp.inf, logits)
            vals.append(mx)
            sels.append(sel)
            idxs.append(idx)
        exps = [jnp.exp(v - vals[0]) for v in vals]
        denom = exps[0] + exps[1] + exps[2] + exps[3]

        multi_hot = sum(s.astype(F32) for s in sels)
        topi = jnp.zeros(logits.shape, jnp.int32)
        gate = jnp.zeros(logits.shape, F32)
        for k in range(TOP_K):
            topi = jnp.where(lane == k, idxs[k], topi)
            gate = jnp.where(lane == k, exps[k] / denom, gate)
        topi_ref[rows, :] = topi
        gate_ref[rows, :] = gate
        counts = counts + jnp.sum(multi_hot, axis=0, keepdims=True)
    carry_ref[...] = counts
    cnt_ref[...] = counts


def _expert_kernel(be_ref, nu_ref, off_ref, nv_ref, tok_hbm, asg_hbm, x_hbm, w1f_ref, b1_ref, w2f_ref, b2_ref,
                   perm_ref, y_hbm, tok_ref, asg_ref, x0, x1, y0, y1, w1_ref, w2_ref, h_ref, idx_sem, g_sem, s_sem):
    i = pl.program_id(0)
    n_used = nu_ref[0]
    n_blocks = be_ref.shape[0]
    spare_row = y_hbm.shape[0] - MOE_BLOCK
    xbufs, ybufs = (x0, x1), (y0, y1)

    def tok_copy(blk, slot):
        start = pl.multiple_of(off_ref[blk], IDX_ALIGN)
        return pltpu.make_async_copy(tok_hbm.at[pl.ds(start, MOE_BLOCK)], tok_ref.at[slot], idx_sem.at[0, slot])

    def asg_copy(blk, slot):
        start = pl.multiple_of(off_ref[blk], IDX_ALIGN)
        return pltpu.make_async_copy(asg_hbm.at[pl.ds(start, MOE_BLOCK)], asg_ref.at[slot], idx_sem.at[1, slot])

    def asg_ready(blk, slot):
        asg_copy(0, slot).wait()

        def fix(r, carry):
            asg_ref[slot, r] = spare_row + r
            return carry
        lax.fori_loop(nv_ref[blk], MOE_BLOCK, fix, 0)

    def row_loop(fn):
        for r in range(MOE_BLOCK):
            fn(r)

    def gather_issue(slot):
        row_loop(lambda r: pltpu.make_async_copy(x_hbm.at[tok_ref[slot, r]], xbufs[slot].at[pl.ds(r, 1)],
                                                 g_sem.at[slot]).start())

    def gather_wait(slot):
        pltpu.make_async_copy(xbufs[slot], xbufs[slot], g_sem.at[slot]).wait()

    def scatter_issue(slot):
        row_loop(lambda r: pltpu.make_async_copy(ybufs[slot].at[pl.ds(r, 1)], y_hbm.at[asg_ref[slot, r]],
                                                 s_sem.at[slot]).start())

    def scatter_wait(slot):
        pltpu.make_async_copy(ybufs[slot], ybufs[slot], s_sem.at[slot]).wait()

    @pl.when(i == 0)
    def _():
        y1[...] = jnp.zeros(y1.shape, y1.dtype)
        for r in range(MOE_BLOCK):
            asg_ref[1, r] = spare_row + r
        tok_copy(0, 0).start()
        tok_copy(0, 0).wait()
        tok_copy(jnp.minimum(1, n_blocks - 1), 1).start()
        gather_issue(0)

    @pl.when((i < n_used) & ((i == 0) | (be_ref[i] != be_ref[jnp.maximum(i - 1, 0)])))
    def _():
        for g in range(w1f_ref.shape[1] // (2 * LANES)):
            cols = slice(2 * g * LANES, 2 * (g + 1) * LANES)
            w1_ref[:, cols] = _dot(w1f_ref[:, cols].astype(BF16), perm_ref[...]).astype(BF16)
        w2_ref[...] = w2f_ref[...].astype(BF16)

    def step(cur):
        nxt = 1 - cur
        asg_copy(i, cur).start()
        tok_copy(0, nxt).wait()

        @pl.when(i >= 1)
        def _():
            asg_ready(i - 1, nxt)
            scatter_wait(cur)

        gather_wait(cur)
        gather_issue(nxt)
        scatter_issue(nxt)
        tok_copy(jnp.minimum(i + 2, n_blocks - 1), cur).start()
        xbuf, ybuf = xbufs[cur], ybufs[cur]
        h_ref[...] = _dot(xbuf[...].astype(BF16), w1_ref[...]) + b1_ref[...]

        @pl.when(n_used > 0)
        def _():
            acts = []
            for c in range(h_ref.shape[1] // (2 * LANES)):
                glu = jnp.minimum(h_ref[:, 2 * c * LANES:(2 * c + 1) * LANES], SWIGLU_LIMIT)
                lin = jnp.clip(h_ref[:, (2 * c + 1) * LANES:(2 * c + 2) * LANES], -SWIGLU_LIMIT, SWIGLU_LIMIT)
                acts.append((glu * jax.nn.sigmoid(SWIGLU_ALPHA * glu) * (lin + 1.0)).astype(BF16))
            ybuf[...] = _dot(jnp.concatenate(acts, axis=1), w2_ref[...]) + b2_ref[...]

        @pl.when(i == n_used - 1)
        def _():
            asg_ready(i, cur)
            scatter_issue(cur)
            scatter_wait(nxt)
            scatter_wait(cur)
            gather_wait(nxt)
            tok_copy(0, cur).wait()

    for parity in range(2):
        pl.when((i < n_used) & (i % 2 == parity))(functools.partial(step, parity))


def _final_kernel(h_ref, *rest):
    y_refs, (gate_ref, g_ref, wg_ref, p_ref, wp_ref, o_ref) = rest[:TOP_K], rest[TOP_K:]
    for rows in (slice(0, TM // 2), slice(TM // 2, TM)):
        gates = gate_ref[rows, :]
        moe = sum(gates[:, k:k + 1] * y_ref[rows, 0, :] for k, y_ref in enumerate(y_refs))
        h = h_ref[rows, :] + moe
        n = (h * lax.rsqrt(jnp.mean(h * h, axis=-1, keepdims=True) + NORM_EPS) * g_ref[...]).astype(BF16)
        gate = jax.nn.sigmoid(_dot(n, wg_ref[...]))
        o_ref[rows, :] = h + gate * _dot(p_ref[rows, :].astype(BF16), wp_ref[...])


def _t5_bucket(rel):
    n = jnp.maximum(rel, 0)
    max_exact = NUM_BUCKETS // 2
    nf = jnp.maximum(n, 1).astype(F32)
    large = max_exact + (jnp.log(nf / max_exact) / math.log(MAX_DISTANCE / max_exact)
                         * (NUM_BUCKETS - max_exact)).astype(jnp.int32)
    large = jnp.minimum(large, NUM_BUCKETS - 1)
    return jnp.where(n < max_exact, n, large)


def _full(shape):
    return pl.BlockSpec(shape, lambda *_: (0,) * len(shape))


def _layer(i, h_in, p_i, rel_bias, prm):
    bsz, seq, d = h_in.shape
    n_tok = bsz * seq
    ns = seq // TM
    nk = seq // TK
    x2 = h_in.reshape(n_tok, d)

    w_in = prm['w_in']
    fox_w = N_FOX_HEADS * HEAD_DIM
    offs = [0, fox_w, 2 * fox_w, 3 * fox_w, 3 * fox_w + N_FOX_HEADS]
    offs += [offs[4] + GROUP, offs[4] + 2 * GROUP, offs[4] + 3 * GROUP]
    col = lambda a, b: w_in[:, a:b].astype(BF16)
    wfqT, wfk, wfvT = col(offs[0], offs[1]).T, col(offs[1], offs[2]), col(offs[2], offs[3]).T
    wdqT, wdk, wdvT = col(offs[4], offs[5]).T, col(offs[5], offs[6]), col(offs[6], offs[7]).T
    wf = jnp.zeros((d, LANES), BF16).at[:, :N_FOX_HEADS].set(col(offs[3], offs[4]))
    bf = jnp.zeros((1, LANES), F32).at[0, :N_FOX_HEADS].set(prm['b_f'])
    scale = HEAD_DIM ** -0.5 * LOG2E
    rep = GROUP // HEAD_DIM
    heads = jnp.arange(N_FOX_HEADS)
    place = jnp.concatenate([jnp.zeros((LANES, GROUP), BF16)
                             .at[heads, (heads // 2) * LANES + (heads % 2) * N_SPLIT + t].set(1.0)
                             for t in range(N_SPLIT)])
    gfqT = jnp.broadcast_to((jnp.tile(prm['g_fox_q'], rep) * scale)[:, None], (GROUP, TM))
    gdqT = jnp.broadcast_to((jnp.tile(prm['g_diff_q'], rep) * scale)[:, None], (GROUP, TM))
    gfk = jnp.tile(prm['g_fox_k'], rep)[None, :]
    gdk = jnp.tile(prm['g_diff_k'], rep)[None, :]
    gi = jnp.arange(GROUP // 2) // HEAD_DIM
    seg = jnp.where(gi[:, None] == gi[None, :], 1.0 / HEAD_DIM, 0.0).astype(BF16)
    ti = jnp.arange(TM)
    tri_incl = (ti[None, :] <= ti[:, None]).astype(BF16)

    w_spec_t = _full((GROUP, d))
    w_spec = _full((d, GROUP))
    qT_shape = jax.ShapeDtypeStruct((bsz, GROUP, seq), BF16)
    k_shape = jax.ShapeDtypeStruct((n_tok, GROUP), BF16)
    vT_shape = jax.ShapeDtypeStruct((bsz, GROUP // LANES, nk, LANES, TK), BF16)
    qT_spec = pl.BlockSpec((None, GROUP, TM), lambda b, s: (b, 0, s))
    k_spec = pl.BlockSpec((TM, GROUP), lambda b, s: (b * ns + s, 0))
    vT_spec = pl.BlockSpec((None, GROUP // LANES, TM // TK, LANES, TK), lambda b, s: (b, 0, s, 0, 0))
    fk_shape = jax.ShapeDtypeStruct((n_tok, 2 * GROUP), BF16)
    fk_spec = pl.BlockSpec((TM, 2 * GROUP), lambda b, s: (b * ns + s, 0))
    fqT, fk, fvT, dqT, dk, dvT = pl.pallas_call(
        _inproj_kernel,
        grid=(bsz, ns),
        in_specs=[pl.BlockSpec((TM, d), lambda b, s: (b * ns + s, 0)), _full((1, d)),
                  w_spec_t, w_spec, w_spec_t, w_spec_t, w_spec, w_spec_t, _full((d, LANES)), _full((1, LANES)),
                  _full((GROUP, TM)), _full((1, GROUP)), _full((GROUP, TM)), _full((1, GROUP)),
                  _full((GROUP // 2, GROUP // 2)), _full((TM, TM)), _full((N_SPLIT * LANES, GROUP))],
        out_specs=[qT_spec, fk_spec, vT_spec, qT_spec, k_spec, vT_spec],
        out_shape=[qT_shape, fk_shape, vT_shape, qT_shape, k_shape, vT_shape],
        scratch_shapes=[pltpu.VMEM((1, LANES), F32)],
        compiler_params=_params(2),
        name="inproj",
    )(x2, prm['g_attn'][None, :], wfqT, wfk, wfvT, wdqT, wdk, wdvT, wf, bf, gfqT, gfk, gdqT, gdk, seg, tri_incl,
      place)

    n_hb = GROUP // LANES
    att_q = pl.BlockSpec((None, LANES, seq), lambda b, h: (b, h, 0))
    att_k = pl.BlockSpec((seq, LANES), lambda b, h: (b, h))
    att_v = pl.BlockSpec((None, None, nk, LANES, TK), lambda b, h: (b, h, 0, 0, 0))
    att_o = pl.BlockSpec((seq, LANES), lambda b, h: (b, h))
    fox_out = pl.pallas_call(
        _fox_kernel,
        grid=(bsz, n_hb),
        in_specs=[att_q, pl.BlockSpec((seq, 2 * LANES), lambda b, h: (b, h)), att_v],
        out_specs=att_o,
        out_shape=jax.ShapeDtypeStruct((n_tok, GROUP), BF16),
        compiler_params=_params(2),
        name="fox_attention",
    )(fqT, fk, fvT)

    kpos = jnp.arange(TK)[:, None]
    qpos = jnp.arange(TQ)[None, :]
    rel = jnp.stack([qpos - kpos, qpos - kpos + TK])
    assert TK + 1 >= MAX_DISTANCE and TQ == TK
    table = (rel_bias.astype(F32) - rel_bias[NUM_BUCKETS - 1].astype(F32)[None, :]) * LOG2E
    onehot = _t5_bucket(rel)[None, ..., None] == jnp.arange(NUM_BUCKETS, dtype=jnp.int32)
    biasT = jnp.sum(jnp.where(onehot, table.T[:, None, None, None, :], 0.0), axis=-1)
    biasT = jnp.where((rel >= 0)[None], biasT, NEG_INF)
    lam_init = 0.8 - 0.6 * math.exp(-0.3 * i)
    lam_p = jnp.stack([prm['lambda_q1'], prm['lambda_k1'], prm['lambda_q2'], prm['lambda_k2']]).astype(F32)
    gsubT = jnp.broadcast_to((prm['g_subln'] * (1.0 - lam_init))[:, None], (LANES, TQ))
    diff_out = pl.pallas_call(
        functools.partial(_diff_kernel, lam_init),
        grid=(bsz, N_DIFF_HEADS),
        in_specs=[att_q, att_k, att_v, pl.BlockSpec((None, 2, TK, TQ), lambda b, h: (h, 0, 0, 0)),
                  _full((4, HEAD_DIM)), _full((LANES, TQ))],
        out_specs=att_o,
        out_shape=jax.ShapeDtypeStruct((n_tok, GROUP), BF16),
        compiler_params=_params(2),
        name="diff_attention",
    )(dqT, dk, dvT, biasT, lam_p, gsubT)

    w_out = prm['w_out'].astype(BF16)
    wr = jnp.zeros((d, LANES), F32).at[:, :N_EXPERTS].set(prm['w_router'])
    wr_hi = wr.astype(BF16)
    wr_lo = (wr - wr_hi.astype(F32)).astype(BF16)
    br = jnp.full((1, LANES), NEG_INF, F32).at[0, :N_EXPERTS].set(prm['b_router'])
    row_spec = lambda w: pl.BlockSpec((TM, w), lambda t: (t, 0))
    h1, xm, topi, gates, counts = pl.pallas_call(
        _mix_kernel,
        grid=(n_tok // TM,),
        in_specs=[row_spec(d), row_spec(GROUP), row_spec(GROUP), _full((GROUP, d)), _full((GROUP, d)),
                  _full((1, d)), _full((d, LANES)), _full((d, LANES)), _full((1, LANES))],
        out_specs=[row_spec(d), pl.BlockSpec((TM, 1, d), lambda t: (t, 0, 0)), row_spec(LANES),
                   row_spec(LANES), _full((1, LANES))],
        out_shape=[jax.ShapeDtypeStruct((n_tok, d), F32), jax.ShapeDtypeStruct((n_tok, 1, d), F32),
                   jax.ShapeDtypeStruct((n_tok, LANES), jnp.int32), jax.ShapeDtypeStruct((n_tok, LANES), F32),
                   jax.ShapeDtypeStruct((1, LANES), F32)],
        scratch_shapes=[pltpu.VMEM((1, LANES), F32)],
        compiler_params=_params(1),
        name="mix_router",
    )(x2, fox_out, diff_out, w_out[:GROUP], w_out[GROUP:], prm['g_mlp'][None, :], wr_hi, wr_lo, br)

    m = n_tok * TOP_K
    n_blocks = (m + N_EXPERTS * MOE_BLOCK + MOE_BLOCK - 1) // MOE_BLOCK
    cnt = counts[0, :N_EXPERTS].astype(jnp.int32)
    padded = ((cnt + MOE_BLOCK - 1) // MOE_BLOCK) * MOE_BLOCK
    pad_ends = jnp.cumsum(padded)
    pad_starts = pad_ends - padded
    block_start = jnp.arange(n_blocks, dtype=jnp.int32) * MOE_BLOCK
    block_expert = jnp.minimum(jnp.sum(pad_ends[None, :] <= block_start[:, None], axis=1),
                               N_EXPERTS - 1).astype(jnp.int32)
    n_used = (pad_ends[-1] // MOE_BLOCK).astype(jnp.int32)[None]
    experts = jnp.arange(N_EXPERTS, dtype=jnp.int32)
    real = topi[:, :TOP_K] * (2 * m) + jnp.arange(m, dtype=jnp.int32).reshape(n_tok, TOP_K)
    fill = jnp.arange(IDX_ALIGN - 1, dtype=jnp.int32)[None, :]
    need = (-cnt) % IDX_ALIGN
    filler = jnp.where(fill < need[:, None], experts[:, None] * (2 * m) + m + fill,
                       N_EXPERTS * (2 * m) + experts[:, None] * IDX_ALIGN + fill)
    keys = jnp.sort(jnp.concatenate([real.reshape(m), filler.reshape(-1)]))
    flat = jnp.concatenate([(keys % (2 * m)) % m, jnp.zeros((MOE_BLOCK,), jnp.int32)])
    grouped_tok = flat // TOP_K
    grouped_row = (flat % TOP_K) * n_tok + grouped_tok
    onehot_e = block_expert[:, None] == experts[None, :]
    pick = lambda v: jnp.sum(jnp.where(onehot_e, v[None, :], 0), axis=1)
    in_group = block_start - pick(pad_starts)
    n_valid = jnp.clip(pick(cnt) - in_group, 0, MOE_BLOCK)
    group_start = jnp.cumsum(cnt + need) - (cnt + need)
    first = jnp.clip(pick(group_start) + in_group, 0, keys.shape[0]) // IDX_ALIGN * IDX_ALIGN

    d_ff = prm['w2'].shape[1]
    n_grp = 2 * d_ff // (2 * LANES)
    b1p = prm['b1'].reshape(N_EXPERTS, n_grp, LANES, 2).transpose(0, 1, 3, 2).reshape(N_EXPERTS, 1, 2 * d_ff)
    b2 = prm['b2'].reshape(N_EXPERTS, 1, d)
    pj = jnp.arange(LANES)
    perm = (jnp.zeros((2 * LANES, 2 * LANES), BF16).at[2 * pj, pj].set(1.0)
            .at[2 * pj + 1, LANES + pj].set(1.0))
    any_spec = pl.BlockSpec(memory_space=pl.ANY)
    ys = pl.pallas_call(
        _expert_kernel,
        grid_spec=pltpu.PrefetchScalarGridSpec(
            num_scalar_prefetch=4,
            grid=(n_blocks,),
            in_specs=[any_spec, any_spec, any_spec,
                      pl.BlockSpec((None, d, 2 * d_ff), lambda t, be, *_:(be[t], 0, 0)),
                      pl.BlockSpec((None, 1, 2 * d_ff), lambda t, be, *_:(be[t], 0, 0)),
                      pl.BlockSpec((None, d_ff, d), lambda t, be, *_:(be[t], 0, 0)),
                      pl.BlockSpec((None, 1, d), lambda t, be, *_:(be[t], 0, 0)),
                      pl.BlockSpec((2 * LANES, 2 * LANES), lambda t, be, *_:(0, 0))],
            out_specs=any_spec,
            scratch_shapes=[pltpu.SMEM((2, MOE_BLOCK), jnp.int32), pltpu.SMEM((2, MOE_BLOCK), jnp.int32)]
            + [pltpu.VMEM((MOE_BLOCK, d), F32)] * 4
            + [pltpu.VMEM((d, 2 * d_ff), BF16), pltpu.VMEM((d_ff, d), BF16), pltpu.VMEM((MOE_BLOCK, 2 * d_ff), F32)]
            + [pltpu.SemaphoreType.DMA((2, 2)), pltpu.SemaphoreType.DMA((2,)), pltpu.SemaphoreType.DMA((2,))],
        ),
        out_shape=jax.ShapeDtypeStruct((m + MOE_BLOCK, 1, d), F32),
        compiler_params=pltpu.CompilerParams(dimension_semantics=("arbitrary",),
                                             vmem_limit_bytes=EXPERT_VMEM_LIMIT),
        name="experts",
    )(block_expert, n_used, first, n_valid, grouped_tok, grouped_row, xm, prm['w1'], b1p, prm['w2'], b2, perm)

    ple = p_i.shape[-1]
    out = pl.pallas_call(
        _final_kernel,
        grid=(n_tok // TM,),
        in_specs=[row_spec(d)]
        + [pl.BlockSpec((TM, 1, d), functools.partial(lambda k, t: (k * (n_tok // TM) + t, 0, 0), k))
           for k in range(TOP_K)]
        + [row_spec(LANES),
                  _full((1, d)), _full((d, d)), row_spec(ple), _full((ple, d))],
        out_specs=row_spec(d),
        out_shape=jax.ShapeDtypeStruct((n_tok, d), F32),
        compiler_params=_params(1),
        name="combine_ple",
    )(h1, ys, ys, ys, ys, gates, prm['g_ple'][None, :],
      prm['w_ple_gate'].astype(BF16), p_i.reshape(n_tok, ple), prm['w_ple_proj'].astype(BF16))
    return out.reshape(bsz, seq, d)


def kernel(x, p, rel_bias, g_attn, w_in, b_f, g_fox_q, g_fox_k, g_diff_q, g_diff_k, lambda_q1, lambda_k1,
           lambda_q2, lambda_k2, g_subln, w_out, g_mlp, w_router, b_router, w1, b1, w2, b2, g_ple,
           w_ple_gate, w_ple_proj):
    stacked = dict(g_attn=g_attn, w_in=w_in, b_f=b_f, g_fox_q=g_fox_q, g_fox_k=g_fox_k, g_diff_q=g_diff_q,
                   g_diff_k=g_diff_k, lambda_q1=lambda_q1, lambda_k1=lambda_k1, lambda_q2=lambda_q2,
                   lambda_k2=lambda_k2, g_subln=g_subln, w_out=w_out, g_mlp=g_mlp, w_router=w_router,
                   b_router=b_router, w1=w1, b1=b1, w2=w2, b2=b2, g_ple=g_ple, w_ple_gate=w_ple_gate,
                   w_ple_proj=w_ple_proj)
    h = x
    for i in range(p.shape[0]):
        h = _layer(i, h, p[i], rel_bias, {name: v[i] for name, v in stacked.items()})
    return h
```

```python
import functools
import math

import jax
import jax.numpy as jnp
from jax import lax
from jax.experimental import pallas as pl
from jax.experimental.pallas import tpu as pltpu

F32 = jnp.float32
BF16 = jnp.bfloat16

HEAD_DIM = 64
N_FOX_HEADS = 8
N_DIFF_HEADS = 4
GROUP = 512
LANES = 128
NUM_BUCKETS = 32
MAX_DISTANCE = 128
N_EXPERTS = 32
TOP_K = 4
SWIGLU_ALPHA = 1.702
SWIGLU_LIMIT = 7.0
MOE_BLOCK = 256
IDX_ALIGN = 128
NORM_EPS = 1e-6
NEG_INF = -1e30
LOG2E = math.log2(math.e)
N_SPLIT = 3

TM = 512
TQ = 256
TK = 256
LOOKAHEAD = 2
VMEM_LIMIT = 48 * 1024 * 1024
EXPERT_VMEM_LIMIT = 58 * 1024 * 1024


def _params(n_axes):
    return pltpu.CompilerParams(dimension_semantics=("arbitrary",) * n_axes,
                                vmem_limit_bytes=VMEM_LIMIT)


def _dot(a, b):
    return jnp.dot(a, b, preferred_element_type=F32)


def _dot_nt(a, b):
    return lax.dot_general(a, b, (((1,), (1,)), ((), ())), preferred_element_type=F32)


def _split3(v):
    hi = v.astype(BF16)
    r = v - hi.astype(F32)
    mid = r.astype(BF16)
    lo = (r - mid.astype(F32)).astype(BF16)
    return hi, mid, lo


def _inproj_kernel(x_ref, g_ref, wfq_ref, wfk_ref, wfv_ref, wdq_ref, wdk_ref, wdv_ref, wf_ref, bf_ref,
                   gfq_ref, gfk_ref, gdq_ref, gdk_ref, seg_ref, tri_ref, place_ref,
                   fq_ref, fk_ref, fv_ref, dq_ref, dk_ref, dv_ref, carry_ref):
    @pl.when(pl.program_id(1) == 0)
    def _():
        carry_ref[...] = jnp.zeros_like(carry_ref)

    xf = x_ref[...]
    a = (xf * lax.rsqrt(jnp.mean(xf * xf, axis=-1, keepdims=True) + NORM_EPS) * g_ref[...]).astype(BF16)
    seg = seg_ref[...]

    fl = _dot(a, wf_ref[...]) + bf_ref[...]
    fq_acc = _dot_nt(wfq_ref[...], a)
    dq_acc = _dot_nt(wdq_ref[...], a)
    fk_acc = _dot(a, wfk_ref[...])
    dk_acc = _dot(a, wdk_ref[...])
    fv_acc = _dot_nt(wfv_ref[...], a)
    dv_acc = _dot_nt(wdv_ref[...], a)

    half = seg.shape[0]

    def q_norm(acc, gain_ref):
        sq = (acc * acc).astype(BF16)
        ms = jnp.concatenate([_dot(seg, sq[:half]), _dot(seg, sq[half:])], axis=0)
        return (acc * lax.rsqrt(ms + NORM_EPS) * gain_ref[...]).astype(BF16)

    def k_norm(acc, gain_ref):
        sq = (acc * acc).astype(BF16)
        ms = jnp.concatenate([_dot(sq[:, :half], seg), _dot(sq[:, half:], seg)], axis=1)
        return (acc * lax.rsqrt(ms + NORM_EPS) * gain_ref[...]).astype(BF16)

    def v_store(acc, out_ref):
        acc = acc.astype(BF16)
        for hb in range(GROUP // LANES):
            for j in range(TM // TK):
                out_ref[hb, j] = acc[hb * LANES:(hb + 1) * LANES, j * TK:(j + 1) * TK]

    logf = jnp.minimum(fl, 0.0) - jnp.log1p(jnp.exp(-jnp.abs(fl)))
    sums = _dot(tri_ref[...], jnp.concatenate(_split3(logf), axis=1))
    cs = carry_ref[...] + sums[:, :LANES] + sums[:, LANES:2 * LANES] + sums[:, 2 * LANES:]
    carry_ref[...] = cs[TM - 1:TM, :]

    fq_ref[...] = q_norm(fq_acc, gfq_ref)
    dq_ref[...] = q_norm(dq_acc, gdq_ref)
    fk = k_norm(fk_acc, gfk_ref)
    dk_ref[...] = k_norm(dk_acc, gdk_ref)
    v_store(fv_acc, fv_ref)
    v_store(dv_acc, dv_ref)

    extra = _dot(jnp.concatenate(_split3(cs * LOG2E), axis=1), place_ref[...]).astype(BF16)
    for hb in range(GROUP // LANES):
        fk_ref[:, 2 * hb * LANES:(2 * hb + 1) * LANES] = fk[:, hb * LANES:(hb + 1) * LANES]
        fk_ref[:, (2 * hb + 1) * LANES:(2 * hb + 2) * LANES] = extra[:, hb * LANES:(hb + 1) * LANES]


def _two_map_queries(qT_blk):
    row = lax.broadcasted_iota(jnp.int32, qT_blk.shape, 0)
    zero = jnp.zeros_like(qT_blk)
    return jnp.concatenate([jnp.where(row < HEAD_DIM, qT_blk, zero),
                            jnp.where(row >= HEAD_DIM, qT_blk, zero)], axis=1)


def _softmax_step(sT, vT_blk, carry):
    m, l, acc = carry
    m_new = jnp.maximum(m, jnp.max(sT, axis=0, keepdims=True))
    alpha = jnp.exp2(m - m_new)
    pT = jnp.exp2(sT - m_new)
    l = alpha * l + jnp.sum(pT, axis=0, keepdims=True)
    acc = alpha * acc + _dot(vT_blk, pT.astype(BF16))
    return m_new, l, acc


def _causal_keep():
    key = lax.broadcasted_iota(jnp.int32, (TK, 2 * TQ), 0)
    col = lax.broadcasted_iota(jnp.int32, (TK, 2 * TQ), 1)
    qry = jnp.where(col >= TQ, col - TQ, col)
    return key <= qry


def _init_carry():
    return (jnp.full((1, 2 * TQ), NEG_INF, F32), jnp.zeros((1, 2 * TQ), F32),
            jnp.zeros((LANES, 2 * TQ), F32))


def _causal_sweep(seq, k_ref, vT_ref, queries, adjust, finish):
    pairs = [(qi, kj) for qi in range(seq // TQ) for kj in range(qi + 1)]
    q_ops = {}

    def score(qi, kj):
        if qi not in q_ops:
            q_ops[qi] = queries(qi)
        return _dot(k_ref[kj * TK:(kj + 1) * TK, :], q_ops[qi])

    pending = [score(*pr) for pr in pairs[:LOOKAHEAD]]
    carry = None
    for n, (qi, kj) in enumerate(pairs):
        if n + LOOKAHEAD < len(pairs):
            pending.append(score(*pairs[n + LOOKAHEAD]))
        if kj == 0:
            carry = _init_carry()
        carry = _softmax_step(adjust(pending.pop(0), qi, kj), vT_ref[kj], carry)
        if kj == qi:
            m, l, acc = carry
            finish(qi, acc * (1.0 / l))


def _fox_kernel(qT_ref, k_ref, vT_ref, o_ref):
    seq = k_ref.shape[0]
    keep = _causal_keep()
    row = lax.broadcasted_iota(jnp.int32, (LANES, TQ), 0)
    arow = lax.broadcasted_iota(jnp.int32, (LANES, 2 * TQ), 0)
    acol = lax.broadcasted_iota(jnp.int32, (LANES, 2 * TQ), 1)
    first = (acol < TQ) & (arow < N_SPLIT)
    second = (acol >= TQ) & (arow >= N_SPLIT) & (arow < 2 * N_SPLIT)
    minus_c = jnp.where(first | second, -1.0, 0.0).astype(BF16)

    def queries(qi):
        return jnp.concatenate([_two_map_queries(qT_ref[:, qi * TQ:(qi + 1) * TQ]), minus_c], axis=0)

    def adjust(sT, qi, kj):
        return jnp.where(keep, sT, NEG_INF) if kj == qi else sT

    def finish(qi, oT):
        o = jnp.where(row < HEAD_DIM, oT[:, :TQ], oT[:, TQ:])
        o_ref[qi * TQ:(qi + 1) * TQ, :] = o.T.astype(BF16)

    _causal_sweep(seq, k_ref, vT_ref, queries, adjust, finish)


def _diff_kernel(lam_init, qT_ref, k_ref, vT_ref, bias_ref, lam_ref, gsub_ref, o_ref):
    seq = k_ref.shape[0]
    lp = lam_ref[...]
    lam = (jnp.exp(jnp.sum(lp[0:1] * lp[1:2], axis=1, keepdims=True))
           - jnp.exp(jnp.sum(lp[2:3] * lp[3:4], axis=1, keepdims=True)) + lam_init)

    def queries(qi):
        return _two_map_queries(qT_ref[:, qi * TQ:(qi + 1) * TQ])

    def adjust(sT, qi, kj):
        if kj >= qi - 1:
            b = bias_ref[qi - kj]
            sT = jnp.concatenate([sT[:, :TQ] + b, sT[:, TQ:] + b], axis=1)
        return sT

    def finish(qi, oT):
        o = oT[:, :TQ] - lam * oT[:, TQ:]
        y = o * lax.rsqrt(jnp.mean(o * o, axis=0, keepdims=True) + NORM_EPS) * gsub_ref[...]
        o_ref[qi * TQ:(qi + 1) * TQ, :] = y.T.astype(BF16)

    _causal_sweep(seq, k_ref, vT_ref, queries, adjust, finish)


def _mix_kernel(x_ref, fox_ref, dif_ref, wa_ref, wb_ref, g_ref, wr_hi_ref, wr_lo_ref, br_ref,
                h_ref, xm_ref, topi_ref, gate_ref, cnt_ref, carry_ref):
    @pl.when(pl.program_id(0) == 0)
    def _():
        carry_ref[...] = jnp.zeros_like(carry_ref)

    halves = [slice(0, TM // 2), slice(TM // 2, TM)]
    hs = [x_ref[rows, :] + _dot(fox_ref[rows, :], wa_ref[...]) + _dot(dif_ref[rows, :], wb_ref[...])
          for rows in halves]
    counts = carry_ref[...]
    for rows, h in zip(halves, hs):
        h_ref[rows, :] = h
        xm = h * lax.rsqrt(jnp.mean(h * h, axis=-1, keepdims=True) + NORM_EPS) * g_ref[...]
        xm_ref[rows, 0, :] = xm

        x_hi = xm.astype(BF16)
        x_lo = (xm - x_hi.astype(F32)).astype(BF16)
        logits = (_dot(x_hi, wr_hi_ref[...]) + _dot(x_hi, wr_lo_ref[...]) + _dot(x_lo, wr_hi_ref[...])
                  + br_ref[...])

        lane = lax.broadcasted_iota(jnp.int32, logits.shape, 1)
        vals, sels, idxs = [], [], []
        for _ in range(TOP_K):
            mx = jnp.max(logits, axis=1, keepdims=True)
            idx = jnp.min(jnp.where(logits == mx, lane, LANES), axis=1, keepdims=True)
            sel = lane == idx
            logits = jnp.where(sel, -jnp.inf, logits)
            vals.append(mx)
            sels.append(sel)
            idxs.append(idx)
        exps = [jnp.exp(v - vals[0]) for v in vals]
        denom = exps[0] + exps[1] + exps[2] + exps[3]

        multi_hot = sum(s.astype(F32) for s in sels)
        topi = jnp.zeros(logits.shape, jnp.int32)
        gate = jnp.zeros(logits.shape, F32)
        for k in range(TOP_K):
            topi = jnp.where(lane == k, idxs[k], topi)
            gate = jnp.where(lane == k, exps[k] / denom, gate)
        topi_ref[rows, :] = topi
        gate_ref[rows, :] = gate
        counts = counts + jnp.sum(multi_hot, axis=0, keepdims=True)
    carry_ref[...] = counts
    cnt_ref[...] = counts


def _expert_kernel(be_ref, nu_ref, off_ref, nv_ref, tok_hbm, asg_hbm, x_hbm, w1f_ref, b1_ref, w2f_ref, b2_ref,
                   perm_ref, y_hbm, tok_ref, asg_ref, x0, x1, y0, y1, w1_ref, w2_ref, h_ref, idx_sem, g_sem, s_sem):
    i = pl.program_id(0)
    n_used = nu_ref[0]
    n_blocks = be_ref.shape[0]
    spare_row = y_hbm.shape[0] - MOE_BLOCK
    xbufs, ybufs = (x0, x1), (y0, y1)

    def tok_copy(blk, slot):
        start = pl.multiple_of(off_ref[blk], IDX_ALIGN)
        return pltpu.make_async_copy(tok_hbm.at[pl.ds(start, MOE_BLOCK)], tok_ref.at[slot], idx_sem.at[0, slot])

    def asg_copy(blk, slot):
        start = pl.multiple_of(off_ref[blk], IDX_ALIGN)
        return pltpu.make_async_copy(asg_hbm.at[pl.ds(start, MOE_BLOCK)], asg_ref.at[slot], idx_sem.at[1, slot])

    def asg_ready(blk, slot):
        asg_copy(0, slot).wait()

        def fix(r, carry):
            asg_ref[slot, r] = spare_row + r
            return carry
        lax.fori_loop(nv_ref[blk], MOE_BLOCK, fix, 0)

    def row_loop(fn):
        for r in range(MOE_BLOCK):
            fn(r)

    def gather_issue(slot):
        row_loop(lambda r: pltpu.make_async_copy(x_hbm.at[tok_ref[slot, r]], xbufs[slot].at[pl.ds(r, 1)],
                                                 g_sem.at[slot]).start())

    def gather_wait(slot):
        pltpu.make_async_copy(xbufs[slot], xbufs[slot], g_sem.at[slot]).wait()

    def scatter_issue(slot):
        row_loop(lambda r: pltpu.make_async_copy(ybufs[slot].at[pl.ds(r, 1)], y_hbm.at[asg_ref[slot, r]],
                                                 s_sem.at[slot]).start())

    def scatter_wait(slot):
        pltpu.make_async_copy(ybufs[slot], ybufs[slot], s_sem.at[slot]).wait()

    @pl.when(i == 0)
    def _():
        y1[...] = jnp.zeros(y1.shape, y1.dtype)
        for r in range(MOE_BLOCK):
            asg_ref[1, r] = spare_row + r
        tok_copy(0, 0).start()
        tok_copy(0, 0).wait()
        tok_copy(jnp.minimum(1, n_blocks - 1), 1).start()
        gather_issue(0)

    @pl.when((i < n_used) & ((i == 0) | (be_ref[i] != be_ref[jnp.maximum(i - 1, 0)])))
    def _():
        for g in range(w1f_ref.shape[1] // (2 * LANES)):
            cols = slice(2 * g * LANES, 2 * (g + 1) * LANES)
            w1_ref[:, cols] = _dot(w1f_ref[:, cols].astype(BF16), perm_ref[...]).astype(BF16)
        w2_ref[...] = w2f_ref[...].astype(BF16)

    def step(cur):
        nxt = 1 - cur
        asg_copy(i, cur).start()
        tok_copy(0, nxt).wait()

        pl.when(i >= 1)(functools.partial(asg_ready, i - 1, nxt))

        gather_wait(cur)
        gather_issue(nxt)
        tok_copy(jnp.minimum(i + 2, n_blocks - 1), cur).start()
        xbuf, ybuf = xbufs[cur], ybufs[cur]
        h_ref[...] = _dot(xbuf[...].astype(BF16), w1_ref[...]) + b1_ref[...]

        pl.when(i >= 1)(functools.partial(scatter_wait, cur))

        @pl.when(n_used > 0)
        def _():
            scatter_issue(nxt)
            acts = []
            for c in range(h_ref.shape[1] // (2 * LANES)):
                glu = jnp.minimum(h_ref[:, 2 * c * LANES:(2 * c + 1) * LANES], SWIGLU_LIMIT)
                lin = jnp.clip(h_ref[:, (2 * c + 1) * LANES:(2 * c + 2) * LANES], -SWIGLU_LIMIT, SWIGLU_LIMIT)
                acts.append((glu * jax.nn.sigmoid(SWIGLU_ALPHA * glu) * (lin + 1.0)).astype(BF16))
            ybuf[...] = _dot(jnp.concatenate(acts, axis=1), w2_ref[...]) + b2_ref[...]

        @pl.when(i == n_used - 1)
        def _():
            asg_ready(i, cur)
            scatter_issue(cur)
            scatter_wait(nxt)
            scatter_wait(cur)
            gather_wait(nxt)
            tok_copy(0, cur).wait()

    for parity in range(2):
        pl.when((i < n_used) & (i % 2 == parity))(functools.partial(step, parity))


def _final_kernel(h_ref, *rest):
    y_refs, (gate_ref, g_ref, wg_ref, p_ref, wp_ref, o_ref) = rest[:TOP_K], rest[TOP_K:]
    for rows in (slice(0, TM // 2), slice(TM // 2, TM)):
        gates = gate_ref[rows, :]
        moe = sum(gates[:, k:k + 1] * y_ref[rows, 0, :] for k, y_ref in enumerate(y_refs))
        h = h_ref[rows, :] + moe
        n = (h * lax.rsqrt(jnp.mean(h * h, axis=-1, keepdims=True) + NORM_EPS) * g_ref[...]).astype(BF16)
        gate = jax.nn.sigmoid(_dot(n, wg_ref[...]))
        o_ref[rows, :] = h + gate * _dot(p_ref[rows, :].astype(BF16), wp_ref[...])


def _t5_bucket(rel):
    n = jnp.maximum(rel, 0)
    max_exact = NUM_BUCKETS // 2
    nf = jnp.maximum(n, 1).astype(F32)
    large = max_exact + (jnp.log(nf / max_exact) / math.log(MAX_DISTANCE / max_exact)
                         * (NUM_BUCKETS - max_exact)).astype(jnp.int32)
    large = jnp.minimum(large, NUM_BUCKETS - 1)
    return jnp.where(n < max_exact, n, large)


def _full(shape):
    return pl.BlockSpec(shape, lambda *_: (0,) * len(shape))


def _layer(i, h_in, p_i, rel_bias, prm):
    bsz, seq, d = h_in.shape
    n_tok = bsz * seq
    ns = seq // TM
    nk = seq // TK
    x2 = h_in.reshape(n_tok, d)

    w_in = prm['w_in']
    fox_w = N_FOX_HEADS * HEAD_DIM
    offs = [0, fox_w, 2 * fox_w, 3 * fox_w, 3 * fox_w + N_FOX_HEADS]
    offs += [offs[4] + GROUP, offs[4] + 2 * GROUP, offs[4] + 3 * GROUP]
    col = lambda a, b: w_in[:, a:b].astype(BF16)
    wfqT, wfk, wfvT = col(offs[0], offs[1]).T, col(offs[1], offs[2]), col(offs[2], offs[3]).T
    wdqT, wdk, wdvT = col(offs[4], offs[5]).T, col(offs[5], offs[6]), col(offs[6], offs[7]).T
    wf = jnp.zeros((d, LANES), BF16).at[:, :N_FOX_HEADS].set(col(offs[3], offs[4]))
    bf = jnp.zeros((1, LANES), F32).at[0, :N_FOX_HEADS].set(prm['b_f'])
    scale = HEAD_DIM ** -0.5 * LOG2E
    rep = GROUP // HEAD_DIM
    heads = jnp.arange(N_FOX_HEADS)
    place = jnp.concatenate([jnp.zeros((LANES, GROUP), BF16)
                             .at[heads, (heads // 2) * LANES + (heads % 2) * N_SPLIT + t].set(1.0)
                             for t in range(N_SPLIT)])
    gfqT = jnp.broadcast_to((jnp.tile(prm['g_fox_q'], rep) * scale)[:, None], (GROUP, TM))
    gdqT = jnp.broadcast_to((jnp.tile(prm['g_diff_q'], rep) * scale)[:, None], (GROUP, TM))
    gfk = jnp.tile(prm['g_fox_k'], rep)[None, :]
    gdk = jnp.tile(prm['g_diff_k'], rep)[None, :]
    gi = jnp.arange(GROUP // 2) // HEAD_DIM
    seg = jnp.where(gi[:, None] == gi[None, :], 1.0 / HEAD_DIM, 0.0).astype(BF16)
    ti = jnp.arange(TM)
    tri_incl = (ti[None, :] <= ti[:, None]).astype(BF16)

    w_spec_t = _full((GROUP, d))
    w_spec = _full((d, GROUP))
    qT_shape = jax.ShapeDtypeStruct((bsz, GROUP, seq), BF16)
    k_shape = jax.ShapeDtypeStruct((n_tok, GROUP), BF16)
    vT_shape = jax.ShapeDtypeStruct((bsz, GROUP // LANES, nk, LANES, TK), BF16)
    qT_spec = pl.BlockSpec((None, GROUP, TM), lambda b, s: (b, 0, s))
    k_spec = pl.BlockSpec((TM, GROUP), lambda b, s: (b * ns + s, 0))
    vT_spec = pl.BlockSpec((None, GROUP // LANES, TM // TK, LANES, TK), lambda b, s: (b, 0, s, 0, 0))
    fk_shape = jax.ShapeDtypeStruct((n_tok, 2 * GROUP), BF16)
    fk_spec = pl.BlockSpec((TM, 2 * GROUP), lambda b, s: (b * ns + s, 0))
    fqT, fk, fvT, dqT, dk, dvT = pl.pallas_call(
        _inproj_kernel,
        grid=(bsz, ns),
        in_specs=[pl.BlockSpec((TM, d), lambda b, s: (b * ns + s, 0)), _full((1, d)),
                  w_spec_t, w_spec, w_spec_t, w_spec_t, w_spec, w_spec_t, _full((d, LANES)), _full((1, LANES)),
                  _full((GROUP, TM)), _full((1, GROUP)), _full((GROUP, TM)), _full((1, GROUP)),
                  _full((GROUP // 2, GROUP // 2)), _full((TM, TM)), _full((N_SPLIT * LANES, GROUP))],
        out_specs=[qT_spec, fk_spec, vT_spec, qT_spec, k_spec, vT_spec],
        out_shape=[qT_shape, fk_shape, vT_shape, qT_shape, k_shape, vT_shape],
        scratch_shapes=[pltpu.VMEM((1, LANES), F32)],
        compiler_params=_params(2),
        name="inproj",
    )(x2, prm['g_attn'][None, :], wfqT, wfk, wfvT, wdqT, wdk, wdvT, wf, bf, gfqT, gfk, gdqT, gdk, seg, tri_incl,
      place)

    n_hb = GROUP // LANES
    att_q = pl.BlockSpec((None, LANES, seq), lambda b, h: (b, h, 0))
    att_k = pl.BlockSpec((seq, LANES), lambda b, h: (b, h))
    att_v = pl.BlockSpec((None, None, nk, LANES, TK), lambda b, h: (b, h, 0, 0, 0))
    att_o = pl.BlockSpec((seq, LANES), lambda b, h: (b, h))
    fox_out = pl.pallas_call(
        _fox_kernel,
        grid=(bsz, n_hb),
        in_specs=[att_q, pl.BlockSpec((seq, 2 * LANES), lambda b, h: (b, h)), att_v],
        out_specs=att_o,
        out_shape=jax.ShapeDtypeStruct((n_tok, GROUP), BF16),
        compiler_params=_params(2),
        name="fox_attention",
    )(fqT, fk, fvT)

    kpos = jnp.arange(TK)[:, None]
    qpos = jnp.arange(TQ)[None, :]
    rel = jnp.stack([qpos - kpos, qpos - kpos + TK])
    assert TK + 1 >= MAX_DISTANCE and TQ == TK
    table = (rel_bias.astype(F32) - rel_bias[NUM_BUCKETS - 1].astype(F32)[None, :]) * LOG2E
    onehot = _t5_bucket(rel)[None, ..., None] == jnp.arange(NUM_BUCKETS, dtype=jnp.int32)
    biasT = jnp.sum(jnp.where(onehot, table.T[:, None, None, None, :], 0.0), axis=-1)
    biasT = jnp.where((rel >= 0)[None], biasT, NEG_INF)
    lam_init = 0.8 - 0.6 * math.exp(-0.3 * i)
    lam_p = jnp.stack([prm['lambda_q1'], prm['lambda_k1'], prm['lambda_q2'], prm['lambda_k2']]).astype(F32)
    gsubT = jnp.broadcast_to((prm['g_subln'] * (1.0 - lam_init))[:, None], (LANES, TQ))
    diff_out = pl.pallas_call(
        functools.partial(_diff_kernel, lam_init),
        grid=(bsz, N_DIFF_HEADS),
        in_specs=[att_q, att_k, att_v, pl.BlockSpec((None, 2, TK, TQ), lambda b, h: (h, 0, 0, 0)),
                  _full((4, HEAD_DIM)), _full((LANES, TQ))],
        out_specs=att_o,
        out_shape=jax.ShapeDtypeStruct((n_tok, GROUP), BF16),
        compiler_params=_params(2),
        name="diff_attention",
    )(dqT, dk, dvT, biasT, lam_p, gsubT)

    w_out = prm['w_out'].astype(BF16)
    wr = jnp.zeros((d, LANES), F32).at[:, :N_EXPERTS].set(prm['w_router'])
    wr_hi = wr.astype(BF16)
    wr_lo = (wr - wr_hi.astype(F32)).astype(BF16)
    br = jnp.full((1, LANES), NEG_INF, F32).at[0, :N_EXPERTS].set(prm['b_router'])
    row_spec = lambda w: pl.BlockSpec((TM, w), lambda t: (t, 0))
    h1, xm, topi, gates, counts = pl.pallas_call(
        _mix_kernel,
        grid=(n_tok // TM,),
        in_specs=[row_spec(d), row_spec(GROUP), row_spec(GROUP), _full((GROUP, d)), _full((GROUP, d)),
                  _full((1, d)), _full((d, LANES)), _full((d, LANES)), _full((1, LANES))],
        out_specs=[row_spec(d), pl.BlockSpec((TM, 1, d), lambda t: (t, 0, 0)), row_spec(LANES),
                   row_spec(LANES), _full((1, LANES))],
        out_shape=[jax.ShapeDtypeStruct((n_tok, d), F32), jax.ShapeDtypeStruct((n_tok, 1, d), F32),
                   jax.ShapeDtypeStruct((n_tok, LANES), jnp.int32), jax.ShapeDtypeStruct((n_tok, LANES), F32),
                   jax.ShapeDtypeStruct((1, LANES), F32)],
        scratch_shapes=[pltpu.VMEM((1, LANES), F32)],
        compiler_params=_params(1),
        name="mix_router",
    )(x2, fox_out, diff_out, w_out[:GROUP], w_out[GROUP:], prm['g_mlp'][None, :], wr_hi, wr_lo, br)

    m = n_tok * TOP_K
    n_blocks = (m + N_EXPERTS * MOE_BLOCK + MOE_BLOCK - 1) // MOE_BLOCK
    cnt = counts[0, :N_EXPERTS].astype(jnp.int32)
    padded = ((cnt + MOE_BLOCK - 1) // MOE_BLOCK) * MOE_BLOCK
    pad_ends = jnp.cumsum(padded)
    pad_starts = pad_ends - padded
    block_start = jnp.arange(n_blocks, dtype=jnp.int32) * MOE_BLOCK
    block_expert = jnp.minimum(jnp.sum(pad_ends[None, :] <= block_start[:, None], axis=1),
                               N_EXPERTS - 1).astype(jnp.int32)
    n_used = (pad_ends[-1] // MOE_BLOCK).astype(jnp.int32)[None]
    experts = jnp.arange(N_EXPERTS, dtype=jnp.int32)
    real = topi[:, :TOP_K] * (2 * m) + jnp.arange(m, dtype=jnp.int32).reshape(n_tok, TOP_K)
    fill = jnp.arange(IDX_ALIGN - 1, dtype=jnp.int32)[None, :]
    need = (-cnt) % IDX_ALIGN
    filler = jnp.where(fill < need[:, None], experts[:, None] * (2 * m) + m + fill,
                       N_EXPERTS * (2 * m) + experts[:, None] * IDX_ALIGN + fill)
    keys = jnp.sort(jnp.concatenate([real.reshape(m), filler.reshape(-1)]))
    flat = jnp.concatenate([(keys % (2 * m)) % m, jnp.zeros((MOE_BLOCK,), jnp.int32)])
    grouped_tok = flat // TOP_K
    grouped_row = (flat % TOP_K) * n_tok + grouped_tok
    onehot_e = block_expert[:, None] == experts[None, :]
    pick = lambda v: jnp.sum(jnp.where(onehot_e, v[None, :], 0), axis=1)
    in_group = block_start - pick(pad_starts)
    n_valid = jnp.clip(pick(cnt) - in_group, 0, MOE_BLOCK)
    group_start = jnp.cumsum(cnt + need) - (cnt + need)
    first = jnp.clip(pick(group_start) + in_group, 0, keys.shape[0]) // IDX_ALIGN * IDX_ALIGN

    d_ff = prm['w2'].shape[1]
    n_grp = 2 * d_ff // (2 * LANES)
    b1p = prm['b1'].reshape(N_EXPERTS, n_grp, LANES, 2).transpose(0, 1, 3, 2).reshape(N_EXPERTS, 1, 2 * d_ff)
    b2 = prm['b2'].reshape(N_EXPERTS, 1, d)
    pj = jnp.arange(LANES)
    perm = (jnp.zeros((2 * LANES, 2 * LANES), BF16).at[2 * pj, pj].set(1.0)
            .at[2 * pj + 1, LANES + pj].set(1.0))
    any_spec = pl.BlockSpec(memory_space=pl.ANY)
    ys = pl.pallas_call(
        _expert_kernel,
        grid_spec=pltpu.PrefetchScalarGridSpec(
            num_scalar_prefetch=4,
            grid=(n_blocks,),
            in_specs=[any_spec, any_spec, any_spec,
                      pl.BlockSpec((None, d, 2 * d_ff), lambda t, be, *_:(be[t], 0, 0)),
                      pl.BlockSpec((None, 1, 2 * d_ff), lambda t, be, *_:(be[t], 0, 0)),
                      pl.BlockSpec((None, d_ff, d), lambda t, be, *_:(be[t], 0, 0)),
                      pl.BlockSpec((None, 1, d), lambda t, be, *_:(be[t], 0, 0)),
                      pl.BlockSpec((2 * LANES, 2 * LANES), lambda t, be, *_:(0, 0))],
            out_specs=any_spec,
            scratch_shapes=[pltpu.SMEM((2, MOE_BLOCK), jnp.int32), pltpu.SMEM((2, MOE_BLOCK), jnp.int32)]
            + [pltpu.VMEM((MOE_BLOCK, d), F32)] * 4
            + [pltpu.VMEM((d, 2 * d_ff), BF16), pltpu.VMEM((d_ff, d), BF16), pltpu.VMEM((MOE_BLOCK, 2 * d_ff), F32)]
            + [pltpu.SemaphoreType.DMA((2, 2)), pltpu.SemaphoreType.DMA((2,)), pltpu.SemaphoreType.DMA((2,))],
        ),
        out_shape=jax.ShapeDtypeStruct((m + MOE_BLOCK, 1, d), F32),
        compiler_params=pltpu.CompilerParams(dimension_semantics=("arbitrary",),
                                             vmem_limit_bytes=EXPERT_VMEM_LIMIT),
        name="experts",
    )(block_expert, n_used, first, n_valid, grouped_tok, grouped_row, xm, prm['w1'], b1p, prm['w2'], b2, perm)

    ple = p_i.shape[-1]
    out = pl.pallas_call(
        _final_kernel,
        grid=(n_tok // TM,),
        in_specs=[row_spec(d)]
        + [pl.BlockSpec((TM, 1, d), functools.partial(lambda k, t: (k * (n_tok // TM) + t, 0, 0), k))
           for k in range(TOP_K)]
        + [row_spec(LANES),
                  _full((1, d)), _full((d, d)), row_spec(ple), _full((ple, d))],
        out_specs=row_spec(d),
        out_shape=jax.ShapeDtypeStruct((n_tok, d), F32),
        compiler_params=_params(1),
        name="combine_ple",
    )(h1, ys, ys, ys, ys, gates, prm['g_ple'][None, :],
      prm['w_ple_gate'].astype(BF16), p_i.reshape(n_tok, ple), prm['w_ple_proj'].astype(BF16))
    return out.reshape(bsz, seq, d)


def kernel(x, p, rel_bias, g_attn, w_in, b_f, g_fox_q, g_fox_k, g_diff_q, g_diff_k, lambda_q1, lambda_k1,
           lambda_q2, lambda_k2, g_subln, w_out, g_mlp, w_router, b_router, w1, b1, w2, b2, g_ple,
           w_ple_gate, w_ple_proj):
    stacked = dict(g_attn=g_attn, w_in=w_in, b_f=b_f, g_fox_q=g_fox_q, g_fox_k=g_fox_k, g_diff_q=g_diff_q,
                   g_diff_k=g_diff_k, lambda_q1=lambda_q1, lambda_k1=lambda_k1, lambda_q2=lambda_q2,
                   lambda_k2=lambda_k2, g_subln=g_subln, w_out=w_out, g_mlp=g_mlp, w_router=w_router,
                   b_router=b_router, w1=w1, b1=b1, w2=w2, b2=b2, g_ple=g_ple, w_ple_gate=w_ple_gate,
                   w_ple_proj=w_ple_proj)
    h = x
    for i in range(p.shape[0]):
        h = _layer(i, h, p[i], rel_bias, {name: v[i] for name, v in stacked.items()})
    return h
```

```python
import functools
import math

import jax
import jax.numpy as jnp
from jax import lax
from jax.experimental import pallas as pl
from jax.experimental.pallas import tpu as pltpu

F32 = jnp.float32
BF16 = jnp.bfloat16

HEAD_DIM = 64
N_FOX_HEADS = 8
N_DIFF_HEADS = 4
GROUP = 512
LANES = 128
NUM_BUCKETS = 32
MAX_DISTANCE = 128
N_EXPERTS = 32
TOP_K = 4
SWIGLU_ALPHA = 1.702
SWIGLU_LIMIT = 7.0
MOE_BLOCK = 256
IDX_ALIGN = 128
NORM_EPS = 1e-6
NEG_INF = -1e30
LOG2E = math.log2(math.e)
N_SPLIT = 3

TM = 512
TQ = 256
TK = 256
LOOKAHEAD = 2
VMEM_LIMIT = 48 * 1024 * 1024
EXPERT_VMEM_LIMIT = 58 * 1024 * 1024


def _params(n_axes):
    return pltpu.CompilerParams(dimension_semantics=("arbitrary",) * n_axes,
                                vmem_limit_bytes=VMEM_LIMIT)


def _dot(a, b):
    return jnp.dot(a, b, preferred_element_type=F32)


def _dot_nt(a, b):
    return lax.dot_general(a, b, (((1,), (1,)), ((), ())), preferred_element_type=F32)


def _split3(v):
    hi = v.astype(BF16)
    r = v - hi.astype(F32)
    mid = r.astype(BF16)
    lo = (r - mid.astype(F32)).astype(BF16)
    return hi, mid, lo


def _inproj_kernel(x_ref, g_ref, wfq_ref, wfk_ref, wfv_ref, wdq_ref, wdk_ref, wdv_ref, wf_ref, bf_ref,
                   gfq_ref, gfk_ref, gdq_ref, gdk_ref, seg_ref, tri_ref, place_ref,
                   fq_ref, fk_ref, fv_ref, dq_ref, dk_ref, dv_ref, carry_ref):
    @pl.when(pl.program_id(1) == 0)
    def _():
        carry_ref[...] = jnp.zeros_like(carry_ref)

    xf = x_ref[...]
    a = (xf * lax.rsqrt(jnp.mean(xf * xf, axis=-1, keepdims=True) + NORM_EPS) * g_ref[...]).astype(BF16)
    seg = seg_ref[...]

    fl = _dot(a, wf_ref[...]) + bf_ref[...]
    fq_acc = _dot_nt(wfq_ref[...], a)
    dq_acc = _dot_nt(wdq_ref[...], a)
    fk_acc = _dot(a, wfk_ref[...])
    dk_acc = _dot(a, wdk_ref[...])
    fv_acc = _dot_nt(wfv_ref[...], a)
    dv_acc = _dot_nt(wdv_ref[...], a)

    half = seg.shape[0]

    def q_norm(acc, gain_ref):
        sq = (acc * acc).astype(BF16)
        ms = jnp.concatenate([_dot(seg, sq[:half]), _dot(seg, sq[half:])], axis=0)
        return (acc * lax.rsqrt(ms + NORM_EPS) * gain_ref[...]).astype(BF16)

    def k_norm(acc, gain_ref):
        sq = (acc * acc).astype(BF16)
        ms = jnp.concatenate([_dot(sq[:, :half], seg), _dot(sq[:, half:], seg)], axis=1)
        return (acc * lax.rsqrt(ms + NORM_EPS) * gain_ref[...]).astype(BF16)

    def v_store(acc, out_ref):
        acc = acc.astype(BF16)
        for hb in range(GROUP // LANES):
            for j in range(TM // TK):
                out_ref[hb, j] = acc[hb * LANES:(hb + 1) * LANES, j * TK:(j + 1) * TK]

    logf = jnp.minimum(fl, 0.0) - jnp.log1p(jnp.exp(-jnp.abs(fl)))
    sums = _dot(tri_ref[...], jnp.concatenate(_split3(logf), axis=1))
    cs = carry_ref[...] + sums[:, :LANES] + sums[:, LANES:2 * LANES] + sums[:, 2 * LANES:]
    carry_ref[...] = cs[TM - 1:TM, :]

    fq_ref[...] = q_norm(fq_acc, gfq_ref)
    dq_ref[...] = q_norm(dq_acc, gdq_ref)
    fk = k_norm(fk_acc, gfk_ref)
    dk_ref[...] = k_norm(dk_acc, gdk_ref)
    v_store(fv_acc, fv_ref)
    v_store(dv_acc, dv_ref)

    extra = _dot(jnp.concatenate(_split3(cs * LOG2E), axis=1), place_ref[...]).astype(BF16)
    for hb in range(GROUP // LANES):
        fk_ref[:, 2 * hb * LANES:(2 * hb + 1) * LANES] = fk[:, hb * LANES:(hb + 1) * LANES]
        fk_ref[:, (2 * hb + 1) * LANES:(2 * hb + 2) * LANES] = extra[:, hb * LANES:(hb + 1) * LANES]


def _two_map_queries(qT_blk):
    row = lax.broadcasted_iota(jnp.int32, qT_blk.shape, 0)
    zero = jnp.zeros_like(qT_blk)
    return jnp.concatenate([jnp.where(row < HEAD_DIM, qT_blk, zero),
                            jnp.where(row >= HEAD_DIM, qT_blk, zero)], axis=1)


def _softmax_step(sT, vT_blk, carry):
    m, l, acc = carry
    m_new = jnp.maximum(m, jnp.max(sT, axis=0, keepdims=True))
    alpha = jnp.exp2(m - m_new)
    pT = jnp.exp2(sT - m_new)
    l = alpha * l + jnp.sum(pT, axis=0, keepdims=True)
    acc = alpha * acc + _dot(vT_blk, pT.astype(BF16))
    return m_new, l, acc


def _causal_keep():
    key = lax.broadcasted_iota(jnp.int32, (TK, 2 * TQ), 0)
    col = lax.broadcasted_iota(jnp.int32, (TK, 2 * TQ), 1)
    qry = jnp.where(col >= TQ, col - TQ, col)
    return key <= qry


def _init_carry():
    return (jnp.full((1, 2 * TQ), NEG_INF, F32), jnp.zeros((1, 2 * TQ), F32),
            jnp.zeros((LANES, 2 * TQ), F32))


def _causal_sweep(seq, k_ref, vT_ref, queries, adjust, finish):
    pairs = [(qi, kj) for qi in range(seq // TQ) for kj in range(qi + 1)]
    q_ops = {}

    def score(qi, kj):
        if qi not in q_ops:
            q_ops[qi] = queries(qi)
        return _dot(k_ref[kj * TK:(kj + 1) * TK, :], q_ops[qi])

    pending = [score(*pr) for pr in pairs[:LOOKAHEAD]]
    carry = None
    for n, (qi, kj) in enumerate(pairs):
        if n + LOOKAHEAD < len(pairs):
            pending.append(score(*pairs[n + LOOKAHEAD]))
        if kj == 0:
            carry = _init_carry()
        carry = _softmax_step(adjust(pending.pop(0), qi, kj), vT_ref[kj], carry)
        if kj == qi:
            m, l, acc = carry
            finish(qi, acc * (1.0 / l))


def _fox_kernel(qT_ref, k_ref, vT_ref, o_ref):
    seq = k_ref.shape[0]
    keep = _causal_keep()
    row = lax.broadcasted_iota(jnp.int32, (LANES, TQ), 0)
    arow = lax.broadcasted_iota(jnp.int32, (LANES, 2 * TQ), 0)
    acol = lax.broadcasted_iota(jnp.int32, (LANES, 2 * TQ), 1)
    first = (acol < TQ) & (arow < N_SPLIT)
    second = (acol >= TQ) & (arow >= N_SPLIT) & (arow < 2 * N_SPLIT)
    minus_c = jnp.where(first | second, -1.0, 0.0).astype(BF16)

    def queries(qi):
        return jnp.concatenate([_two_map_queries(qT_ref[:, qi * TQ:(qi + 1) * TQ]), minus_c], axis=0)

    def adjust(sT, qi, kj):
        return jnp.where(keep, sT, NEG_INF) if kj == qi else sT

    def finish(qi, oT):
        o = jnp.where(row < HEAD_DIM, oT[:, :TQ], oT[:, TQ:])
        o_ref[qi * TQ:(qi + 1) * TQ, :] = o.T.astype(BF16)

    _causal_sweep(seq, k_ref, vT_ref, queries, adjust, finish)


def _diff_kernel(lam_init, qT_ref, k_ref, vT_ref, bias_ref, lam_ref, gsub_ref, o_ref):
    seq = k_ref.shape[0]
    lp = lam_ref[...]
    lam = (jnp.exp(jnp.sum(lp[0:1] * lp[1:2], axis=1, keepdims=True))
           - jnp.exp(jnp.sum(lp[2:3] * lp[3:4], axis=1, keepdims=True)) + lam_init)

    def queries(qi):
        return _two_map_queries(qT_ref[:, qi * TQ:(qi + 1) * TQ])

    def adjust(sT, qi, kj):
        if kj >= qi - 1:
            b = bias_ref[qi - kj]
            sT = jnp.concatenate([sT[:, :TQ] + b, sT[:, TQ:] + b], axis=1)
        return sT

    def finish(qi, oT):
        o = oT[:, :TQ] - lam * oT[:, TQ:]
        y = o * lax.rsqrt(jnp.mean(o * o, axis=0, keepdims=True) + NORM_EPS) * gsub_ref[...]
        o_ref[qi * TQ:(qi + 1) * TQ, :] = y.T.astype(BF16)

    _causal_sweep(seq, k_ref, vT_ref, queries, adjust, finish)


def _mix_kernel(x_ref, fox_ref, dif_ref, wa_ref, wb_ref, g_ref, wr_hi_ref, wr_lo_ref, br_ref,
                h_ref, xm_ref, topi_ref, gate_ref, cnt_ref, carry_ref):
    @pl.when(pl.program_id(0) == 0)
    def _():
        carry_ref[...] = jnp.zeros_like(carry_ref)

    halves = [slice(0, TM // 2), slice(TM // 2, TM)]
    hs = [x_ref[rows, :] + _dot(fox_ref[rows, :], wa_ref[...]) + _dot(dif_ref[rows, :], wb_ref[...])
          for rows in halves]
    counts = carry_ref[...]
    for rows, h in zip(halves, hs):
        h_ref[rows, :] = h
        xm = h * lax.rsqrt(jnp.mean(h * h, axis=-1, keepdims=True) + NORM_EPS) * g_ref[...]
        xm_ref[rows, 0, :] = xm

        x_hi = xm.astype(BF16)
        x_lo = (xm - x_hi.astype(F32)).astype(BF16)
        logits = (_dot(x_hi, wr_hi_ref[...]) + _dot(x_hi, wr_lo_ref[...]) + _dot(x_lo, wr_hi_ref[...])
                  + br_ref[...])

        lane = lax.broadcasted_iota(jnp.int32, logits.shape, 1)
        vals, sels, idxs = [], [], []
        for _ in range(TOP_K):
            mx = jnp.max(logits, axis=1, keepdims=True)
            idx = jnp.min(jnp.where(logits == mx, lane, LANES), axis=1, keepdims=True)
            sel = lane == idx
            logits = jnp.where(sel, -jnp.inf, logits)
            vals.append(mx)
            sels.append(sel)
            idxs.append(idx)
        exps = [jnp.exp(v - vals[0]) for v in vals]
        denom = exps[0] + exps[1] + exps[2] + exps[3]

        multi_hot = sum(s.astype(F32) for s in sels)
        topi = jnp.zeros(logits.shape, jnp.int32)
        gate = jnp.zeros(logits.shape, F32)
        for k in range(TOP_K):
            topi = jnp.where(lane == k, idxs[k], topi)
            gate = jnp.where(lane == k, exps[k] / denom, gate)
        topi_ref[rows, :] = topi
        gate_ref[rows, :] = gate
        counts = counts + jnp.sum(multi_hot, axis=0, keepdims=True)
    carry_ref[...] = counts
    cnt_ref[...] = counts


def _expert_kernel(be_ref, nu_ref, off_ref, nv_ref, tok_hbm, asg_hbm, x_hbm, w1f_ref, b1_ref, w2f_ref, b2_ref,
                   perm_ref, y_hbm, tok_ref, asg_ref, x0, x1, y0, y1, w1_ref, w2_ref, h_ref, act_ref,
                   idx_sem, g_sem, s_sem):
    i = pl.program_id(0)
    n_used = nu_ref[0]
    n_blocks = be_ref.shape[0]
    spare_row = y_hbm.shape[0] - MOE_BLOCK
    xbufs, ybufs = (x0, x1), (y0, y1)

    def tok_copy(blk, slot):
        start = pl.multiple_of(off_ref[blk], IDX_ALIGN)
        return pltpu.make_async_copy(tok_hbm.at[pl.ds(start, MOE_BLOCK)], tok_ref.at[slot], idx_sem.at[0, slot])

    def asg_copy(blk, slot):
        start = pl.multiple_of(off_ref[blk], IDX_ALIGN)
        return pltpu.make_async_copy(asg_hbm.at[pl.ds(start, MOE_BLOCK)], asg_ref.at[slot], idx_sem.at[1, slot])

    def asg_ready(blk, slot):
        asg_copy(0, slot).wait()

        def fix(r, carry):
            asg_ref[slot, r] = spare_row + r
            return carry
        lax.fori_loop(nv_ref[blk], MOE_BLOCK, fix, 0)

    def row_loop(fn, part, n_parts):
        for r in range(part * MOE_BLOCK // n_parts, (part + 1) * MOE_BLOCK // n_parts):
            fn(r)

    def gather_issue(slot, part=0, n_parts=1):
        row_loop(lambda r: pltpu.make_async_copy(x_hbm.at[tok_ref[slot, r]], xbufs[slot].at[pl.ds(r, 1)],
                                                 g_sem.at[slot]).start(), part, n_parts)

    def gather_wait(slot):
        pltpu.make_async_copy(xbufs[slot], xbufs[slot], g_sem.at[slot]).wait()

    def scatter_issue(slot, part=0, n_parts=1):
        row_loop(lambda r: pltpu.make_async_copy(ybufs[slot].at[pl.ds(r, 1)], y_hbm.at[asg_ref[slot, r]],
                                                 s_sem.at[slot]).start(), part, n_parts)

    def scatter_wait(slot):
        pltpu.make_async_copy(ybufs[slot], ybufs[slot], s_sem.at[slot]).wait()

    @pl.when(i == 0)
    def _():
        y1[...] = jnp.zeros(y1.shape, y1.dtype)
        for r in range(MOE_BLOCK):
            asg_ref[1, r] = spare_row + r
        tok_copy(0, 0).start()
        tok_copy(0, 0).wait()
        tok_copy(jnp.minimum(1, n_blocks - 1), 1).start()
        gather_issue(0)

    @pl.when((i < n_used) & ((i == 0) | (be_ref[i] != be_ref[jnp.maximum(i - 1, 0)])))
    def _():
        for g in range(w1f_ref.shape[1] // (2 * LANES)):
            cols = slice(2 * g * LANES, 2 * (g + 1) * LANES)
            w1_ref[:, cols] = _dot(w1f_ref[:, cols].astype(BF16), perm_ref[...]).astype(BF16)
        w2_ref[...] = w2f_ref[...].astype(BF16)

    def step(cur):
        nxt = 1 - cur
        asg_copy(i, cur).start()
        tok_copy(0, nxt).wait()

        pl.when(i >= 1)(functools.partial(asg_ready, i - 1, nxt))

        xbuf, ybuf = xbufs[cur], ybufs[cur]
        half_ff, half_d = h_ref.shape[1] // 2, ybuf.shape[1] // 2

        def first_matmul(cols):
            h_ref[:, cols] = _dot(xbuf[...].astype(BF16), w1_ref[:, cols]) + b1_ref[:, cols]

        gather_wait(cur)
        gather_issue(nxt, 0, 2)
        tok_copy(jnp.minimum(i + 2, n_blocks - 1), cur).start()
        first_matmul(slice(0, half_ff))

        @pl.when(n_used > 0)
        def _():
            gather_issue(nxt, 1, 2)
            first_matmul(slice(half_ff, 2 * half_ff))

        pl.when(i >= 1)(functools.partial(scatter_wait, cur))

        @pl.when(n_used > 0)
        def _():
            scatter_issue(nxt, 0, 2)
            for c in range(h_ref.shape[1] // (2 * LANES)):
                glu = jnp.minimum(h_ref[:, 2 * c * LANES:(2 * c + 1) * LANES], SWIGLU_LIMIT)
                lin = jnp.clip(h_ref[:, (2 * c + 1) * LANES:(2 * c + 2) * LANES], -SWIGLU_LIMIT, SWIGLU_LIMIT)
                act_ref[:, c * LANES:(c + 1) * LANES] = (
                    glu * jax.nn.sigmoid(SWIGLU_ALPHA * glu) * (lin + 1.0)).astype(BF16)
            ybuf[:, :half_d] = _dot(act_ref[...], w2_ref[:, :half_d]) + b2_ref[:, :half_d]

        @pl.when(n_used > 0)
        def _():
            scatter_issue(nxt, 1, 2)
            ybuf[:, half_d:] = _dot(act_ref[...], w2_ref[:, half_d:]) + b2_ref[:, half_d:]

        @pl.when(i == n_used - 1)
        def _():
            asg_ready(i, cur)
            scatter_issue(cur)
            scatter_wait(nxt)
            scatter_wait(cur)
            gather_wait(nxt)
            tok_copy(0, cur).wait()

    for parity in range(2):
        pl.when((i < n_used) & (i % 2 == parity))(functools.partial(step, parity))


def _final_kernel(h_ref, *rest):
    y_refs, (gate_ref, g_ref, wg_ref, p_ref, wp_ref, o_ref) = rest[:TOP_K], rest[TOP_K:]
    for rows in (slice(0, TM // 2), slice(TM // 2, TM)):
        gates = gate_ref[rows, :]
        moe = sum(gates[:, k:k + 1] * y_ref[rows, 0, :] for k, y_ref in enumerate(y_refs))
        h = h_ref[rows, :] + moe
        n = (h * lax.rsqrt(jnp.mean(h * h, axis=-1, keepdims=True) + NORM_EPS) * g_ref[...]).astype(BF16)
        gate = jax.nn.sigmoid(_dot(n, wg_ref[...]))
        o_ref[rows, :] = h + gate * _dot(p_ref[rows, :].astype(BF16), wp_ref[...])


def _t5_bucket(rel):
    n = jnp.maximum(rel, 0)
    max_exact = NUM_BUCKETS // 2
    nf = jnp.maximum(n, 1).astype(F32)
    large = max_exact + (jnp.log(nf / max_exact) / math.log(MAX_DISTANCE / max_exact)
                         * (NUM_BUCKETS - max_exact)).astype(jnp.int32)
    large = jnp.minimum(large, NUM_BUCKETS - 1)
    return jnp.where(n < max_exact, n, large)


def _full(shape):
    return pl.BlockSpec(shape, lambda *_: (0,) * len(shape))


def _layer(i, h_in, p_i, rel_bias, prm):
    bsz, seq, d = h_in.shape
    n_tok = bsz * seq
    ns = seq // TM
    nk = seq // TK
    x2 = h_in.reshape(n_tok, d)

    w_in = prm['w_in']
    fox_w = N_FOX_HEADS * HEAD_DIM
    offs = [0, fox_w, 2 * fox_w, 3 * fox_w, 3 * fox_w + N_FOX_HEADS]
    offs += [offs[4] + GROUP, offs[4] + 2 * GROUP, offs[4] + 3 * GROUP]
    col = lambda a, b: w_in[:, a:b].astype(BF16)
    wfqT, wfk, wfvT = col(offs[0], offs[1]).T, col(offs[1], offs[2]), col(offs[2], offs[3]).T
    wdqT, wdk, wdvT = col(offs[4], offs[5]).T, col(offs[5], offs[6]), col(offs[6], offs[7]).T
    wf = jnp.zeros((d, LANES), BF16).at[:, :N_FOX_HEADS].set(col(offs[3], offs[4]))
    bf = jnp.zeros((1, LANES), F32).at[0, :N_FOX_HEADS].set(prm['b_f'])
    scale = HEAD_DIM ** -0.5 * LOG2E
    rep = GROUP // HEAD_DIM
    heads = jnp.arange(N_FOX_HEADS)
    place = jnp.concatenate([jnp.zeros((LANES, GROUP), BF16)
                             .at[heads, (heads // 2) * LANES + (heads % 2) * N_SPLIT + t].set(1.0)
                             for t in range(N_SPLIT)])
    gfqT = jnp.broadcast_to((jnp.tile(prm['g_fox_q'], rep) * scale)[:, None], (GROUP, TM))
    gdqT = jnp.broadcast_to((jnp.tile(prm['g_diff_q'], rep) * scale)[:, None], (GROUP, TM))
    gfk = jnp.tile(prm['g_fox_k'], rep)[None, :]
    gdk = jnp.tile(prm['g_diff_k'], rep)[None, :]
    gi = jnp.arange(GROUP // 2) // HEAD_DIM
    seg = jnp.where(gi[:, None] == gi[None, :], 1.0 / HEAD_DIM, 0.0).astype(BF16)
    ti = jnp.arange(TM)
    tri_incl = (ti[None, :] <= ti[:, None]).astype(BF16)

    w_spec_t = _full((GROUP, d))
    w_spec = _full((d, GROUP))
    qT_shape = jax.ShapeDtypeStruct((bsz, GROUP, seq), BF16)
    k_shape = jax.ShapeDtypeStruct((n_tok, GROUP), BF16)
    vT_shape = jax.ShapeDtypeStruct((bsz, GROUP // LANES, nk, LANES, TK), BF16)
    qT_spec = pl.BlockSpec((None, GROUP, TM), lambda b, s: (b, 0, s))
    k_spec = pl.BlockSpec((TM, GROUP), lambda b, s: (b * ns + s, 0))
    vT_spec = pl.BlockSpec((None, GROUP // LANES, TM // TK, LANES, TK), lambda b, s: (b, 0, s, 0, 0))
    fk_shape = jax.ShapeDtypeStruct((n_tok, 2 * GROUP), BF16)
    fk_spec = pl.BlockSpec((TM, 2 * GROUP), lambda b, s: (b * ns + s, 0))
    fqT, fk, fvT, dqT, dk, dvT = pl.pallas_call(
        _inproj_kernel,
        grid=(bsz, ns),
        in_specs=[pl.BlockSpec((TM, d), lambda b, s: (b * ns + s, 0)), _full((1, d)),
                  w_spec_t, w_spec, w_spec_t, w_spec_t, w_spec, w_spec_t, _full((d, LANES)), _full((1, LANES)),
                  _full((GROUP, TM)), _full((1, GROUP)), _full((GROUP, TM)), _full((1, GROUP)),
                  _full((GROUP // 2, GROUP // 2)), _full((TM, TM)), _full((N_SPLIT * LANES, GROUP))],
        out_specs=[qT_spec, fk_spec, vT_spec, qT_spec, k_spec, vT_spec],
        out_shape=[qT_shape, fk_shape, vT_shape, qT_shape, k_shape, vT_shape],
        scratch_shapes=[pltpu.VMEM((1, LANES), F32)],
        compiler_params=_params(2),
        name="inproj",
    )(x2, prm['g_attn'][None, :], wfqT, wfk, wfvT, wdqT, wdk, wdvT, wf, bf, gfqT, gfk, gdqT, gdk, seg, tri_incl,
      place)

    n_hb = GROUP // LANES
    att_q = pl.BlockSpec((None, LANES, seq), lambda b, h: (b, h, 0))
    att_k = pl.BlockSpec((seq, LANES), lambda b, h: (b, h))
    att_v = pl.BlockSpec((None, None, nk, LANES, TK), lambda b, h: (b, h, 0, 0, 0))
    att_o = pl.BlockSpec((seq, LANES), lambda b, h: (b, h))
    fox_out = pl.pallas_call(
        _fox_kernel,
        grid=(bsz, n_hb),
        in_specs=[att_q, pl.BlockSpec((seq, 2 * LANES), lambda b, h: (b, h)), att_v],
        out_specs=att_o,
        out_shape=jax.ShapeDtypeStruct((n_tok, GROUP), BF16),
        compiler_params=_params(2),
        name="fox_attention",
    )(fqT, fk, fvT)

    kpos = jnp.arange(TK)[:, None]
    qpos = jnp.arange(TQ)[None, :]
    rel = jnp.stack([qpos - kpos, qpos - kpos + TK])
    assert TK + 1 >= MAX_DISTANCE and TQ == TK
    table = (rel_bias.astype(F32) - rel_bias[NUM_BUCKETS - 1].astype(F32)[None, :]) * LOG2E
    onehot = _t5_bucket(rel)[None, ..., None] == jnp.arange(NUM_BUCKETS, dtype=jnp.int32)
    biasT = jnp.sum(jnp.where(onehot, table.T[:, None, None, None, :], 0.0), axis=-1)
    biasT = jnp.where((rel >= 0)[None], biasT, NEG_INF)
    lam_init = 0.8 - 0.6 * math.exp(-0.3 * i)
    lam_p = jnp.stack([prm['lambda_q1'], prm['lambda_k1'], prm['lambda_q2'], prm['lambda_k2']]).astype(F32)
    gsubT = jnp.broadcast_to((prm['g_subln'] * (1.0 - lam_init))[:, None], (LANES, TQ))
    diff_out = pl.pallas_call(
        functools.partial(_diff_kernel, lam_init),
        grid=(bsz, N_DIFF_HEADS),
        in_specs=[att_q, att_k, att_v, pl.BlockSpec((None, 2, TK, TQ), lambda b, h: (h, 0, 0, 0)),
                  _full((4, HEAD_DIM)), _full((LANES, TQ))],
        out_specs=att_o,
        out_shape=jax.ShapeDtypeStruct((n_tok, GROUP), BF16),
        compiler_params=_params(2),
        name="diff_attention",
    )(dqT, dk, dvT, biasT, lam_p, gsubT)

    w_out = prm['w_out'].astype(BF16)
    wr = jnp.zeros((d, LANES), F32).at[:, :N_EXPERTS].set(prm['w_router'])
    wr_hi = wr.astype(BF16)
    wr_lo = (wr - wr_hi.astype(F32)).astype(BF16)
    br = jnp.full((1, LANES), NEG_INF, F32).at[0, :N_EXPERTS].set(prm['b_router'])
    row_spec = lambda w: pl.BlockSpec((TM, w), lambda t: (t, 0))
    h1, xm, topi, gates, counts = pl.pallas_call(
        _mix_kernel,
        grid=(n_tok // TM,),
        in_specs=[row_spec(d), row_spec(GROUP), row_spec(GROUP), _full((GROUP, d)), _full((GROUP, d)),
                  _full((1, d)), _full((d, LANES)), _full((d, LANES)), _full((1, LANES))],
        out_specs=[row_spec(d), pl.BlockSpec((TM, 1, d), lambda t: (t, 0, 0)), row_spec(LANES),
                   row_spec(LANES), _full((1, LANES))],
        out_shape=[jax.ShapeDtypeStruct((n_tok, d), F32), jax.ShapeDtypeStruct((n_tok, 1, d), F32),
                   jax.ShapeDtypeStruct((n_tok, LANES), jnp.int32), jax.ShapeDtypeStruct((n_tok, LANES), F32),
                   jax.ShapeDtypeStruct((1, LANES), F32)],
        scratch_shapes=[pltpu.VMEM((1, LANES), F32)],
        compiler_params=_params(1),
        name="mix_router",
    )(x2, fox_out, diff_out, w_out[:GROUP], w_out[GROUP:], prm['g_mlp'][None, :], wr_hi, wr_lo, br)

    m = n_tok * TOP_K
    n_blocks = (m + N_EXPERTS * MOE_BLOCK + MOE_BLOCK - 1) // MOE_BLOCK
    cnt = counts[0, :N_EXPERTS].astype(jnp.int32)
    padded = ((cnt + MOE_BLOCK - 1) // MOE_BLOCK) * MOE_BLOCK
    pad_ends = jnp.cumsum(padded)
    pad_starts = pad_ends - padded
    block_start = jnp.arange(n_blocks, dtype=jnp.int32) * MOE_BLOCK
    block_expert = jnp.minimum(jnp.sum(pad_ends[None, :] <= block_start[:, None], axis=1),
                               N_EXPERTS - 1).astype(jnp.int32)
    n_used = (pad_ends[-1] // MOE_BLOCK).astype(jnp.int32)[None]
    experts = jnp.arange(N_EXPERTS, dtype=jnp.int32)
    real = topi[:, :TOP_K] * (2 * m) + jnp.arange(m, dtype=jnp.int32).reshape(n_tok, TOP_K)
    fill = jnp.arange(IDX_ALIGN - 1, dtype=jnp.int32)[None, :]
    need = (-cnt) % IDX_ALIGN
    filler = jnp.where(fill < need[:, None], experts[:, None] * (2 * m) + m + fill,
                       N_EXPERTS * (2 * m) + experts[:, None] * IDX_ALIGN + fill)
    keys = jnp.sort(jnp.concatenate([real.reshape(m), filler.reshape(-1)]))
    flat = jnp.concatenate([(keys % (2 * m)) % m, jnp.zeros((MOE_BLOCK,), jnp.int32)])
    grouped_tok = flat // TOP_K
    grouped_row = (flat % TOP_K) * n_tok + grouped_tok
    onehot_e = block_expert[:, None] == experts[None, :]
    pick = lambda v: jnp.sum(jnp.where(onehot_e, v[None, :], 0), axis=1)
    in_group = block_start - pick(pad_starts)
    n_valid = jnp.clip(pick(cnt) - in_group, 0, MOE_BLOCK)
    group_start = jnp.cumsum(cnt + need) - (cnt + need)
    first = jnp.clip(pick(group_start) + in_group, 0, keys.shape[0]) // IDX_ALIGN * IDX_ALIGN

    d_ff = prm['w2'].shape[1]
    n_grp = 2 * d_ff // (2 * LANES)
    b1p = prm['b1'].reshape(N_EXPERTS, n_grp, LANES, 2).transpose(0, 1, 3, 2).reshape(N_EXPERTS, 1, 2 * d_ff)
    b2 = prm['b2'].reshape(N_EXPERTS, 1, d)
    pj = jnp.arange(LANES)
    perm = (jnp.zeros((2 * LANES, 2 * LANES), BF16).at[2 * pj, pj].set(1.0)
            .at[2 * pj + 1, LANES + pj].set(1.0))
    any_spec = pl.BlockSpec(memory_space=pl.ANY)
    ys = pl.pallas_call(
        _expert_kernel,
        grid_spec=pltpu.PrefetchScalarGridSpec(
            num_scalar_prefetch=4,
            grid=(n_blocks,),
            in_specs=[any_spec, any_spec, any_spec,
                      pl.BlockSpec((None, d, 2 * d_ff), lambda t, be, *_:(be[t], 0, 0)),
                      pl.BlockSpec((None, 1, 2 * d_ff), lambda t, be, *_:(be[t], 0, 0)),
                      pl.BlockSpec((None, d_ff, d), lambda t, be, *_:(be[t], 0, 0)),
                      pl.BlockSpec((None, 1, d), lambda t, be, *_:(be[t], 0, 0)),
                      pl.BlockSpec((2 * LANES, 2 * LANES), lambda t, be, *_:(0, 0))],
            out_specs=any_spec,
            scratch_shapes=[pltpu.SMEM((2, MOE_BLOCK), jnp.int32), pltpu.SMEM((2, MOE_BLOCK), jnp.int32)]
            + [pltpu.VMEM((MOE_BLOCK, d), F32)] * 4
            + [pltpu.VMEM((d, 2 * d_ff), BF16), pltpu.VMEM((d_ff, d), BF16), pltpu.VMEM((MOE_BLOCK, 2 * d_ff), F32),
               pltpu.VMEM((MOE_BLOCK, d_ff), BF16)]
            + [pltpu.SemaphoreType.DMA((2, 2)), pltpu.SemaphoreType.DMA((2,)), pltpu.SemaphoreType.DMA((2,))],
        ),
        out_shape=jax.ShapeDtypeStruct((m + MOE_BLOCK, 1, d), F32),
        compiler_params=pltpu.CompilerParams(dimension_semantics=("arbitrary",),
                                             vmem_limit_bytes=EXPERT_VMEM_LIMIT),
        name="experts",
    )(block_expert, n_used, first, n_valid, grouped_tok, grouped_row, xm, prm['w1'], b1p, prm['w2'], b2, perm)

    ple = p_i.shape[-1]
    out = pl.pallas_call(
        _final_kernel,
        grid=(n_tok // TM,),
        in_specs=[row_spec(d)]
        + [pl.BlockSpec((TM, 1, d), functools.partial(lambda k, t: (k * (n_tok // TM) + t, 0, 0), k))
           for k in range(TOP_K)]
        + [row_spec(LANES),
                  _full((1, d)), _full((d, d)), row_spec(ple), _full((ple, d))],
        out_specs=row_spec(d),
        out_shape=jax.ShapeDtypeStruct((n_tok, d), F32),
        compiler_params=_params(1),
        name="combine_ple",
    )(h1, ys, ys, ys, ys, gates, prm['g_ple'][None, :],
      prm['w_ple_gate'].astype(BF16), p_i.reshape(n_tok, ple), prm['w_ple_proj'].astype(BF16))
    return out.reshape(bsz, seq, d)


def kernel(x, p, rel_bias, g_attn, w_in, b_f, g_fox_q, g_fox_k, g_diff_q, g_diff_k, lambda_q1, lambda_k1,
           lambda_q2, lambda_k2, g_subln, w_out, g_mlp, w_router, b_router, w1, b1, w2, b2, g_ple,
           w_ple_gate, w_ple_proj):
    stacked = dict(g_attn=g_attn, w_in=w_in, b_f=b_f, g_fox_q=g_fox_q, g_fox_k=g_fox_k, g_diff_q=g_diff_q,
                   g_diff_k=g_diff_k, lambda_q1=lambda_q1, lambda_k1=lambda_k1, lambda_q2=lambda_q2,
                   lambda_k2=lambda_k2, g_subln=g_subln, w_out=w_out, g_mlp=g_mlp, w_router=w_router,
                   b_router=b_router, w1=w1, b1=b1, w2=w2, b2=b2, g_ple=g_ple, w_ple_gate=w_ple_gate,
                   w_ple_proj=w_ple_proj)
    h = x
    for i in range(p.shape[0]):
        h = _layer(i, h, p[i], rel_bias, {name: v[i] for name, v in stacked.items()})
    return h
```

```python
import functools
import math

import jax
import jax.numpy as jnp
from jax import lax
from jax.experimental import pallas as pl
from jax.experimental.pallas import tpu as pltpu

F32 = jnp.float32
BF16 = jnp.bfloat16

HEAD_DIM = 64
N_FOX_HEADS = 8
N_DIFF_HEADS = 4
GROUP = 512
LANES = 128
NUM_BUCKETS = 32
MAX_DISTANCE = 128
N_EXPERTS = 32
TOP_K = 4
SWIGLU_ALPHA = 1.702
SWIGLU_LIMIT = 7.0
MOE_BLOCK = 256
IDX_ALIGN = 128
NORM_EPS = 1e-6
NEG_INF = -1e30
LOG2E = math.log2(math.e)
N_SPLIT = 3

TM = 512
TQ = 256
TK = 256
LOOKAHEAD = 2
VMEM_LIMIT = 48 * 1024 * 1024
EXPERT_VMEM_LIMIT = 58 * 1024 * 1024


def _params(n_axes):
    return pltpu.CompilerParams(dimension_semantics=("arbitrary",) * n_axes,
                                vmem_limit_bytes=VMEM_LIMIT)


def _dot(a, b):
    return jnp.dot(a, b, preferred_element_type=F32)


def _dot_nt(a, b):
    return lax.dot_general(a, b, (((1,), (1,)), ((), ())), preferred_element_type=F32)


def _split3(v):
    hi = v.astype(BF16)
    r = v - hi.astype(F32)
    mid = r.astype(BF16)
    lo = (r - mid.astype(F32)).astype(BF16)
    return hi, mid, lo


def _inproj_kernel(x_ref, g_ref, wfq_ref, wfk_ref, wfv_ref, wdq_ref, wdk_ref, wdv_ref, wf_ref, bf_ref,
                   gfq_ref, gfk_ref, gdq_ref, gdk_ref, seg_ref, tri_ref, place_ref,
                   fq_ref, fk_ref, fv_ref, dq_ref, dk_ref, dv_ref, carry_ref):
    @pl.when(pl.program_id(1) == 0)
    def _():
        carry_ref[...] = jnp.zeros_like(carry_ref)

    xf = x_ref[...]
    a = (xf * lax.rsqrt(jnp.mean(xf * xf, axis=-1, keepdims=True) + NORM_EPS) * g_ref[...]).astype(BF16)
    seg = seg_ref[...]

    fl = _dot(a, wf_ref[...]) + bf_ref[...]
    fq_acc = _dot_nt(wfq_ref[...], a)
    dq_acc = _dot_nt(wdq_ref[...], a)
    fk_acc = _dot(a, wfk_ref[...])
    dk_acc = _dot(a, wdk_ref[...])
    fv_acc = _dot_nt(wfv_ref[...], a)
    dv_acc = _dot_nt(wdv_ref[...], a)

    half = seg.shape[0]

    def q_norm(acc, gain_ref):
        sq = (acc * acc).astype(BF16)
        ms = jnp.concatenate([_dot(seg, sq[:half]), _dot(seg, sq[half:])], axis=0)
        return (acc * lax.rsqrt(ms + NORM_EPS) * gain_ref[...]).astype(BF16)

    def k_norm(acc, gain_ref):
        sq = (acc * acc).astype(BF16)
        ms = jnp.concatenate([_dot(sq[:, :half], seg), _dot(sq[:, half:], seg)], axis=1)
        return (acc * lax.rsqrt(ms + NORM_EPS) * gain_ref[...]).astype(BF16)

    def v_store(acc, out_ref):
        acc = acc.astype(BF16)
        for hb in range(GROUP // LANES):
            for j in range(TM // TK):
                out_ref[hb, j] = acc[hb * LANES:(hb + 1) * LANES, j * TK:(j + 1) * TK]

    logf = jnp.minimum(fl, 0.0) - jnp.log1p(jnp.exp(-jnp.abs(fl)))
    sums = _dot(tri_ref[...], jnp.concatenate(_split3(logf), axis=1))
    cs = carry_ref[...] + sums[:, :LANES] + sums[:, LANES:2 * LANES] + sums[:, 2 * LANES:]
    carry_ref[...] = cs[TM - 1:TM, :]

    fq_ref[...] = q_norm(fq_acc, gfq_ref)
    dq_ref[...] = q_norm(dq_acc, gdq_ref)
    fk = k_norm(fk_acc, gfk_ref)
    dk_ref[...] = k_norm(dk_acc, gdk_ref)
    v_store(fv_acc, fv_ref)
    v_store(dv_acc, dv_ref)

    extra = _dot(jnp.concatenate(_split3(cs * LOG2E), axis=1), place_ref[...]).astype(BF16)
    for hb in range(GROUP // LANES):
        fk_ref[:, 2 * hb * LANES:(2 * hb + 1) * LANES] = fk[:, hb * LANES:(hb + 1) * LANES]
        fk_ref[:, (2 * hb + 1) * LANES:(2 * hb + 2) * LANES] = extra[:, hb * LANES:(hb + 1) * LANES]


def _two_map_queries(qT_blk):
    row = lax.broadcasted_iota(jnp.int32, qT_blk.shape, 0)
    zero = jnp.zeros_like(qT_blk)
    return jnp.concatenate([jnp.where(row < HEAD_DIM, qT_blk, zero),
                            jnp.where(row >= HEAD_DIM, qT_blk, zero)], axis=1)


def _softmax_step(sT, vT_blk, carry):
    m, l, acc = carry
    m_new = jnp.maximum(m, jnp.max(sT, axis=0, keepdims=True))
    alpha = jnp.exp2(m - m_new)
    pT = jnp.exp2(sT - m_new)
    l = alpha * l + jnp.sum(pT, axis=0, keepdims=True)
    acc = alpha * acc + _dot(vT_blk, pT.astype(BF16))
    return m_new, l, acc


def _causal_keep():
    key = lax.broadcasted_iota(jnp.int32, (TK, 2 * TQ), 0)
    col = lax.broadcasted_iota(jnp.int32, (TK, 2 * TQ), 1)
    qry = jnp.where(col >= TQ, col - TQ, col)
    return key <= qry


def _init_carry():
    return (jnp.full((1, 2 * TQ), NEG_INF, F32), jnp.zeros((1, 2 * TQ), F32),
            jnp.zeros((LANES, 2 * TQ), F32))


def _causal_sweep(seq, k_ref, vT_ref, queries, adjust, finish):
    pairs = [(qi, kj) for qi in range(seq // TQ) for kj in range(qi + 1)]
    q_ops = {}

    def score(qi, kj):
        if qi not in q_ops:
            q_ops[qi] = queries(qi)
        return _dot(k_ref[kj * TK:(kj + 1) * TK, :], q_ops[qi])

    pending = [score(*pr) for pr in pairs[:LOOKAHEAD]]
    carry = None
    for n, (qi, kj) in enumerate(pairs):
        if n + LOOKAHEAD < len(pairs):
            pending.append(score(*pairs[n + LOOKAHEAD]))
        if kj == 0:
            carry = _init_carry()
        carry = _softmax_step(adjust(pending.pop(0), qi, kj), vT_ref[kj], carry)
        if kj == qi:
            m, l, acc = carry
            finish(qi, acc * (1.0 / l))


def _fox_kernel(qT_ref, k_ref, vT_ref, o_ref):
    seq = k_ref.shape[0]
    keep = _causal_keep()
    row = lax.broadcasted_iota(jnp.int32, (LANES, TQ), 0)
    arow = lax.broadcasted_iota(jnp.int32, (LANES, 2 * TQ), 0)
    acol = lax.broadcasted_iota(jnp.int32, (LANES, 2 * TQ), 1)
    first = (acol < TQ) & (arow < N_SPLIT)
    second = (acol >= TQ) & (arow >= N_SPLIT) & (arow < 2 * N_SPLIT)
    minus_c = jnp.where(first | second, -1.0, 0.0).astype(BF16)

    def queries(qi):
        return jnp.concatenate([_two_map_queries(qT_ref[:, qi * TQ:(qi + 1) * TQ]), minus_c], axis=0)

    def adjust(sT, qi, kj):
        return jnp.where(keep, sT, NEG_INF) if kj == qi else sT

    def finish(qi, oT):
        o = jnp.where(row < HEAD_DIM, oT[:, :TQ], oT[:, TQ:])
        o_ref[qi * TQ:(qi + 1) * TQ, :] = o.T.astype(BF16)

    _causal_sweep(seq, k_ref, vT_ref, queries, adjust, finish)


def _diff_kernel(lam_init, qT_ref, k_ref, vT_ref, bias_ref, lam_ref, gsub_ref, o_ref):
    seq = k_ref.shape[0]
    lp = lam_ref[...]
    lam = (jnp.exp(jnp.sum(lp[0:1] * lp[1:2], axis=1, keepdims=True))
           - jnp.exp(jnp.sum(lp[2:3] * lp[3:4], axis=1, keepdims=True)) + lam_init)

    def queries(qi):
        return _two_map_queries(qT_ref[:, qi * TQ:(qi + 1) * TQ])

    def adjust(sT, qi, kj):
        if kj >= qi - 1:
            b = bias_ref[qi - kj]
            sT = jnp.concatenate([sT[:, :TQ] + b, sT[:, TQ:] + b], axis=1)
        return sT

    def finish(qi, oT):
        o = oT[:, :TQ] - lam * oT[:, TQ:]
        y = o * lax.rsqrt(jnp.mean(o * o, axis=0, keepdims=True) + NORM_EPS) * gsub_ref[...]
        o_ref[qi * TQ:(qi + 1) * TQ, :] = y.T.astype(BF16)

    _causal_sweep(seq, k_ref, vT_ref, queries, adjust, finish)


def _mix_kernel(x_ref, fox_ref, dif_ref, wa_ref, wb_ref, g_ref, wr_hi_ref, wr_lo_ref, br_ref,
                h_ref, xm_ref, topi_ref, gate_ref, cnt_ref, carry_ref):
    @pl.when(pl.program_id(0) == 0)
    def _():
        carry_ref[...] = jnp.zeros_like(carry_ref)

    halves = [slice(0, TM // 2), slice(TM // 2, TM)]
    hs = [x_ref[rows, :] + _dot(fox_ref[rows, :], wa_ref[...]) + _dot(dif_ref[rows, :], wb_ref[...])
          for rows in halves]
    counts = carry_ref[...]
    for rows, h in zip(halves, hs):
        h_ref[rows, :] = h
        xm = h * lax.rsqrt(jnp.mean(h * h, axis=-1, keepdims=True) + NORM_EPS) * g_ref[...]
        xm_ref[rows, 0, :] = xm

        x_hi = xm.astype(BF16)
        x_lo = (xm - x_hi.astype(F32)).astype(BF16)
        logits = (_dot(x_hi, wr_hi_ref[...]) + _dot(x_hi, wr_lo_ref[...]) + _dot(x_lo, wr_hi_ref[...])
                  + br_ref[...])

        lane = lax.broadcasted_iota(jnp.int32, logits.shape, 1)
        vals, sels, idxs = [], [], []
        for _ in range(TOP_K):
            mx = jnp.max(logits, axis=1, keepdims=True)
            idx = jnp.min(jnp.where(logits == mx, lane, LANES), axis=1, keepdims=True)
            sel = lane == idx
            logits = jnp.where(sel, -jnp.inf, logits)
            vals.append(mx)
            sels.append(sel)
            idxs.append(idx)
        exps = [jnp.exp(v - vals[0]) for v in vals]
        denom = exps[0] + exps[1] + exps[2] + exps[3]

        multi_hot = sum(s.astype(F32) for s in sels)
        topi = jnp.zeros(logits.shape, jnp.int32)
        gate = jnp.zeros(logits.shape, F32)
        for k in range(TOP_K):
            topi = jnp.where(lane == k, idxs[k], topi)
            gate = jnp.where(lane == k, exps[k] / denom, gate)
        topi_ref[rows, :] = topi
        gate_ref[rows, :] = gate
        counts = counts + jnp.sum(multi_hot, axis=0, keepdims=True)
    carry_ref[...] = counts
    cnt_ref[...] = counts


def _expert_kernel(be_ref, nu_ref, off_ref, nv_ref, tok_hbm, asg_hbm, x_hbm, w1f_ref, b1_ref, w2f_ref, b2_ref,
                   perm_ref, y_hbm, tok_ref, asg_ref, x0, x1, y0, y1, w1_ref, w2_ref, h_ref, idx_sem, g_sem, s_sem):
    i = pl.program_id(0)
    n_used = nu_ref[0]
    n_blocks = be_ref.shape[0]
    spare_row = y_hbm.shape[0] - MOE_BLOCK
    xbufs, ybufs = (x0, x1), (y0, y1)

    def tok_copy(blk, slot):
        start = pl.multiple_of(off_ref[blk], IDX_ALIGN)
        return pltpu.make_async_copy(tok_hbm.at[pl.ds(start, MOE_BLOCK)], tok_ref.at[slot], idx_sem.at[0, slot])

    def asg_copy(blk, slot):
        start = pl.multiple_of(off_ref[blk], IDX_ALIGN)
        return pltpu.make_async_copy(asg_hbm.at[pl.ds(start, MOE_BLOCK)], asg_ref.at[slot], idx_sem.at[1, slot])

    def asg_ready(blk, slot):
        asg_copy(0, slot).wait()

        def fix(r, carry):
            asg_ref[slot, r] = spare_row + r
            return carry
        lax.fori_loop(nv_ref[blk], MOE_BLOCK, fix, 0)

    def row_loop(fn):
        for r in range(MOE_BLOCK):
            fn(r)

    def gather_issue(slot):
        row_loop(lambda r: pltpu.make_async_copy(x_hbm.at[tok_ref[slot, r]], xbufs[slot].at[pl.ds(r, 1)],
                                                 g_sem.at[slot]).start())

    def gather_wait(slot):
        pltpu.make_async_copy(xbufs[slot], xbufs[slot], g_sem.at[slot]).wait()

    def scatter_issue(slot):
        row_loop(lambda r: pltpu.make_async_copy(ybufs[slot].at[pl.ds(r, 1)], y_hbm.at[asg_ref[slot, r]],
                                                 s_sem.at[slot]).start())

    def scatter_wait(slot):
        pltpu.make_async_copy(ybufs[slot], ybufs[slot], s_sem.at[slot]).wait()

    @pl.when(i == 0)
    def _():
        y1[...] = jnp.zeros(y1.shape, y1.dtype)
        for r in range(MOE_BLOCK):
            asg_ref[1, r] = spare_row + r
        tok_copy(0, 0).start()
        tok_copy(0, 0).wait()
        tok_copy(jnp.minimum(1, n_blocks - 1), 1).start()
        gather_issue(0)

    @pl.when((i < n_used) & ((i == 0) | (be_ref[i] != be_ref[jnp.maximum(i - 1, 0)])))
    def _():
        for g in range(w1f_ref.shape[1] // (2 * LANES)):
            cols = slice(2 * g * LANES, 2 * (g + 1) * LANES)
            w1_ref[:, cols] = _dot(w1f_ref[:, cols].astype(BF16), perm_ref[...]).astype(BF16)
        w2_ref[...] = w2f_ref[...].astype(BF16)

    def step(cur):
        nxt = 1 - cur
        asg_copy(i, cur).start()
        tok_copy(0, nxt).wait()

        pl.when(i >= 1)(functools.partial(asg_ready, i - 1, nxt))

        gather_wait(cur)
        gather_issue(nxt)
        tok_copy(jnp.minimum(i + 2, n_blocks - 1), cur).start()
        xbuf, ybuf = xbufs[cur], ybufs[cur]
        h_ref[...] = _dot(xbuf[...].astype(BF16), w1_ref[...]) + b1_ref[...]

        pl.when(i >= 1)(functools.partial(scatter_wait, cur))

        @pl.when(n_used > 0)
        def _():
            scatter_issue(nxt)
            acts = []
            for c in range(h_ref.shape[1] // (2 * LANES)):
                glu = jnp.minimum(h_ref[:, 2 * c * LANES:(2 * c + 1) * LANES], SWIGLU_LIMIT)
                lin = jnp.clip(h_ref[:, (2 * c + 1) * LANES:(2 * c + 2) * LANES], -SWIGLU_LIMIT, SWIGLU_LIMIT)
                acts.append((glu * jax.nn.sigmoid(SWIGLU_ALPHA * glu) * (lin + 1.0)).astype(BF16))
            ybuf[...] = _dot(jnp.concatenate(acts, axis=1), w2_ref[...]) + b2_ref[...]

        @pl.when(i == n_used - 1)
        def _():
            asg_ready(i, cur)
            scatter_issue(cur)
            scatter_wait(nxt)
            scatter_wait(cur)
            gather_wait(nxt)
            tok_copy(0, cur).wait()

    for parity in range(2):
        pl.when((i < n_used) & (i % 2 == parity))(functools.partial(step, parity))


def _final_kernel(h_ref, *rest):
    y_refs, (gate_ref, g_ref, wg_ref, p_ref, wp_ref, o_ref) = rest[:TOP_K], rest[TOP_K:]
    for rows in (slice(0, TM // 2), slice(TM // 2, TM)):
        gates = gate_ref[rows, :]
        moe = sum(gates[:, k:k + 1] * y_ref[rows, 0, :] for k, y_ref in enumerate(y_refs))
        h = h_ref[rows, :] + moe
        n = (h * lax.rsqrt(jnp.mean(h * h, axis=-1, keepdims=True) + NORM_EPS) * g_ref[...]).astype(BF16)
        gate = jax.nn.sigmoid(_dot(n, wg_ref[...]))
        o_ref[rows, :] = h + gate * _dot(p_ref[rows, :].astype(BF16), wp_ref[...])


def _t5_bucket(rel):
    n = jnp.maximum(rel, 0)
    max_exact = NUM_BUCKETS // 2
    nf = jnp.maximum(n, 1).astype(F32)
    large = max_exact + (jnp.log(nf / max_exact) / math.log(MAX_DISTANCE / max_exact)
                         * (NUM_BUCKETS - max_exact)).astype(jnp.int32)
    large = jnp.minimum(large, NUM_BUCKETS - 1)
    return jnp.where(n < max_exact, n, large)


def _full(shape):
    return pl.BlockSpec(shape, lambda *_: (0,) * len(shape))


def _layer(i, h_in, p_i, rel_bias, prm):
    bsz, seq, d = h_in.shape
    n_tok = bsz * seq
    ns = seq // TM
    nk = seq // TK
    x2 = h_in.reshape(n_tok, d)

    w_in = prm['w_in']
    fox_w = N_FOX_HEADS * HEAD_DIM
    offs = [0, fox_w, 2 * fox_w, 3 * fox_w, 3 * fox_w + N_FOX_HEADS]
    offs += [offs[4] + GROUP, offs[4] + 2 * GROUP, offs[4] + 3 * GROUP]
    col = lambda a, b: w_in[:, a:b].astype(BF16)
    wfqT, wfk, wfvT = col(offs[0], offs[1]).T, col(offs[1], offs[2]), col(offs[2], offs[3]).T
    wdqT, wdk, wdvT = col(offs[4], offs[5]).T, col(offs[5], offs[6]), col(offs[6], offs[7]).T
    wf = jnp.zeros((d, LANES), BF16).at[:, :N_FOX_HEADS].set(col(offs[3], offs[4]))
    bf = jnp.zeros((1, LANES), F32).at[0, :N_FOX_HEADS].set(prm['b_f'])
    scale = HEAD_DIM ** -0.5 * LOG2E
    rep = GROUP // HEAD_DIM
    heads = jnp.arange(N_FOX_HEADS)
    place = jnp.concatenate([jnp.zeros((LANES, GROUP), BF16)
                             .at[heads, (heads // 2) * LANES + (heads % 2) * N_SPLIT + t].set(1.0)
                             for t in range(N_SPLIT)])
    gfqT = jnp.broadcast_to((jnp.tile(prm['g_fox_q'], rep) * scale)[:, None], (GROUP, TM))
    gdqT = jnp.broadcast_to((jnp.tile(prm['g_diff_q'], rep) * scale)[:, None], (GROUP, TM))
    gfk = jnp.tile(prm['g_fox_k'], rep)[None, :]
    gdk = jnp.tile(prm['g_diff_k'], rep)[None, :]
    gi = jnp.arange(GROUP // 2) // HEAD_DIM
    seg = jnp.where(gi[:, None] == gi[None, :], 1.0 / HEAD_DIM, 0.0).astype(BF16)
    ti = jnp.arange(TM)
    tri_incl = (ti[None, :] <= ti[:, None]).astype(BF16)

    w_spec_t = _full((GROUP, d))
    w_spec = _full((d, GROUP))
    qT_shape = jax.ShapeDtypeStruct((bsz, GROUP, seq), BF16)
    k_shape = jax.ShapeDtypeStruct((n_tok, GROUP), BF16)
    vT_shape = jax.ShapeDtypeStruct((bsz, GROUP // LANES, nk, LANES, TK), BF16)
    qT_spec = pl.BlockSpec((None, GROUP, TM), lambda b, s: (b, 0, s))
    k_spec = pl.BlockSpec((TM, GROUP), lambda b, s: (b * ns + s, 0))
    vT_spec = pl.BlockSpec((None, GROUP // LANES, TM // TK, LANES, TK), lambda b, s: (b, 0, s, 0, 0))
    fk_shape = jax.ShapeDtypeStruct((n_tok, 2 * GROUP), BF16)
    fk_spec = pl.BlockSpec((TM, 2 * GROUP), lambda b, s: (b * ns + s, 0))
    fqT, fk, fvT, dqT, dk, dvT = pl.pallas_call(
        _inproj_kernel,
        grid=(bsz, ns),
        in_specs=[pl.BlockSpec((TM, d), lambda b, s: (b * ns + s, 0)), _full((1, d)),
                  w_spec_t, w_spec, w_spec_t, w_spec_t, w_spec, w_spec_t, _full((d, LANES)), _full((1, LANES)),
                  _full((GROUP, TM)), _full((1, GROUP)), _full((GROUP, TM)), _full((1, GROUP)),
                  _full((GROUP // 2, GROUP // 2)), _full((TM, TM)), _full((N_SPLIT * LANES, GROUP))],
        out_specs=[qT_spec, fk_spec, vT_spec, qT_spec, k_spec, vT_spec],
        out_shape=[qT_shape, fk_shape, vT_shape, qT_shape, k_shape, vT_shape],
        scratch_shapes=[pltpu.VMEM((1, LANES), F32)],
        compiler_params=_params(2),
        name="inproj",
    )(x2, prm['g_attn'][None, :], wfqT, wfk, wfvT, wdqT, wdk, wdvT, wf, bf, gfqT, gfk, gdqT, gdk, seg, tri_incl,
      place)

    n_hb = GROUP // LANES
    att_q = pl.BlockSpec((None, LANES, seq), lambda b, h: (b, h, 0))
    att_k = pl.BlockSpec((seq, LANES), lambda b, h: (b, h))
    att_v = pl.BlockSpec((None, None, nk, LANES, TK), lambda b, h: (b, h, 0, 0, 0))
    att_o = pl.BlockSpec((seq, LANES), lambda b, h: (b, h))
    fox_out = pl.pallas_call(
        _fox_kernel,
        grid=(bsz, n_hb),
        in_specs=[att_q, pl.BlockSpec((seq, 2 * LANES), lambda b, h: (b, h)), att_v],
        out_specs=att_o,
        out_shape=jax.ShapeDtypeStruct((n_tok, GROUP), BF16),
        compiler_params=_params(2),
        name="fox_attention",
    )(fqT, fk, fvT)

    kpos = jnp.arange(TK)[:, None]
    qpos = jnp.arange(TQ)[None, :]
    rel = jnp.stack([qpos - kpos, qpos - kpos + TK])
    assert TK + 1 >= MAX_DISTANCE and TQ == TK
    table = (rel_bias.astype(F32) - rel_bias[NUM_BUCKETS - 1].astype(F32)[None, :]) * LOG2E
    onehot = _t5_bucket(rel)[None, ..., None] == jnp.arange(NUM_BUCKETS, dtype=jnp.int32)
    biasT = jnp.sum(jnp.where(onehot, table.T[:, None, None, None, :], 0.0), axis=-1)
    biasT = jnp.where((rel >= 0)[None], biasT, NEG_INF)
    lam_init = 0.8 - 0.6 * math.exp(-0.3 * i)
    lam_p = jnp.stack([prm['lambda_q1'], prm['lambda_k1'], prm['lambda_q2'], prm['lambda_k2']]).astype(F32)
    gsubT = jnp.broadcast_to((prm['g_subln'] * (1.0 - lam_init))[:, None], (LANES, TQ))
    diff_out = pl.pallas_call(
        functools.partial(_diff_kernel, lam_init),
        grid=(bsz, N_DIFF_HEADS),
        in_specs=[att_q, att_k, att_v, pl.BlockSpec((None, 2, TK, TQ), lambda b, h: (h, 0, 0, 0)),
                  _full((4, HEAD_DIM)), _full((LANES, TQ))],
        out_specs=att_o,
        out_shape=jax.ShapeDtypeStruct((n_tok, GROUP), BF16),
        compiler_params=_params(2),
        name="diff_attention",
    )(dqT, dk, dvT, biasT, lam_p, gsubT)

    w_out = prm['w_out'].astype(BF16)
    wr = jnp.zeros((d, LANES), F32).at[:, :N_EXPERTS].set(prm['w_router'])
    wr_hi = wr.astype(BF16)
    wr_lo = (wr - wr_hi.astype(F32)).astype(BF16)
    br = jnp.full((1, LANES), NEG_INF, F32).at[0, :N_EXPERTS].set(prm['b_router'])
    row_spec = lambda w: pl.BlockSpec((TM, w), lambda t: (t, 0))
    h1, xm, topi, gates, counts = pl.pallas_call(
        _mix_kernel,
        grid=(n_tok // TM,),
        in_specs=[row_spec(d), row_spec(GROUP), row_spec(GROUP), _full((GROUP, d)), _full((GROUP, d)),
                  _full((1, d)), _full((d, LANES)), _full((d, LANES)), _full((1, LANES))],
        out_specs=[row_spec(d), pl.BlockSpec((TM, 1, d), lambda t: (t, 0, 0)), row_spec(LANES),
                   row_spec(LANES), _full((1, LANES))],
        out_shape=[jax.ShapeDtypeStruct((n_tok, d), F32), jax.ShapeDtypeStruct((n_tok, 1, d), F32),
                   jax.ShapeDtypeStruct((n_tok, LANES), jnp.int32), jax.ShapeDtypeStruct((n_tok, LANES), F32),
                   jax.ShapeDtypeStruct((1, LANES), F32)],
        scratch_shapes=[pltpu.VMEM((1, LANES), F32)],
        compiler_params=_params(1),
        name="mix_router",
    )(x2, fox_out, diff_out, w_out[:GROUP], w_out[GROUP:], prm['g_mlp'][None, :], wr_hi, wr_lo, br)

    m = n_tok * TOP_K
    n_blocks = (m + N_EXPERTS * MOE_BLOCK + MOE_BLOCK - 1) // MOE_BLOCK
    cnt = counts[0, :N_EXPERTS].astype(jnp.int32)
    padded = ((cnt + MOE_BLOCK - 1) // MOE_BLOCK) * MOE_BLOCK
    pad_ends = jnp.cumsum(padded)
    pad_starts = pad_ends - padded
    block_start = jnp.arange(n_blocks, dtype=jnp.int32) * MOE_BLOCK
    block_expert = jnp.minimum(jnp.sum(pad_ends[None, :] <= block_start[:, None], axis=1),
                               N_EXPERTS - 1).astype(jnp.int32)
    n_used = (pad_ends[-1] // MOE_BLOCK).astype(jnp.int32)[None]
    experts = jnp.arange(N_EXPERTS, dtype=jnp.int32)
    real = topi[:, :TOP_K] * (2 * m) + jnp.arange(m, dtype=jnp.int32).reshape(n_tok, TOP_K)
    fill = jnp.arange(IDX_ALIGN - 1, dtype=jnp.int32)[None, :]
    need = (-cnt) % IDX_ALIGN
    filler = jnp.where(fill < need[:, None], experts[:, None] * (2 * m) + m + fill,
                       N_EXPERTS * (2 * m) + experts[:, None] * IDX_ALIGN + fill)
    keys = jnp.sort(jnp.concatenate([real.reshape(m), filler.reshape(-1)]), stable=False)
    flat = jnp.concatenate([(keys % (2 * m)) % m, jnp.zeros((MOE_BLOCK,), jnp.int32)])
    grouped_tok = flat // TOP_K
    grouped_row = (flat % TOP_K) * n_tok + grouped_tok
    onehot_e = block_expert[:, None] == experts[None, :]
    pick = lambda v: jnp.sum(jnp.where(onehot_e, v[None, :], 0), axis=1)
    in_group = block_start - pick(pad_starts)
    n_valid = jnp.clip(pick(cnt) - in_group, 0, MOE_BLOCK)
    group_start = jnp.cumsum(cnt + need) - (cnt + need)
    first = jnp.clip(pick(group_start) + in_group, 0, keys.shape[0]) // IDX_ALIGN * IDX_ALIGN

    d_ff = prm['w2'].shape[1]
    n_grp = 2 * d_ff // (2 * LANES)
    b1p = prm['b1'].reshape(N_EXPERTS, n_grp, LANES, 2).transpose(0, 1, 3, 2).reshape(N_EXPERTS, 1, 2 * d_ff)
    b2 = prm['b2'].reshape(N_EXPERTS, 1, d)
    pj = jnp.arange(LANES)
    perm = (jnp.zeros((2 * LANES, 2 * LANES), BF16).at[2 * pj, pj].set(1.0)
            .at[2 * pj + 1, LANES + pj].set(1.0))
    any_spec = pl.BlockSpec(memory_space=pl.ANY)
    ys = pl.pallas_call(
        _expert_kernel,
        grid_spec=pltpu.PrefetchScalarGridSpec(
            num_scalar_prefetch=4,
            grid=(n_blocks,),
            in_specs=[any_spec, any_spec, any_spec,
                      pl.BlockSpec((None, d, 2 * d_ff), lambda t, be, *_:(be[t], 0, 0)),
                      pl.BlockSpec((None, 1, 2 * d_ff), lambda t, be, *_:(be[t], 0, 0)),
                      pl.BlockSpec((None, d_ff, d), lambda t, be, *_:(be[t], 0, 0)),
                      pl.BlockSpec((None, 1, d), lambda t, be, *_:(be[t], 0, 0)),
                      pl.BlockSpec((2 * LANES, 2 * LANES), lambda t, be, *_:(0, 0))],
            out_specs=any_spec,
            scratch_shapes=[pltpu.SMEM((2, MOE_BLOCK), jnp.int32), pltpu.SMEM((2, MOE_BLOCK), jnp.int32)]
            + [pltpu.VMEM((MOE_BLOCK, d), F32)] * 4
            + [pltpu.VMEM((d, 2 * d_ff), BF16), pltpu.VMEM((d_ff, d), BF16), pltpu.VMEM((MOE_BLOCK, 2 * d_ff), F32)]
            + [pltpu.SemaphoreType.DMA((2, 2)), pltpu.SemaphoreType.DMA((2,)), pltpu.SemaphoreType.DMA((2,))],
        ),
        out_shape=jax.ShapeDtypeStruct((m + MOE_BLOCK, 1, d), F32),
        compiler_params=pltpu.CompilerParams(dimension_semantics=("arbitrary",),
                                             vmem_limit_bytes=EXPERT_VMEM_LIMIT),
        name="experts",
    )(block_expert, n_used, first, n_valid, grouped_tok, grouped_row, xm, prm['w1'], b1p, prm['w2'], b2, perm)

    ple = p_i.shape[-1]
    out = pl.pallas_call(
        _final_kernel,
        grid=(n_tok // TM,),
        in_specs=[row_spec(d)]
        + [pl.BlockSpec((TM, 1, d), functools.partial(lambda k, t: (k * (n_tok // TM) + t, 0, 0), k))
           for k in range(TOP_K)]
        + [row_spec(LANES),
                  _full((1, d)), _full((d, d)), row_spec(ple), _full((ple, d))],
        out_specs=row_spec(d),
        out_shape=jax.ShapeDtypeStruct((n_tok, d), F32),
        compiler_params=_params(1),
        name="combine_ple",
    )(h1, ys, ys, ys, ys, gates, prm['g_ple'][None, :],
      prm['w_ple_gate'].astype(BF16), p_i.reshape(n_tok, ple), prm['w_ple_proj'].astype(BF16))
    return out.reshape(bsz, seq, d)


def kernel(x, p, rel_bias, g_attn, w_in, b_f, g_fox_q, g_fox_k, g_diff_q, g_diff_k, lambda_q1, lambda_k1,
           lambda_q2, lambda_k2, g_subln, w_out, g_mlp, w_router, b_router, w1, b1, w2, b2, g_ple,
           w_ple_gate, w_ple_proj):
    stacked = dict(g_attn=g_attn, w_in=w_in, b_f=b_f, g_fox_q=g_fox_q, g_fox_k=g_fox_k, g_diff_q=g_diff_q,
                   g_diff_k=g_diff_k, lambda_q1=lambda_q1, lambda_k1=lambda_k1, lambda_q2=lambda_q2,
                   lambda_k2=lambda_k2, g_subln=g_subln, w_out=w_out, g_mlp=g_mlp, w_router=w_router,
                   b_router=b_router, w1=w1, b1=b1, w2=w2, b2=b2, g_ple=g_ple, w_ple_gate=w_ple_gate,
                   w_ple_proj=w_ple_proj)
    h = x
    for i in range(p.shape[0]):
        h = _layer(i, h, p[i], rel_bias, {name: v[i] for name, v in stacked.items()})
    return h
```

```python
import functools
import math

import jax
import jax.numpy as jnp
from jax import lax
from jax.experimental import pallas as pl
from jax.experimental.pallas import tpu as pltpu

F32 = jnp.float32
BF16 = jnp.bfloat16

HEAD_DIM = 64
N_FOX_HEADS = 8
N_DIFF_HEADS = 4
GROUP = 512
LANES = 128
NUM_BUCKETS = 32
MAX_DISTANCE = 128
N_EXPERTS = 32
TOP_K = 4
SWIGLU_ALPHA = 1.702
SWIGLU_LIMIT = 7.0
MOE_BLOCK = 256
IDX_ALIGN = 128
NORM_EPS = 1e-6
NEG_INF = -1e30
LOG2E = math.log2(math.e)
N_SPLIT = 3

TM = 512
TQ = 256
TK = 256
LOOKAHEAD = 2
VMEM_LIMIT = 48 * 1024 * 1024
EXPERT_VMEM_LIMIT = 58 * 1024 * 1024


def _params(n_axes):
    return pltpu.CompilerParams(dimension_semantics=("arbitrary",) * n_axes,
                                vmem_limit_bytes=VMEM_LIMIT)


def _dot(a, b):
    return jnp.dot(a, b, preferred_element_type=F32)


def _dot_nt(a, b):
    return lax.dot_general(a, b, (((1,), (1,)), ((), ())), preferred_element_type=F32)


def _split3(v):
    hi = v.astype(BF16)
    r = v - hi.astype(F32)
    mid = r.astype(BF16)
    lo = (r - mid.astype(F32)).astype(BF16)
    return hi, mid, lo


def _inproj_kernel(x_ref, g_ref, wfq_ref, wfk_ref, wfv_ref, wdq_ref, wdk_ref, wdv_ref, wf_ref, bf_ref,
                   gfq_ref, gfk_ref, gdq_ref, gdk_ref, seg_ref, tri_ref, place_ref,
                   fq_ref, fk_ref, fv_ref, dq_ref, dk_ref, dv_ref, carry_ref):
    @pl.when(pl.program_id(1) == 0)
    def _():
        carry_ref[...] = jnp.zeros_like(carry_ref)

    xf = x_ref[...]
    a = (xf * lax.rsqrt(jnp.mean(xf * xf, axis=-1, keepdims=True) + NORM_EPS) * g_ref[...]).astype(BF16)
    seg = seg_ref[...]

    fl = _dot(a, wf_ref[...]) + bf_ref[...]
    fq_acc = _dot_nt(wfq_ref[...], a)
    dq_acc = _dot_nt(wdq_ref[...], a)
    fk_acc = _dot(a, wfk_ref[...])
    dk_acc = _dot(a, wdk_ref[...])
    fv_acc = _dot_nt(wfv_ref[...], a)
    dv_acc = _dot_nt(wdv_ref[...], a)

    half = seg.shape[0]

    def q_norm(acc, gain_ref):
        sq = (acc * acc).astype(BF16)
        ms = jnp.concatenate([_dot(seg, sq[:half]), _dot(seg, sq[half:])], axis=0)
        return (acc * lax.rsqrt(ms + NORM_EPS) * gain_ref[...]).astype(BF16)

    def k_norm(acc, gain_ref):
        sq = (acc * acc).astype(BF16)
        ms = jnp.concatenate([_dot(sq[:, :half], seg), _dot(sq[:, half:], seg)], axis=1)
        return (acc * lax.rsqrt(ms + NORM_EPS) * gain_ref[...]).astype(BF16)

    def v_store(acc, out_ref):
        acc = acc.astype(BF16)
        for hb in range(GROUP // LANES):
            for j in range(TM // TK):
                out_ref[hb, j] = acc[hb * LANES:(hb + 1) * LANES, j * TK:(j + 1) * TK]

    logf = jnp.minimum(fl, 0.0) - jnp.log1p(jnp.exp(-jnp.abs(fl)))
    sums = _dot(tri_ref[...], jnp.concatenate(_split3(logf), axis=1))
    cs = carry_ref[...] + sums[:, :LANES] + sums[:, LANES:2 * LANES] + sums[:, 2 * LANES:]
    carry_ref[...] = cs[TM - 1:TM, :]

    fq_ref[...] = q_norm(fq_acc, gfq_ref)
    dq_ref[...] = q_norm(dq_acc, gdq_ref)
    fk = k_norm(fk_acc, gfk_ref)
    dk_ref[...] = k_norm(dk_acc, gdk_ref)
    v_store(fv_acc, fv_ref)
    v_store(dv_acc, dv_ref)

    extra = _dot(jnp.concatenate(_split3(cs * LOG2E), axis=1), place_ref[...]).astype(BF16)
    for hb in range(GROUP // LANES):
        fk_ref[:, 2 * hb * LANES:(2 * hb + 1) * LANES] = fk[:, hb * LANES:(hb + 1) * LANES]
        fk_ref[:, (2 * hb + 1) * LANES:(2 * hb + 2) * LANES] = extra[:, hb * LANES:(hb + 1) * LANES]


def _two_map_queries(qT_blk):
    row = lax.broadcasted_iota(jnp.int32, qT_blk.shape, 0)
    zero = jnp.zeros_like(qT_blk)
    return jnp.concatenate([jnp.where(row < HEAD_DIM, qT_blk, zero),
                            jnp.where(row >= HEAD_DIM, qT_blk, zero)], axis=1)


def _softmax_step(sT, vT_blk, carry):
    m, l, acc = carry
    m_new = jnp.maximum(m, jnp.max(sT, axis=0, keepdims=True))
    alpha = jnp.exp2(m - m_new)
    pT = jnp.exp2(sT - m_new)
    l = alpha * l + jnp.sum(pT, axis=0, keepdims=True)
    acc = alpha * acc + _dot(vT_blk, pT.astype(BF16))
    return m_new, l, acc


def _causal_keep():
    key = lax.broadcasted_iota(jnp.int32, (TK, 2 * TQ), 0)
    col = lax.broadcasted_iota(jnp.int32, (TK, 2 * TQ), 1)
    qry = jnp.where(col >= TQ, col - TQ, col)
    return key <= qry


def _init_carry():
    return (jnp.full((1, 2 * TQ), NEG_INF, F32), jnp.zeros((1, 2 * TQ), F32),
            jnp.zeros((LANES, 2 * TQ), F32))


def _causal_sweep(seq, k_ref, vT_ref, queries, adjust, finish):
    pairs = [(qi, kj) for qi in range(seq // TQ) for kj in range(qi + 1)]
    q_ops = {}

    def score(qi, kj):
        if qi not in q_ops:
            q_ops[qi] = queries(qi)
        return _dot(k_ref[kj * TK:(kj + 1) * TK, :], q_ops[qi])

    pending = [score(*pr) for pr in pairs[:LOOKAHEAD]]
    carry = None
    for n, (qi, kj) in enumerate(pairs):
        if n + LOOKAHEAD < len(pairs):
            pending.append(score(*pairs[n + LOOKAHEAD]))
        if kj == 0:
            carry = _init_carry()
        carry = _softmax_step(adjust(pending.pop(0), qi, kj), vT_ref[kj], carry)
        if kj == qi:
            m, l, acc = carry
            finish(qi, acc * (1.0 / l))


def _fox_kernel(qT_ref, k_ref, vT_ref, o_ref):
    seq = k_ref.shape[0]
    keep = _causal_keep()
    row = lax.broadcasted_iota(jnp.int32, (LANES, TQ), 0)
    arow = lax.broadcasted_iota(jnp.int32, (LANES, 2 * TQ), 0)
    acol = lax.broadcasted_iota(jnp.int32, (LANES, 2 * TQ), 1)
    first = (acol < TQ) & (arow < N_SPLIT)
    second = (acol >= TQ) & (arow >= N_SPLIT) & (arow < 2 * N_SPLIT)
    minus_c = jnp.where(first | second, -1.0, 0.0).astype(BF16)

    def queries(qi):
        return jnp.concatenate([_two_map_queries(qT_ref[:, qi * TQ:(qi + 1) * TQ]), minus_c], axis=0)

    def adjust(sT, qi, kj):
        return jnp.where(keep, sT, NEG_INF) if kj == qi else sT

    def finish(qi, oT):
        o = jnp.where(row < HEAD_DIM, oT[:, :TQ], oT[:, TQ:])
        o_ref[qi * TQ:(qi + 1) * TQ, :] = o.T.astype(BF16)

    _causal_sweep(seq, k_ref, vT_ref, queries, adjust, finish)


def _diff_kernel(lam_init, qT_ref, k_ref, vT_ref, bias_ref, lam_ref, gsub_ref, o_ref):
    seq = k_ref.shape[0]
    lp = lam_ref[...]
    lam = (jnp.exp(jnp.sum(lp[0:1] * lp[1:2], axis=1, keepdims=True))
           - jnp.exp(jnp.sum(lp[2:3] * lp[3:4], axis=1, keepdims=True)) + lam_init)

    def queries(qi):
        return _two_map_queries(qT_ref[:, qi * TQ:(qi + 1) * TQ])

    def adjust(sT, qi, kj):
        if kj >= qi - 1:
            b = bias_ref[qi - kj]
            sT = jnp.concatenate([sT[:, :TQ] + b, sT[:, TQ:] + b], axis=1)
        return sT

    def finish(qi, oT):
        o = oT[:, :TQ] - lam * oT[:, TQ:]
        y = o * lax.rsqrt(jnp.mean(o * o, axis=0, keepdims=True) + NORM_EPS) * gsub_ref[...]
        o_ref[qi * TQ:(qi + 1) * TQ, :] = y.T.astype(BF16)

    _causal_sweep(seq, k_ref, vT_ref, queries, adjust, finish)


def _mix_kernel(x_ref, fox_ref, dif_ref, wa_ref, wb_ref, g_ref, wr_hi_ref, wr_lo_ref, br_ref,
                h_ref, xm_ref, topi_ref, gate_ref, cnt_ref, carry_ref):
    @pl.when(pl.program_id(0) == 0)
    def _():
        carry_ref[...] = jnp.zeros_like(carry_ref)

    halves = [slice(0, TM // 2), slice(TM // 2, TM)]
    hs = [x_ref[rows, :] + _dot(fox_ref[rows, :], wa_ref[...]) + _dot(dif_ref[rows, :], wb_ref[...])
          for rows in halves]
    counts = carry_ref[...]
    for rows, h in zip(halves, hs):
        h_ref[rows, :] = h
        xm = h * lax.rsqrt(jnp.mean(h * h, axis=-1, keepdims=True) + NORM_EPS) * g_ref[...]
        xm_ref[rows, 0, :] = xm

        x_hi = xm.astype(BF16)
        x_lo = (xm - x_hi.astype(F32)).astype(BF16)
        logits = (_dot(x_hi, wr_hi_ref[...]) + _dot(x_hi, wr_lo_ref[...]) + _dot(x_lo, wr_hi_ref[...])
                  + br_ref[...])

        lane = lax.broadcasted_iota(jnp.int32, logits.shape, 1)
        vals, sels, idxs = [], [], []
        for _ in range(TOP_K):
            mx = jnp.max(logits, axis=1, keepdims=True)
            idx = jnp.min(jnp.where(logits == mx, lane, LANES), axis=1, keepdims=True)
            sel = lane == idx
            logits = jnp.where(sel, -jnp.inf, logits)
            vals.append(mx)
            sels.append(sel)
            idxs.append(idx)
        exps = [jnp.exp(v - vals[0]) for v in vals]
        denom = exps[0] + exps[1] + exps[2] + exps[3]

        multi_hot = sum(s.astype(F32) for s in sels)
        topi = jnp.zeros(logits.shape, jnp.int32)
        gate = jnp.zeros(logits.shape, F32)
        for k in range(TOP_K):
            topi = jnp.where(lane == k, idxs[k], topi)
            gate = jnp.where(lane == k, exps[k] / denom, gate)
        topi_ref[rows, :] = topi
        gate_ref[rows, :] = gate
        counts = counts + jnp.sum(multi_hot, axis=0, keepdims=True)
    carry_ref[...] = counts
    cnt_ref[...] = counts


def _expert_kernel(be_ref, nu_ref, off_ref, nv_ref, tok_hbm, asg_hbm, x_hbm, w1f_ref, b1_ref, w2f_ref, b2_ref,
                   perm_ref, y_hbm, tok_ref, asg_ref, x0, x1, y0, y1, w1_ref, w2_ref, h_ref, idx_sem, g_sem, s_sem):
    i = pl.program_id(0)
    n_used = nu_ref[0]
    n_blocks = be_ref.shape[0]
    spare_row = y_hbm.shape[0] - MOE_BLOCK
    xbufs, ybufs = (x0, x1), (y0, y1)

    def tok_copy(blk, slot):
        start = pl.multiple_of(off_ref[blk], IDX_ALIGN)
        return pltpu.make_async_copy(tok_hbm.at[pl.ds(start, MOE_BLOCK)], tok_ref.at[slot], idx_sem.at[0, slot])

    def asg_copy(blk, slot):
        start = pl.multiple_of(off_ref[blk], IDX_ALIGN)
        return pltpu.make_async_copy(asg_hbm.at[pl.ds(start, MOE_BLOCK)], asg_ref.at[slot], idx_sem.at[1, slot])

    def asg_ready(blk, slot):
        asg_copy(0, slot).wait()

        def fix(r, carry):
            asg_ref[slot, r] = spare_row + r
            return carry
        lax.fori_loop(nv_ref[blk], MOE_BLOCK, fix, 0)

    def row_loop(fn):
        for r in range(MOE_BLOCK):
            fn(r)

    def gather_issue(slot):
        row_loop(lambda r: pltpu.make_async_copy(x_hbm.at[tok_ref[slot, r]], xbufs[slot].at[pl.ds(r, 1)],
                                                 g_sem.at[slot]).start(priority=r % 2))

    def gather_wait(slot):
        pltpu.make_async_copy(xbufs[slot], xbufs[slot], g_sem.at[slot]).wait()

    def scatter_issue(slot):
        row_loop(lambda r: pltpu.make_async_copy(ybufs[slot].at[pl.ds(r, 1)], y_hbm.at[asg_ref[slot, r]],
                                                 s_sem.at[slot]).start(priority=r % 2))

    def scatter_wait(slot):
        pltpu.make_async_copy(ybufs[slot], ybufs[slot], s_sem.at[slot]).wait()

    @pl.when(i == 0)
    def _():
        y1[...] = jnp.zeros(y1.shape, y1.dtype)
        for r in range(MOE_BLOCK):
            asg_ref[1, r] = spare_row + r
        tok_copy(0, 0).start()
        tok_copy(0, 0).wait()
        tok_copy(jnp.minimum(1, n_blocks - 1), 1).start()
        gather_issue(0)

    @pl.when((i < n_used) & ((i == 0) | (be_ref[i] != be_ref[jnp.maximum(i - 1, 0)])))
    def _():
        for g in range(w1f_ref.shape[1] // (2 * LANES)):
            cols = slice(2 * g * LANES, 2 * (g + 1) * LANES)
            w1_ref[:, cols] = _dot(w1f_ref[:, cols].astype(BF16), perm_ref[...]).astype(BF16)
        w2_ref[...] = w2f_ref[...].astype(BF16)

    def step(cur):
        nxt = 1 - cur
        asg_copy(i, cur).start()
        tok_copy(0, nxt).wait()

        pl.when(i >= 1)(functools.partial(asg_ready, i - 1, nxt))

        gather_wait(cur)
        gather_issue(nxt)
        tok_copy(jnp.minimum(i + 2, n_blocks - 1), cur).start()
        xbuf, ybuf = xbufs[cur], ybufs[cur]
        h_ref[...] = _dot(xbuf[...].astype(BF16), w1_ref[...]) + b1_ref[...]

        pl.when(i >= 1)(functools.partial(scatter_wait, cur))

        @pl.when(n_used > 0)
        def _():
            scatter_issue(nxt)
            acts = []
            for c in range(h_ref.shape[1] // (2 * LANES)):
                glu = jnp.minimum(h_ref[:, 2 * c * LANES:(2 * c + 1) * LANES], SWIGLU_LIMIT)
                lin = jnp.clip(h_ref[:, (2 * c + 1) * LANES:(2 * c + 2) * LANES], -SWIGLU_LIMIT, SWIGLU_LIMIT)
                acts.append((glu * jax.nn.sigmoid(SWIGLU_ALPHA * glu) * (lin + 1.0)).astype(BF16))
            ybuf[...] = _dot(jnp.concatenate(acts, axis=1), w2_ref[...]) + b2_ref[...]

        @pl.when(i == n_used - 1)
        def _():
            asg_ready(i, cur)
            scatter_issue(cur)
            scatter_wait(nxt)
            scatter_wait(cur)
            gather_wait(nxt)
            tok_copy(0, cur).wait()

    for parity in range(2):
        pl.when((i < n_used) & (i % 2 == parity))(functools.partial(step, parity))


def _final_kernel(h_ref, *rest):
    y_refs, (gate_ref, g_ref, wg_ref, p_ref, wp_ref, o_ref) = rest[:TOP_K], rest[TOP_K:]
    for rows in (slice(0, TM // 2), slice(TM // 2, TM)):
        gates = gate_ref[rows, :]
        moe = sum(gates[:, k:k + 1] * y_ref[rows, 0, :] for k, y_ref in enumerate(y_refs))
        h = h_ref[rows, :] + moe
        n = (h * lax.rsqrt(jnp.mean(h * h, axis=-1, keepdims=True) + NORM_EPS) * g_ref[...]).astype(BF16)
        gate = jax.nn.sigmoid(_dot(n, wg_ref[...]))
        o_ref[rows, :] = h + gate * _dot(p_ref[rows, :].astype(BF16), wp_ref[...])


def _t5_bucket(rel):
    n = jnp.maximum(rel, 0)
    max_exact = NUM_BUCKETS // 2
    nf = jnp.maximum(n, 1).astype(F32)
    large = max_exact + (jnp.log(nf / max_exact) / math.log(MAX_DISTANCE / max_exact)
                         * (NUM_BUCKETS - max_exact)).astype(jnp.int32)
    large = jnp.minimum(large, NUM_BUCKETS - 1)
    return jnp.where(n < max_exact, n, large)


def _full(shape):
    return pl.BlockSpec(shape, lambda *_: (0,) * len(shape))


def _layer(i, h_in, p_i, rel_bias, prm):
    bsz, seq, d = h_in.shape
    n_tok = bsz * seq
    ns = seq // TM
    nk = seq // TK
    x2 = h_in.reshape(n_tok, d)

    w_in = prm['w_in']
    fox_w = N_FOX_HEADS * HEAD_DIM
    offs = [0, fox_w, 2 * fox_w, 3 * fox_w, 3 * fox_w + N_FOX_HEADS]
    offs += [offs[4] + GROUP, offs[4] + 2 * GROUP, offs[4] + 3 * GROUP]
    col = lambda a, b: w_in[:, a:b].astype(BF16)
    wfqT, wfk, wfvT = col(offs[0], offs[1]).T, col(offs[1], offs[2]), col(offs[2], offs[3]).T
    wdqT, wdk, wdvT = col(offs[4], offs[5]).T, col(offs[5], offs[6]), col(offs[6], offs[7]).T
    wf = jnp.zeros((d, LANES), BF16).at[:, :N_FOX_HEADS].set(col(offs[3], offs[4]))
    bf = jnp.zeros((1, LANES), F32).at[0, :N_FOX_HEADS].set(prm['b_f'])
    scale = HEAD_DIM ** -0.5 * LOG2E
    rep = GROUP // HEAD_DIM
    heads = jnp.arange(N_FOX_HEADS)
    place = jnp.concatenate([jnp.zeros((LANES, GROUP), BF16)
                             .at[heads, (heads // 2) * LANES + (heads % 2) * N_SPLIT + t].set(1.0)
                             for t in range(N_SPLIT)])
    gfqT = jnp.broadcast_to((jnp.tile(prm['g_fox_q'], rep) * scale)[:, None], (GROUP, TM))
    gdqT = jnp.broadcast_to((jnp.tile(prm['g_diff_q'], rep) * scale)[:, None], (GROUP, TM))
    gfk = jnp.tile(prm['g_fox_k'], rep)[None, :]
    gdk = jnp.tile(prm['g_diff_k'], rep)[None, :]
    gi = jnp.arange(GROUP // 2) // HEAD_DIM
    seg = jnp.where(gi[:, None] == gi[None, :], 1.0 / HEAD_DIM, 0.0).astype(BF16)
    ti = jnp.arange(TM)
    tri_incl = (ti[None, :] <= ti[:, None]).astype(BF16)

    w_spec_t = _full((GROUP, d))
    w_spec = _full((d, GROUP))
    qT_shape = jax.ShapeDtypeStruct((bsz, GROUP, seq), BF16)
    k_shape = jax.ShapeDtypeStruct((n_tok, GROUP), BF16)
    vT_shape = jax.ShapeDtypeStruct((bsz, GROUP // LANES, nk, LANES, TK), BF16)
    qT_spec = pl.BlockSpec((None, GROUP, TM), lambda b, s: (b, 0, s))
    k_spec = pl.BlockSpec((TM, GROUP), lambda b, s: (b * ns + s, 0))
    vT_spec = pl.BlockSpec((None, GROUP // LANES, TM // TK, LANES, TK), lambda b, s: (b, 0, s, 0, 0))
    fk_shape = jax.ShapeDtypeStruct((n_tok, 2 * GROUP), BF16)
    fk_spec = pl.BlockSpec((TM, 2 * GROUP), lambda b, s: (b * ns + s, 0))
    fqT, fk, fvT, dqT, dk, dvT = pl.pallas_call(
        _inproj_kernel,
        grid=(bsz, ns),
        in_specs=[pl.BlockSpec((TM, d), lambda b, s: (b * ns + s, 0)), _full((1, d)),
                  w_spec_t, w_spec, w_spec_t, w_spec_t, w_spec, w_spec_t, _full((d, LANES)), _full((1, LANES)),
                  _full((GROUP, TM)), _full((1, GROUP)), _full((GROUP, TM)), _full((1, GROUP)),
                  _full((GROUP // 2, GROUP // 2)), _full((TM, TM)), _full((N_SPLIT * LANES, GROUP))],
        out_specs=[qT_spec, fk_spec, vT_spec, qT_spec, k_spec, vT_spec],
        out_shape=[qT_shape, fk_shape, vT_shape, qT_shape, k_shape, vT_shape],
        scratch_shapes=[pltpu.VMEM((1, LANES), F32)],
        compiler_params=_params(2),
        name="inproj",
    )(x2, prm['g_attn'][None, :], wfqT, wfk, wfvT, wdqT, wdk, wdvT, wf, bf, gfqT, gfk, gdqT, gdk, seg, tri_incl,
      place)

    n_hb = GROUP // LANES
    att_q = pl.BlockSpec((None, LANES, seq), lambda b, h: (b, h, 0))
    att_k = pl.BlockSpec((seq, LANES), lambda b, h: (b, h))
    att_v = pl.BlockSpec((None, None, nk, LANES, TK), lambda b, h: (b, h, 0, 0, 0))
    att_o = pl.BlockSpec((seq, LANES), lambda b, h: (b, h))
    fox_out = pl.pallas_call(
        _fox_kernel,
        grid=(bsz, n_hb),
        in_specs=[att_q, pl.BlockSpec((seq, 2 * LANES), lambda b, h: (b, h)), att_v],
        out_specs=att_o,
        out_shape=jax.ShapeDtypeStruct((n_tok, GROUP), BF16),
        compiler_params=_params(2),
        name="fox_attention",
    )(fqT, fk, fvT)

    kpos = jnp.arange(TK)[:, None]
    qpos = jnp.arange(TQ)[None, :]
    rel = jnp.stack([qpos - kpos, qpos - kpos + TK])
    assert TK + 1 >= MAX_DISTANCE and TQ == TK
    table = (rel_bias.astype(F32) - rel_bias[NUM_BUCKETS - 1].astype(F32)[None, :]) * LOG2E
    onehot = _t5_bucket(rel)[None, ..., None] == jnp.arange(NUM_BUCKETS, dtype=jnp.int32)
    biasT = jnp.sum(jnp.where(onehot, table.T[:, None, None, None, :], 0.0), axis=-1)
    biasT = jnp.where((rel >= 0)[None], biasT, NEG_INF)
    lam_init = 0.8 - 0.6 * math.exp(-0.3 * i)
    lam_p = jnp.stack([prm['lambda_q1'], prm['lambda_k1'], prm['lambda_q2'], prm['lambda_k2']]).astype(F32)
    gsubT = jnp.broadcast_to((prm['g_subln'] * (1.0 - lam_init))[:, None], (LANES, TQ))
    diff_out = pl.pallas_call(
        functools.partial(_diff_kernel, lam_init),
        grid=(bsz, N_DIFF_HEADS),
        in_specs=[att_q, att_k, att_v, pl.BlockSpec((None, 2, TK, TQ), lambda b, h: (h, 0, 0, 0)),
                  _full((4, HEAD_DIM)), _full((LANES, TQ))],
        out_specs=att_o,
        out_shape=jax.ShapeDtypeStruct((n_tok, GROUP), BF16),
        compiler_params=_params(2),
        name="diff_attention",
    )(dqT, dk, dvT, biasT, lam_p, gsubT)

    w_out = prm['w_out'].astype(BF16)
    wr = jnp.zeros((d, LANES), F32).at[:, :N_EXPERTS].set(prm['w_router'])
    wr_hi = wr.astype(BF16)
    wr_lo = (wr - wr_hi.astype(F32)).astype(BF16)
    br = jnp.full((1, LANES), NEG_INF, F32).at[0, :N_EXPERTS].set(prm['b_router'])
    row_spec = lambda w: pl.BlockSpec((TM, w), lambda t: (t, 0))
    h1, xm, topi, gates, counts = pl.pallas_call(
        _mix_kernel,
        grid=(n_tok // TM,),
        in_specs=[row_spec(d), row_spec(GROUP), row_spec(GROUP), _full((GROUP, d)), _full((GROUP, d)),
                  _full((1, d)), _full((d, LANES)), _full((d, LANES)), _full((1, LANES))],
        out_specs=[row_spec(d), pl.BlockSpec((TM, 1, d), lambda t: (t, 0, 0)), row_spec(LANES),
                   row_spec(LANES), _full((1, LANES))],
        out_shape=[jax.ShapeDtypeStruct((n_tok, d), F32), jax.ShapeDtypeStruct((n_tok, 1, d), F32),
                   jax.ShapeDtypeStruct((n_tok, LANES), jnp.int32), jax.ShapeDtypeStruct((n_tok, LANES), F32),
                   jax.ShapeDtypeStruct((1, LANES), F32)],
        scratch_shapes=[pltpu.VMEM((1, LANES), F32)],
        compiler_params=_params(1),
        name="mix_router",
    )(x2, fox_out, diff_out, w_out[:GROUP], w_out[GROUP:], prm['g_mlp'][None, :], wr_hi, wr_lo, br)

    m = n_tok * TOP_K
    n_blocks = (m + N_EXPERTS * MOE_BLOCK + MOE_BLOCK - 1) // MOE_BLOCK
    cnt = counts[0, :N_EXPERTS].astype(jnp.int32)
    padded = ((cnt + MOE_BLOCK - 1) // MOE_BLOCK) * MOE_BLOCK
    pad_ends = jnp.cumsum(padded)
    pad_starts = pad_ends - padded
    block_start = jnp.arange(n_blocks, dtype=jnp.int32) * MOE_BLOCK
    block_expert = jnp.minimum(jnp.sum(pad_ends[None, :] <= block_start[:, None], axis=1),
                               N_EXPERTS - 1).astype(jnp.int32)
    n_used = (pad_ends[-1] // MOE_BLOCK).astype(jnp.int32)[None]
    experts = jnp.arange(N_EXPERTS, dtype=jnp.int32)
    real = topi[:, :TOP_K] * (2 * m) + jnp.arange(m, dtype=jnp.int32).reshape(n_tok, TOP_K)
    fill = jnp.arange(IDX_ALIGN - 1, dtype=jnp.int32)[None, :]
    need = (-cnt) % IDX_ALIGN
    filler = jnp.where(fill < need[:, None], experts[:, None] * (2 * m) + m + fill,
                       N_EXPERTS * (2 * m) + experts[:, None] * IDX_ALIGN + fill)
    keys = jnp.sort(jnp.concatenate([real.reshape(m), filler.reshape(-1)]), stable=False)
    flat = jnp.concatenate([(keys % (2 * m)) % m, jnp.zeros((MOE_BLOCK,), jnp.int32)])
    grouped_tok = flat // TOP_K
    grouped_row = (flat % TOP_K) * n_tok + grouped_tok
    onehot_e = block_expert[:, None] == experts[None, :]
    pick = lambda v: jnp.sum(jnp.where(onehot_e, v[None, :], 0), axis=1)
    in_group = block_start - pick(pad_starts)
    n_valid = jnp.clip(pick(cnt) - in_group, 0, MOE_BLOCK)
    group_start = jnp.cumsum(cnt + need) - (cnt + need)
    first = jnp.clip(pick(group_start) + in_group, 0, keys.shape[0]) // IDX_ALIGN * IDX_ALIGN

    d_ff = prm['w2'].shape[1]
    n_grp = 2 * d_ff // (2 * LANES)
    b1p = prm['b1'].reshape(N_EXPERTS, n_grp, LANES, 2).transpose(0, 1, 3, 2).reshape(N_EXPERTS, 1, 2 * d_ff)
    b2 = prm['b2'].reshape(N_EXPERTS, 1, d)
    pj = jnp.arange(LANES)
    perm = (jnp.zeros((2 * LANES, 2 * LANES), BF16).at[2 * pj, pj].set(1.0)
            .at[2 * pj + 1, LANES + pj].set(1.0))
    any_spec = pl.BlockSpec(memory_space=pl.ANY)
    ys = pl.pallas_call(
        _expert_kernel,
        grid_spec=pltpu.PrefetchScalarGridSpec(
            num_scalar_prefetch=4,
            grid=(n_blocks,),
            in_specs=[any_spec, any_spec, any_spec,
                      pl.BlockSpec((None, d, 2 * d_ff), lambda t, be, *_:(be[t], 0, 0)),
                      pl.BlockSpec((None, 1, 2 * d_ff), lambda t, be, *_:(be[t], 0, 0)),
                      pl.BlockSpec((None, d_ff, d), lambda t, be, *_:(be[t], 0, 0)),
                      pl.BlockSpec((None, 1, d), lambda t, be, *_:(be[t], 0, 0)),
                      pl.BlockSpec((2 * LANES, 2 * LANES), lambda t, be, *_:(0, 0))],
            out_specs=any_spec,
            scratch_shapes=[pltpu.SMEM((2, MOE_BLOCK), jnp.int32), pltpu.SMEM((2, MOE_BLOCK), jnp.int32)]
            + [pltpu.VMEM((MOE_BLOCK, d), F32)] * 4
            + [pltpu.VMEM((d, 2 * d_ff), BF16), pltpu.VMEM((d_ff, d), BF16), pltpu.VMEM((MOE_BLOCK, 2 * d_ff), F32)]
            + [pltpu.SemaphoreType.DMA((2, 2)), pltpu.SemaphoreType.DMA((2,)), pltpu.SemaphoreType.DMA((2,))],
        ),
        out_shape=jax.ShapeDtypeStruct((m + MOE_BLOCK, 1, d), F32),
        compiler_params=pltpu.CompilerParams(dimension_semantics=("arbitrary",),
                                             vmem_limit_bytes=EXPERT_VMEM_LIMIT),
        name="experts",
    )(block_expert, n_used, first, n_valid, grouped_tok, grouped_row, xm, prm['w1'], b1p, prm['w2'], b2, perm)

    ple = p_i.shape[-1]
    out = pl.pallas_call(
        _final_kernel,
        grid=(n_tok // TM,),
        in_specs=[row_spec(d)]
        + [pl.BlockSpec((TM, 1, d), functools.partial(lambda k, t: (k * (n_tok // TM) + t, 0, 0), k))
           for k in range(TOP_K)]
        + [row_spec(LANES),
                  _full((1, d)), _full((d, d)), row_spec(ple), _full((ple, d))],
        out_specs=row_spec(d),
        out_shape=jax.ShapeDtypeStruct((n_tok, d), F32),
        compiler_params=_params(1),
        name="combine_ple",
    )(h1, ys, ys, ys, ys, gates, prm['g_ple'][None, :],
      prm['w_ple_gate'].astype(BF16), p_i.reshape(n_tok, ple), prm['w_ple_proj'].astype(BF16))
    return out.reshape(bsz, seq, d)


def kernel(x, p, rel_bias, g_attn, w_in, b_f, g_fox_q, g_fox_k, g_diff_q, g_diff_k, lambda_q1, lambda_k1,
           lambda_q2, lambda_k2, g_subln, w_out, g_mlp, w_router, b_router, w1, b1, w2, b2, g_ple,
           w_ple_gate, w_ple_proj):
    stacked = dict(g_attn=g_attn, w_in=w_in, b_f=b_f, g_fox_q=g_fox_q, g_fox_k=g_fox_k, g_diff_q=g_diff_q,
                   g_diff_k=g_diff_k, lambda_q1=lambda_q1, lambda_k1=lambda_k1, lambda_q2=lambda_q2,
                   lambda_k2=lambda_k2, g_subln=g_subln, w_out=w_out, g_mlp=g_mlp, w_router=w_router,
                   b_router=b_router, w1=w1, b1=b1, w2=w2, b2=b2, g_ple=g_ple, w_ple_gate=w_ple_gate,
                   w_ple_proj=w_ple_proj)
    h = x
    for i in range(p.shape[0]):
        h = _layer(i, h, p[i], rel_bias, {name: v[i] for name, v in stacked.items()})
    return h
```
